```python
import jax
import jax.numpy as jnp
from jax import lax
import numpy as np

D_MODEL = 1024
BATCH = 8
SEQ = 4096
DEPTH = 2

CHUNK = 64
Q_BLOCK = 128
HEAD_DIM = 64
NEG_INF = -1e30
LN_EPS = 1e-5
RMS_EPS = 1e-6

MLA_HEADS = 8
MLA_Q_RANK = 384
MLA_KV_RANK = 256
MLA_NOPE = 64
MLA_ROPE = 32
MLA_V = 64
ROPE_THETA = 10000.0

SWA_HEADS = 8
SWA_KV_HEADS = 2
SWA_WINDOW = 128
SWA_LEFT_CHUNKS = SWA_WINDOW // CHUNK

FOX_HEADS = 8

CK_HEADS = 8
CK_LEFT_CHUNKS = 8
REL_MAX = 256
REL_TABLE = REL_MAX + CHUNK

D_FF = -(-8 * D_MODEL // (3 * 256)) * 256

DEEPNORM_ALPHA = (2 * DEPTH) ** 0.25
DEEPNORM_BETA = (8 * DEPTH) ** -0.25

AB_SPLITS = (MLA_Q_RANK, MLA_KV_RANK, MLA_ROPE, SWA_HEADS * HEAD_DIM, SWA_KV_HEADS * HEAD_DIM, SWA_KV_HEADS * HEAD_DIM)
CD_SPLITS = (FOX_HEADS * HEAD_DIM,) * 3 + (FOX_HEADS,) + (CK_HEADS * HEAD_DIM,) * 3
AB_MIX = MLA_HEADS * MLA_V + SWA_HEADS * HEAD_DIM
CD_MIX = (FOX_HEADS + CK_HEADS) * HEAD_DIM

kernel_name = 'hybrid_chunk_causal_mla_swa_fox_relpos_deepnorm'


def _split_cols(x, widths):
    return jnp.split(x, [int(i) for i in np.cumsum(widths)[:-1]], axis=-1)


def _layer_norm(x, g, b):
    xf = x.astype(jnp.float32)
    mu = jnp.mean(xf, -1, keepdims=True)
    var = jnp.mean(jnp.square(xf - mu), -1, keepdims=True)
    y = (xf - mu) * lax.rsqrt(var + LN_EPS) * g.astype(jnp.float32) + b.astype(jnp.float32)
    return y.astype(x.dtype)


def _rms_norm(x, g):
    xf = x.astype(jnp.float32)
    y = xf * lax.rsqrt(jnp.mean(xf * xf, -1, keepdims=True) + RMS_EPS)
    return (y * g.astype(jnp.float32)).astype(x.dtype)


def _rope_tables(S):
    inv = ROPE_THETA ** (-jnp.arange(0, MLA_ROPE, 2, dtype=jnp.float32) / MLA_ROPE)
    ang = jnp.arange(S, dtype=jnp.float32)[:, None] * inv[None, :]
    return jnp.cos(ang), jnp.sin(ang)


def _apply_rope(x, cos, sin):
    x1, x2 = jnp.split(x.astype(jnp.float32), 2, axis=-1)
    return jnp.concatenate([x1 * cos - x2 * sin, x1 * sin + x2 * cos], -1).astype(x.dtype)


def _alibi_slopes(n):
    return jnp.exp2(-8.0 * jnp.arange(1, n + 1, dtype=jnp.float32) / n)


def _sweep_query_blocks(block_fn, *q_parts):
    B, S = q_parts[0].shape[:2]
    nb = S // Q_BLOCK
    blocked = tuple(jnp.moveaxis(p.reshape(B, nb, Q_BLOCK, *p.shape[2:]), 1, 0) for p in q_parts)
    out = lax.map(lambda a: block_fn(a[0], *a[1]), (jnp.arange(nb), blocked))
    return jnp.moveaxis(out, 0, 1).reshape(B, S, *out.shape[3:])


def _band(x, n_left):
    B, S = x.shape[:2]
    nc = S // CHUNK
    pad = [(0, 0), (n_left * CHUNK, 0)] + [(0, 0)] * (x.ndim - 2)
    xp = jnp.pad(x, pad).reshape(B, nc + n_left, CHUNK, *x.shape[2:])
    return jnp.concatenate([xp[:, j:j + nc] for j in range(n_left + 1)], axis=2)


def _band_geometry(nc, n_left):
    band = jnp.arange((n_left + 1) * CHUNK)
    key_off = band // CHUNK - n_left
    dist = jnp.arange(CHUNK)[:, None] - (key_off * CHUNK + band % CHUNK)[None, :]
    valid = (jnp.arange(nc)[:, None] + key_off[None, :]) >= 0
    return dist, valid


def _mla_attention(q_nope, q_rope, k_nope, k_rope, v):
    S = k_nope.shape[1]
    scale = (MLA_NOPE + MLA_ROPE) ** -0.5
    key_chunk = jnp.arange(S) // CHUNK

    def block(i, qn, qr):
        q_chunk = (i * Q_BLOCK + jnp.arange(Q_BLOCK)) // CHUNK
        logits = (jnp.einsum('bqhd,bkhd->bhqk', qn, k_nope, preferred_element_type=jnp.float32)
                  + jnp.einsum('bqhr,bkr->bhqk', qr, k_rope, preferred_element_type=jnp.float32)) * scale
        logits = jnp.where(key_chunk[None, :] <= q_chunk[:, None], logits, NEG_INF)
        p = jax.nn.softmax(logits, axis=-1).astype(v.dtype)
        return jnp.einsum('bhqk,bkhd->bqhd', p, v)

    return _sweep_query_blocks(block, q_nope, q_rope)


def _swa_sink_attention(q, k, v, sinks):
    B, S, _, d = q.shape
    nc = S // CHUNK
    G = SWA_HEADS // SWA_KV_HEADS
    qc = q.reshape(B, nc, CHUNK, SWA_KV_HEADS, G, d)
    kb, vb = _band(k, SWA_LEFT_CHUNKS), _band(v, SWA_LEFT_CHUNKS)
    dist, valid = _band_geometry(nc, SWA_LEFT_CHUNKS)
    alibi = -_alibi_slopes(SWA_HEADS).reshape(SWA_KV_HEADS, G, 1, 1) * jnp.abs(dist).astype(jnp.float32)
    logits = jnp.einsum('bcqkgd,bcskd->bckgqs', qc, kb, preferred_element_type=jnp.float32) * d ** -0.5 + alibi
    logits = jnp.where(valid[None, :, None, None, None, :], logits, NEG_INF)
    sink = jnp.broadcast_to(sinks.astype(jnp.float32).reshape(SWA_KV_HEADS, G, 1, 1), logits.shape[:-1] + (1,))
    p = jax.nn.softmax(jnp.concatenate([logits, sink], -1), axis=-1)[..., :-1].astype(v.dtype)
    return jnp.einsum('bckgqs,bcskd->bcqkgd', p, vb).reshape(B, S, SWA_HEADS, d)


def _forgetting_attention(q, k, v, f_logit, b_forget):
    S = k.shape[1]
    scale = q.shape[-1] ** -0.5
    cum_log_f = jnp.cumsum(jax.nn.log_sigmoid(f_logit.astype(jnp.float32) + b_forget.astype(jnp.float32)), axis=1)
    cum_k = jnp.swapaxes(cum_log_f, 1, 2)[:, :, None, :]
    key_pos = jnp.arange(S)

    def block(i, qb, cum_q):
        q_pos = i * Q_BLOCK + jnp.arange(Q_BLOCK)
        decay = jnp.swapaxes(cum_q, 1, 2)[..., None] - cum_k
        logits = jnp.einsum('bqhd,bkhd->bhqk', qb, k, preferred_element_type=jnp.float32) * scale + decay
        logits = jnp.where(key_pos[None, :] <= q_pos[:, None], logits, NEG_INF)
        p = jax.nn.softmax(logits, axis=-1).astype(v.dtype)
        return jnp.einsum('bhqk,bkhd->bqhd', p, v)

    return _sweep_query_blocks(block, q, cum_log_f)


def _chunk_relpos_attention(q, k, v, rel_bias):
    B, S, H, d = q.shape
    nc = S // CHUNK
    qc = q.reshape(B, nc, CHUNK, H, d)
    kb, vb = _band(k, CK_LEFT_CHUNKS), _band(v, CK_LEFT_CHUNKS)
    dist, valid = _band_geometry(nc, CK_LEFT_CHUNKS)
    idx = jnp.clip(dist, -(CHUNK - 1), REL_MAX) + (CHUNK - 1)
    bias = jnp.moveaxis(rel_bias[idx], -1, 0).astype(jnp.float32)
    logits = jnp.einsum('bcqhd,bcshd->bchqs', qc, kb, preferred_element_type=jnp.float32) * d ** -0.5 + bias
    logits = jnp.where(valid[None, :, None, None, :], logits, NEG_INF)
    p = jax.nn.softmax(logits, axis=-1).astype(v.dtype)
    return jnp.einsum('bchqs,bcshd->bcqhd', p, vb).reshape(B, S, H, d)


def _mixer_ab(h, w_in, q_norm, w_uq, kv_norm, w_ukv, sinks, w_out, cos, sin):
    B, S, _ = h.shape
    proj = jnp.einsum('bsd,df->bsf', h, w_in)
    c_q, c_kv, k_r, q_s, k_s, v_s = _split_cols(proj, AB_SPLITS)
    q = jnp.einsum('bsr,rf->bsf', _rms_norm(c_q, q_norm), w_uq).reshape(B, S, MLA_HEADS, MLA_NOPE + MLA_ROPE)
    q_nope = q[..., :MLA_NOPE]
    q_rope = _apply_rope(q[..., MLA_NOPE:], cos[None, :, None], sin[None, :, None])
    kv = jnp.einsum('bsr,rf->bsf', _rms_norm(c_kv, kv_norm), w_ukv).reshape(B, S, MLA_HEADS, MLA_NOPE + MLA_V)
    k_rope = _apply_rope(k_r, cos[None], sin[None])
    o_a = _mla_attention(q_nope, q_rope, kv[..., :MLA_NOPE], k_rope, kv[..., MLA_NOPE:])
    o_b = _swa_sink_attention(q_s.reshape(B, S, SWA_HEADS, HEAD_DIM),
                              k_s.reshape(B, S, SWA_KV_HEADS, HEAD_DIM),
                              v_s.reshape(B, S, SWA_KV_HEADS, HEAD_DIM), sinks)
    o = jnp.concatenate([o_a.reshape(B, S, -1), o_b.reshape(B, S, -1)], -1)
    return jnp.einsum('bsf,fd->bsd', o, w_out)


def _mixer_cd(h, w_in, b_forget, rel_bias, w_out):
    B, S, _ = h.shape
    proj = jnp.einsum('bsd,df->bsf', h, w_in)
    q_f, k_f, v_f, f_logit, q_c, k_c, v_c = _split_cols(proj, CD_SPLITS)
    o_c = _forgetting_attention(q_f.reshape(B, S, FOX_HEADS, HEAD_DIM), k_f.reshape(B, S, FOX_HEADS, HEAD_DIM),
                                v_f.reshape(B, S, FOX_HEADS, HEAD_DIM), f_logit, b_forget)
    o_d = _chunk_relpos_attention(q_c.reshape(B, S, CK_HEADS, HEAD_DIM), k_c.reshape(B, S, CK_HEADS, HEAD_DIM),
                                  v_c.reshape(B, S, CK_HEADS, HEAD_DIM), rel_bias)
    o = jnp.concatenate([o_c.reshape(B, S, -1), o_d.reshape(B, S, -1)], -1)
    return jnp.einsum('bsf,fd->bsd', o, w_out)


def _swiglu(x, w_gate, w_up, w_down):
    g = jnp.einsum('bsd,df->bsf', x, w_gate)
    u = jnp.einsum('bsd,df->bsf', x, w_up)
    return jnp.einsum('bsf,fd->bsd', jax.nn.silu(g) * u, w_down)


def setup_inputs(seed: int = 0) -> dict:
    key = jax.random.key(seed)
    ks = jax.random.split(key, 20)
    ne, no = (DEPTH + 1) // 2, DEPTH // 2
    f32 = jnp.float32

    def normal(k, shape, scale):
        return jax.random.normal(k, shape, f32) * scale

    def gain(k, shape):
        return 1.0 + normal(k, shape, 0.02)

    return {
        'x': normal(ks[0], (BATCH, SEQ, D_MODEL), 1.0),
        'ab_w_in': normal(ks[1], (ne, D_MODEL, sum(AB_SPLITS)), D_MODEL ** -0.5),
        'ab_q_norm': gain(ks[2], (ne, MLA_Q_RANK)),
        'ab_w_uq': normal(ks[3], (ne, MLA_Q_RANK, MLA_HEADS * (MLA_NOPE + MLA_ROPE)), MLA_Q_RANK ** -0.5),
        'ab_kv_norm': gain(ks[4], (ne, MLA_KV_RANK)),
        'ab_w_ukv': normal(ks[5], (ne, MLA_KV_RANK, MLA_HEADS * (MLA_NOPE + MLA_V)), MLA_KV_RANK ** -0.5),
        'ab_sinks': normal(ks[6], (ne, SWA_HEADS), 0.5),
        'ab_w_out': normal(ks[7], (ne, AB_MIX, D_MODEL), DEEPNORM_BETA * AB_MIX ** -0.5),
        'cd_w_in': normal(ks[8], (no, D_MODEL, sum(CD_SPLITS)), D_MODEL ** -0.5),
        'cd_b_forget': jax.random.uniform(ks[9], (no, FOX_HEADS), dtype=f32, minval=1.0, maxval=5.0),
        'cd_rel_bias': normal(ks[10], (no, REL_TABLE, CK_HEADS), 0.2),
        'cd_w_out': normal(ks[11], (no, CD_MIX, D_MODEL), DEEPNORM_BETA * CD_MIX ** -0.5),
        'ln1_g': gain(ks[12], (DEPTH, D_MODEL)),
        'ln1_b': normal(ks[13], (DEPTH, D_MODEL), 0.02),
        'ffn_w_gate': normal(ks[14], (DEPTH, D_MODEL, D_FF), D_MODEL ** -0.5),
        'ffn_w_up': normal(ks[15], (DEPTH, D_MODEL, D_FF), D_MODEL ** -0.5),
        'ffn_w_down': normal(ks[16], (DEPTH, D_FF, D_MODEL), DEEPNORM_BETA * D_FF ** -0.5),
        'ln2_g': gain(ks[17], (DEPTH, D_MODEL)),
        'ln2_b': normal(ks[18], (DEPTH, D_MODEL), 0.02),
    }


def reference(x, ab_w_in, ab_q_norm, ab_w_uq, ab_kv_norm, ab_w_ukv, ab_sinks, ab_w_out,
              cd_w_in, cd_b_forget, cd_rel_bias, cd_w_out,
              ln1_g, ln1_b, ffn_w_gate, ffn_w_up, ffn_w_down, ln2_g, ln2_b):
    cos, sin = _rope_tables(x.shape[1])
    for layer in range(DEPTH):
        i = layer // 2
        if layer % 2 == 0:
            mix = _mixer_ab(x, ab_w_in[i], ab_q_norm[i], ab_w_uq[i], ab_kv_norm[i], ab_w_ukv[i],
                            ab_sinks[i], ab_w_out[i], cos, sin)
        else:
            mix = _mixer_cd(x, cd_w_in[i], cd_b_forget[i], cd_rel_bias[i], cd_w_out[i])
        x = _layer_norm(DEEPNORM_ALPHA * x + mix, ln1_g[layer], ln1_b[layer])
        ffn = _swiglu(x, ffn_w_gate[layer], ffn_w_up[layer], ffn_w_down[layer])
        x = _layer_norm(DEEPNORM_ALPHA * x + ffn, ln2_g[layer], ln2_b[layer])
    return x
```

```python
import functools
import math

import numpy as np
import jax
import jax.numpy as jnp
from jax import lax
from jax.experimental import pallas as pl
from jax.experimental.pallas import tpu as pltpu

D_MODEL = 1024
CHUNK = 64
HEAD_DIM = 64
N_HEADS = 8
LANES = 128
LN_EPS = 1e-5
RMS_EPS = 1e-6
MASK = -1e30
LOG2E = math.log2(math.e)

MLA_Q_RANK = 384
MLA_KV_RANK = 256
MLA_NOPE = 64
MLA_ROPE = 32
ROPE_THETA = 10000.0
SWA_KV_HEADS = 2
SWA_LEFT_CHUNKS = 2
CK_LEFT_CHUNKS = 8
REL_MAX = 256
DEPTH = 2
DEEPNORM_ALPHA = (2 * DEPTH) ** 0.25

TQ = 256
TK = 256
TM = 512
VMEM_LIMIT = 56 * 1024 * 1024

BF16 = jnp.bfloat16
F32 = jnp.float32


def _dot(a, b):
    return jnp.dot(a, b, preferred_element_type=F32)


def _dot_nt(a, b):
    return lax.dot_general(a, b, (((1,), (1,)), ((), ())), preferred_element_type=F32)


def _layer_norm(y, g, b):
    mu = jnp.mean(y, axis=-1, keepdims=True)
    yc = y - mu
    var = jnp.mean(yc * yc, axis=-1, keepdims=True)
    return yc * lax.rsqrt(var + LN_EPS) * g + b


def _rms_norm(c, g):
    return c * lax.rsqrt(jnp.mean(c * c, axis=-1, keepdims=True) + RMS_EPS) * g


def _const_spec(shape):
    zeros = (0,) * len(shape)
    return pl.BlockSpec(shape, lambda *_: zeros, pipeline_mode=pl.Buffered(1))


_AB_CQ, _AB_CKV, _AB_QS, _AB_KS, _AB_VS, _AB_KR, _AB_COLS = 0, 384, 640, 1152, 1280, 1408, 1536


def _rope_mix(t, m1, m2):
    return t * m1 + pltpu.roll(t, 96, 1) * m2


def _proj_ab_kernel(x_ref, win_ref, qn_ref, wuq_ref, kvn_ref, wukv_ref,
                    m1q_ref, m2q_ref, m1k_ref, m2k_ref,
                    qa_ref, ka_ref, va_ref, qs_ref, ks_ref, vs_ref):
    proj = _dot(x_ref[...].astype(BF16), win_ref[...])
    c_q = _rms_norm(proj[:, _AB_CQ:_AB_CKV], qn_ref[...])
    qf = _dot(c_q.astype(BF16), wuq_ref[...])
    c_kv = _rms_norm(proj[:, _AB_CKV:_AB_QS], kvn_ref[...])
    kvf = _dot(c_kv.astype(BF16), wukv_ref[...])
    k_rope = _rope_mix(proj[:, _AB_KR:_AB_COLS], m1k_ref[...], m2k_ref[...])
    m1q, m2q = m1q_ref[...], m2q_ref[...]
    for h in range(N_HEADS):
        sl = slice(h * LANES, (h + 1) * LANES)
        qa_ref[:, sl] = _rope_mix(qf[:, sl], m1q, m2q).astype(BF16)
        ka_ref[:, sl] = (kvf[:, sl] + k_rope).astype(BF16)
    va_ref[...] = kvf[:, N_HEADS * LANES:].astype(BF16)
    qs_ref[...] = (proj[:, _AB_QS:_AB_KS] * (HEAD_DIM ** -0.5 * LOG2E)).astype(BF16)
    ks_ref[...] = proj[:, _AB_KS:_AB_VS].astype(BF16)
    vs_ref[...] = proj[:, _AB_VS:_AB_KR].astype(BF16)


def _proj_ab(x2, w_in, q_norm, w_uq, kv_norm, w_ukv, seq):
    t = x2.shape[0]
    nsb = seq // TM
    c_q, c_kv, k_r, q_s, k_s, v_s = jnp.split(w_in, [384, 640, 672, 1184, 1312], axis=1)
    q_s = q_s.reshape(D_MODEL, 2, 4, HEAD_DIM).transpose(0, 2, 1, 3).reshape(D_MODEL, 512)
    kr_blk = jnp.concatenate([jnp.zeros((D_MODEL, 64), F32), k_r, k_r[:, 16:], k_r[:, :16]], axis=1)
    win_p = jnp.concatenate([c_q, c_kv, q_s, k_s, v_s, kr_blk], axis=1).astype(BF16)
    wq = w_uq.reshape(MLA_Q_RANK, N_HEADS, MLA_NOPE + MLA_ROPE)
    wuq_p = jnp.concatenate([wq, wq[..., 80:96], wq[..., 64:80]], axis=-1)
    wuq_p = wuq_p.reshape(MLA_Q_RANK, N_HEADS * LANES).astype(BF16)
    wkv = w_ukv.reshape(MLA_KV_RANK, N_HEADS, 2 * HEAD_DIM)
    k_pad = jnp.concatenate([wkv[..., :64], jnp.zeros_like(wkv[..., :64])], axis=-1)
    wukv_p = jnp.concatenate([k_pad.reshape(MLA_KV_RANK, N_HEADS * LANES),
                              wkv[..., 64:].reshape(MLA_KV_RANK, 512)], axis=1).astype(BF16)
    inv = ROPE_THETA ** (-jnp.arange(0, MLA_ROPE, 2, dtype=F32) / MLA_ROPE)
    ang = jnp.arange(seq, dtype=F32)[:, None] * inv[None, :]
    cos, sin = jnp.cos(ang), jnp.sin(ang)
    z64, z32, o64 = jnp.zeros((seq, 64), F32), jnp.zeros((seq, 32), F32), jnp.ones((seq, 64), F32)
    m1k = jnp.concatenate([z64, cos, cos, z32], axis=1)
    m2k = jnp.concatenate([z64, -sin, sin, z32], axis=1)
    q_scale = (MLA_NOPE + MLA_ROPE) ** -0.5 * LOG2E
    m1q = jnp.concatenate([o64, cos, cos, z32], axis=1) * q_scale
    m2q = m2k * q_scale

    row = lambda w: pl.BlockSpec((TM, w), lambda i: (i, 0))
    tab = pl.BlockSpec((TM, LANES), lambda i: (i % nsb, 0))
    out_w = (1024, 1024, 512, 512, 128, 128)
    return pl.pallas_call(
        _proj_ab_kernel,
        grid=(t // TM,),
        in_specs=[row(D_MODEL), _const_spec((D_MODEL, _AB_COLS)), _const_spec((1, MLA_Q_RANK)),
                  _const_spec((MLA_Q_RANK, 1024)), _const_spec((1, MLA_KV_RANK)),
                  _const_spec((MLA_KV_RANK, 1536)), tab, tab, tab, tab],
        out_specs=[row(w) for w in out_w],
        out_shape=[jax.ShapeDtypeStruct((t, w), BF16) for w in out_w],
        compiler_params=pltpu.CompilerParams(dimension_semantics=("arbitrary",),
                                             vmem_limit_bytes=VMEM_LIMIT),
        name="proj_ab",
    )(x2, win_p, q_norm.reshape(1, -1), wuq_p, kv_norm.reshape(1, -1), wukv_p, m1q, m2q, m1k, m2k)


def _proj_cd_kernel(x_ref, win_ref, bf_ref, qkv_ref, cum_ref, carry_ref, *, nsb):
    @pl.when(pl.program_id(0) % nsb == 0)
    def _():
        carry_ref[...] = jnp.zeros_like(carry_ref)

    proj = _dot(x_ref[...].astype(BF16), win_ref[...])
    q_scale = HEAD_DIM ** -0.5 * LOG2E
    for blk, scale in enumerate((q_scale, None, None, q_scale, None, None)):
        part = proj[:, blk * 512:(blk + 1) * 512]
        qkv_ref[:, blk * 512:(blk + 1) * 512] = (part if scale is None else part * scale).astype(BF16)
    z = proj[:, 3072:] + bf_ref[...]
    log_f = jnp.minimum(z, 0.0) - jnp.log(1.0 + jnp.exp(-jnp.abs(z)))
    r = lax.broadcasted_iota(jnp.int32, (TM, TM), 0)
    c = lax.broadcasted_iota(jnp.int32, (TM, TM), 1)
    tri = jnp.where(c <= r, 1.0, 0.0).astype(BF16)
    hi = log_f.astype(BF16)
    rem = log_f - hi.astype(F32)
    mid = rem.astype(BF16)
    lo = (rem - mid.astype(F32)).astype(BF16)
    cum = _dot(tri, hi) + _dot(tri, mid) + _dot(tri, lo) + carry_ref[...]
    cum_ref[...] = cum
    carry_ref[...] = cum[TM - 1:TM, :]


def _proj_cd(x2, w_in, b_forget, seq):
    t = x2.shape[0]
    q_f, k_f, v_f, f_l, q_c, k_c, v_c = jnp.split(w_in, [512, 1024, 1536, 1544, 2056, 2568], axis=1)
    f_pad = jnp.concatenate([f_l, jnp.zeros((D_MODEL, LANES - N_HEADS), F32)], axis=1)
    win_p = jnp.concatenate([q_f, k_f, v_f, q_c, k_c, v_c, f_pad], axis=1).astype(BF16)
    bf = jnp.concatenate([b_forget, jnp.zeros((LANES - N_HEADS,), F32)]).reshape(1, LANES)
    return pl.pallas_call(
        functools.partial(_proj_cd_kernel, nsb=seq // TM),
        grid=(t // TM,),
        in_specs=[pl.BlockSpec((TM, D_MODEL), lambda i: (i, 0)),
                  _const_spec((D_MODEL, 3072 + LANES)), _const_spec((1, LANES))],
        out_specs=[pl.BlockSpec((TM, 3072), lambda i: (i, 0)),
                   pl.BlockSpec((TM, LANES), lambda i: (i, 0))],
        out_shape=[jax.ShapeDtypeStruct((t, 3072), BF16), jax.ShapeDtypeStruct((t, LANES), F32)],
        scratch_shapes=[pltpu.VMEM((1, LANES), F32)],
        compiler_params=pltpu.CompilerParams(dimension_semantics=("arbitrary",),
                                             vmem_limit_bytes=VMEM_LIMIT),
        name="proj_cd",
    )(x2, win_p, bf)


def _attn_kernel(*refs, variant, window):
    if variant == "mla":
        q_ref, k_ref, v_ref, o_ref, m_ref, l_ref, acc_ref = refs
    elif variant == "fox":
        q_ref, k_ref, v_ref, ck_ref, o_ref, m_ref, l_ref, acc_ref = refs
    elif variant == "swa":
        sink_ref, q_ref, k_ref, v_ref, bias_ref, o_ref, m_ref, l_ref, acc_ref = refs
    else:
        q_ref, k_ref, v_ref, bias_ref, o_ref, m_ref, l_ref, acc_ref = refs

    i = pl.program_id(2)
    lane = lax.broadcasted_iota(jnp.int32, (TQ, LANES), 1)
    q = q_ref[0]
    if variant == "mla":
        q_a, q_b = q[:, :LANES], q[:, LANES:]
    else:
        zero = jnp.zeros_like(q)
        q_pair = jnp.concatenate([jnp.where(lane < HEAD_DIM, q, zero),
                                  jnp.where(lane >= HEAD_DIM, q, zero)], axis=0)

    m_ref[...] = jnp.full_like(m_ref, MASK)
    l_ref[...] = jnp.zeros_like(l_ref)
    acc_ref[...] = jnp.zeros_like(acc_ref)

    def step(j, diagonal):
        start = pl.multiple_of(j * TK, TK)
        k = k_ref[0, pl.ds(start, TK), :]
        v = v_ref[0, pl.ds(start, TK), :]
        if variant == "mla":
            s = jnp.concatenate([_dot_nt(q_a, k[:, :LANES]), _dot_nt(q_b, k[:, LANES:])], axis=0)
        else:
            s = _dot_nt(q_pair, k)
        if variant == "fox":
            ck = ck_ref[0, 0, j]
            s = s - jnp.concatenate([jnp.broadcast_to(ck[0:1], (TQ, TK)),
                                     jnp.broadcast_to(ck[1:2], (TQ, TK))], axis=0)
        if variant in ("swa", "ck"):
            s = s + bias_ref[0, j - i + window]
        elif diagonal:
            row = lax.broadcasted_iota(jnp.int32, (2 * TQ, TK), 0) & (TQ - 1)
            col = lax.broadcasted_iota(jnp.int32, (2 * TQ, TK), 1)
            if variant == "mla":
                keep = (col >> 6) <= (row >> 6)
            else:
                keep = col <= row
            s = jnp.where(keep, s, MASK)
        m_prev = m_ref[...]
        m_new = jnp.maximum(m_prev, jnp.max(s, axis=1, keepdims=True))
        alpha = jnp.exp2(m_prev - m_new)
        p = jnp.exp2(s - m_new)
        l_ref[...] = alpha * l_ref[...] + jnp.sum(p, axis=1, keepdims=True)
        acc_ref[...] = alpha * acc_ref[...] + _dot(p.astype(BF16), v)
        m_ref[...] = m_new

    if variant in ("swa", "ck"):
        lax.fori_loop(jnp.maximum(i - window, 0), i + 1, lambda j, c: step(j, False), None)
    else:
        lax.fori_loop(0, i, lambda j, c: step(j, False), None)
        step(i, True)

    l = l_ref[...]
    acc = acc_ref[...]
    if variant == "swa":
        pair = pl.program_id(1)
        rows = lax.broadcasted_iota(jnp.int32, (2 * TQ, 1), 0)
        sink = jnp.where(rows < TQ, sink_ref[pair], sink_ref[pair + N_HEADS // 2])
        m = m_ref[...]
        m_fin = jnp.maximum(m, sink)
        scale = jnp.exp2(m - m_fin)
        l = l * scale + jnp.exp2(sink - m_fin)
        acc = acc * scale
    o = acc / l
    o_ref[0] = jnp.where(lane < HEAD_DIM, o[:TQ], o[TQ:]).astype(o_ref.dtype)


def _attention(variant, q, k, v, q_col, k_col, v_col, *, extra=None, sinks=None):
    b, s, _ = q.shape
    n_pairs = N_HEADS // 2
    nq = s // TQ
    window = {"mla": None, "fox": None, "swa": 1, "ck": 2}[variant]
    qw = 2 * LANES if variant == "mla" else LANES
    shared_kv = variant == "swa"

    def im(f):
        if sinks is None:
            return f
        return lambda bi, p, i, sink_ref: f(bi, p, i)

    in_specs = [
        pl.BlockSpec((1, TQ, qw), im(lambda bi, p, i: (bi, i, q_col + p))),
        pl.BlockSpec((1, s, qw), im(lambda bi, p, i: (bi, 0, k_col + (0 if shared_kv else p)))),
        pl.BlockSpec((1, s, LANES), im(lambda bi, p, i: (bi, 0, v_col + (0 if shared_kv else p)))),
    ]
    args = [q, k, v]
    if variant == "fox":
        in_specs.append(pl.BlockSpec((1, 1, s // TK, 2, TK), lambda bi, p, i: (bi, p, 0, 0, 0)))
        args.append(extra)
    elif variant in ("swa", "ck"):
        in_specs.append(pl.BlockSpec((1, window + 1, 2 * TQ, TK), im(lambda bi, p, i: (p, 0, 0, 0))))
        args.append(extra)
    out_spec = pl.BlockSpec((1, TQ, LANES), im(lambda bi, p, i: (bi, i, p)))
    scratch = [pltpu.VMEM((2 * TQ, 1), F32), pltpu.VMEM((2 * TQ, 1), F32),
               pltpu.VMEM((2 * TQ, LANES), F32)]
    kern = functools.partial(_attn_kernel, variant=variant, window=window)
    params = pltpu.CompilerParams(dimension_semantics=("arbitrary",) * 3, vmem_limit_bytes=VMEM_LIMIT)
    out_shape = jax.ShapeDtypeStruct((b, s, n_pairs * LANES), BF16)
    if sinks is not None:
        grid_spec = pltpu.PrefetchScalarGridSpec(
            num_scalar_prefetch=1, grid=(b, n_pairs, nq), in_specs=in_specs, out_specs=out_spec,
            scratch_shapes=scratch)
        return pl.pallas_call(kern, grid_spec=grid_spec, out_shape=out_shape,
                              compiler_params=params, name="attn_" + variant)(sinks, *args)
    return pl.pallas_call(kern, grid=(b, n_pairs, nq), in_specs=in_specs, out_specs=out_spec,
                          out_shape=out_shape, scratch_shapes=scratch,
                          compiler_params=params, name="attn_" + variant)(*args)


def _band_geometry(left_blocks):
    a = jnp.arange(TQ)[None, :, None]
    c = jnp.arange(TK)[None, None, :]
    r = jnp.arange(left_blocks + 1)[:, None, None]
    dist = a - c + (left_blocks - r) * TK
    q_chunk = left_blocks * (TK // CHUNK) + a // CHUNK
    k_chunk = r * (TK // CHUNK) + c // CHUNK
    return dist, q_chunk, k_chunk


def _swa_bias():
    dist, q_chunk, k_chunk = _band_geometry(1)
    valid = (k_chunk <= q_chunk) & (k_chunk >= q_chunk - SWA_LEFT_CHUNKS)
    slopes = jnp.exp2(-8.0 * jnp.arange(1, N_HEADS + 1, dtype=F32) / N_HEADS)
    bias = -slopes[:, None, None, None] * jnp.abs(dist).astype(F32)[None] * LOG2E
    bias = jnp.where(valid[None], bias, MASK)
    return jnp.concatenate([bias[:4], bias[4:]], axis=2)


def _ck_bias(rel_bias):
    dist, q_chunk, k_chunk = _band_geometry(2)
    valid = (k_chunk <= q_chunk) & (k_chunk >= q_chunk - CK_LEFT_CHUNKS)
    idx = jnp.clip(dist, -(CHUNK - 1), REL_MAX) + (CHUNK - 1)
    bias = jnp.moveaxis(rel_bias[idx], -1, 0).astype(F32) * LOG2E
    bias = jnp.where(valid[None], bias, MASK)
    return jnp.concatenate([bias[0::2], bias[1::2]], axis=2)


def _out_ffn_kernel(x_ref, oa_ref, ob_ref, woa_ref, wob_ref, g1_ref, b1_ref,
                    wg_ref, wu_ref, wd_ref, g2_ref, b2_ref, y_ref):
    mix = _dot(oa_ref[...], woa_ref[...]) + _dot(ob_ref[...], wob_ref[...])
    x1 = _layer_norm(DEEPNORM_ALPHA * x_ref[...] + mix, g1_ref[...], b1_ref[...])
    x1b = x1.astype(BF16)
    gate = _dot(x1b, wg_ref[...])
    up = _dot(x1b, wu_ref[...])
    hidden = (gate * jax.nn.sigmoid(gate) * up).astype(BF16)
    ffn = _dot(hidden, wd_ref[...])
    y_ref[...] = _layer_norm(DEEPNORM_ALPHA * x1 + ffn, g2_ref[...], b2_ref[...])


def _out_ffn(x2, o_a, o_b, w_out_a, w_out_b, g1, b1, w_gate, w_up, w_down, g2, b2):
    t = x2.shape[0]
    d_ff = w_gate.shape[1]
    row = lambda w: pl.BlockSpec((TM, w), lambda i: (i, 0))
    vec = _const_spec((1, D_MODEL))
    return pl.pallas_call(
        _out_ffn_kernel,
        grid=(t // TM,),
        in_specs=[row(D_MODEL), row(512), row(512), _const_spec((512, D_MODEL)),
                  _const_spec((512, D_MODEL)), vec, vec, _const_spec((D_MODEL, d_ff)),
                  _const_spec((D_MODEL, d_ff)), _const_spec((d_ff, D_MODEL)), vec, vec],
        out_specs=row(D_MODEL),
        out_shape=jax.ShapeDtypeStruct((t, D_MODEL), F32),
        compiler_params=pltpu.CompilerParams(dimension_semantics=("arbitrary",),
                                             vmem_limit_bytes=VMEM_LIMIT),
        name="out_ffn",
    )(x2, o_a, o_b, w_out_a.astype(BF16), w_out_b.astype(BF16), g1.reshape(1, -1), b1.reshape(1, -1),
      w_gate.astype(BF16), w_up.astype(BF16), w_down.astype(BF16), g2.reshape(1, -1), b2.reshape(1, -1))


def kernel(x, ab_w_in, ab_q_norm, ab_w_uq, ab_kv_norm, ab_w_ukv, ab_sinks, ab_w_out,
           cd_w_in, cd_b_forget, cd_rel_bias, cd_w_out,
           ln1_g, ln1_b, ffn_w_gate, ffn_w_up, ffn_w_down, ln2_g, ln2_b):
    b, s, d = x.shape
    assert d == D_MODEL and s % TM == 0 and TM % TQ == 0 and TQ == TK
    t = b * s
    x2 = x.reshape(t, d)

    qa, ka, va, qs, ks, vs = _proj_ab(x2, ab_w_in[0], ab_q_norm[0], ab_w_uq[0], ab_kv_norm[0],
                                      ab_w_ukv[0], s)
    r3 = lambda a: a.reshape(b, s, a.shape[-1])
    o_a = _attention("mla", r3(qa), r3(ka), r3(va), 0, 0, 0)
    o_b = _attention("swa", r3(qs), r3(ks), r3(vs), 0, 0, 0, extra=_swa_bias(),
                     sinks=ab_sinks[0] * LOG2E)
    w_out = ab_w_out[0]
    w_out_b = w_out[512:].reshape(2, 4, HEAD_DIM, D_MODEL).transpose(1, 0, 2, 3).reshape(512, D_MODEL)
    x2 = _out_ffn(x2, o_a.reshape(t, 512), o_b.reshape(t, 512), w_out[:512], w_out_b,
                  ln1_g[0], ln1_b[0], ffn_w_gate[0], ffn_w_up[0], ffn_w_down[0], ln2_g[0], ln2_b[0])

    qkv, cum = _proj_cd(x2, cd_w_in[0], cd_b_forget[0], s)
    qkv = qkv.reshape(b, s, 3072)
    ck = (cum[:, :N_HEADS] * LOG2E).reshape(b, s // TK, TK, N_HEADS // 2, 2).transpose(0, 3, 1, 4, 2)
    o_c = _attention("fox", qkv, qkv, qkv, 0, 4, 8, extra=ck)
    o_d = _attention("ck", qkv, qkv, qkv, 12, 16, 20, extra=_ck_bias(cd_rel_bias[0]))
    w_out = cd_w_out[0]
    x2 = _out_ffn(x2, o_c.reshape(t, 512), o_d.reshape(t, 512), w_out[:512], w_out[512:],
                  ln1_g[1], ln1_b[1], ffn_w_gate[1], ffn_w_up[1], ffn_w_down[1], ln2_g[1], ln2_b[1])
    return x2.reshape(b, s, d)
```

```python
import functools
import math

import numpy as np
import jax
import jax.numpy as jnp
from jax import lax
from jax.experimental import pallas as pl
from jax.experimental.pallas import tpu as pltpu

D_MODEL = 1024
CHUNK = 64
HEAD_DIM = 64
N_HEADS = 8
LANES = 128
LN_EPS = 1e-5
RMS_EPS = 1e-6
MASK = -1e30
LOG2E = math.log2(math.e)

MLA_Q_RANK = 384
MLA_KV_RANK = 256
MLA_NOPE = 64
MLA_ROPE = 32
ROPE_THETA = 10000.0
SWA_KV_HEADS = 2
SWA_LEFT_CHUNKS = 2
CK_LEFT_CHUNKS = 8
REL_MAX = 256
DEPTH = 2
DEEPNORM_ALPHA = (2 * DEPTH) ** 0.25

TQ = 256
TK = 256
FULL_TILE = 512
TM = 512
VT_ROWS = LANES + 16
VMEM_LIMIT = 56 * 1024 * 1024

BF16 = jnp.bfloat16
F32 = jnp.float32


def _dot(a, b):
    return jnp.dot(a, b, preferred_element_type=F32)


def _layer_norm(y, g, b):
    mu = jnp.mean(y, axis=-1, keepdims=True)
    yc = y - mu
    var = jnp.mean(yc * yc, axis=-1, keepdims=True)
    return yc * lax.rsqrt(var + LN_EPS) * g + b


def _rms_norm(c, g):
    return c * lax.rsqrt(jnp.mean(c * c, axis=-1, keepdims=True) + RMS_EPS) * g


def _const_spec(shape):
    zeros = (0,) * len(shape)
    return pl.BlockSpec(shape, lambda *_: zeros, pipeline_mode=pl.Buffered(1))


_AB_CQ, _AB_CKV, _AB_QS, _AB_KS, _AB_VS, _AB_KR, _AB_COLS = 0, 384, 640, 1152, 1280, 1408, 1536


def _rope_mix(t, m1, m2):
    return t * m1 + pltpu.roll(t, 96, 1) * m2


def _proj_ab_kernel(x_ref, win_ref, qn_ref, wuq_ref, kvn_ref, wukv_ref,
                    m1q_ref, m2q_ref, m1k_ref, m2k_ref,
                    qa_ref, ka_ref, va_ref, qs_ref, ks_ref, vs_ref):
    proj = _dot(x_ref[...].astype(BF16), win_ref[...])
    c_q = _rms_norm(proj[:, _AB_CQ:_AB_CKV], qn_ref[...])
    qf = _dot(c_q.astype(BF16), wuq_ref[...])
    c_kv = _rms_norm(proj[:, _AB_CKV:_AB_QS], kvn_ref[...])
    kvf = _dot(c_kv.astype(BF16), wukv_ref[...])
    k_rope = _rope_mix(proj[:, _AB_KR:_AB_COLS], m1k_ref[...], m2k_ref[...])
    m1q, m2q = m1q_ref[...], m2q_ref[...]
    for h in range(N_HEADS):
        sl = slice(h * LANES, (h + 1) * LANES)
        qa_ref[:, sl] = _rope_mix(qf[:, sl], m1q, m2q).astype(BF16)
        ka_ref[:, sl] = (kvf[:, sl] + k_rope).astype(BF16)
    va_ref[...] = kvf[:, N_HEADS * LANES:].astype(BF16)
    qs_ref[...] = (proj[:, _AB_QS:_AB_KS] * (HEAD_DIM ** -0.5 * LOG2E)).astype(BF16)
    ks_ref[...] = proj[:, _AB_KS:_AB_VS].astype(BF16)
    vs_ref[...] = proj[:, _AB_VS:_AB_KR].astype(BF16)


def _proj_ab(x2, w_in, q_norm, w_uq, kv_norm, w_ukv, seq):
    t = x2.shape[0]
    nsb = seq // TM
    c_q, c_kv, k_r, q_s, k_s, v_s = jnp.split(w_in, [384, 640, 672, 1184, 1312], axis=1)
    q_s = q_s.reshape(D_MODEL, 2, 4, HEAD_DIM).transpose(0, 2, 1, 3).reshape(D_MODEL, 512)
    kr_blk = jnp.concatenate([jnp.zeros((D_MODEL, 64), F32), k_r, k_r[:, 16:], k_r[:, :16]], axis=1)
    win_p = jnp.concatenate([c_q, c_kv, q_s, k_s, v_s, kr_blk], axis=1).astype(BF16)
    wq = w_uq.reshape(MLA_Q_RANK, N_HEADS, MLA_NOPE + MLA_ROPE)
    wuq_p = jnp.concatenate([wq, wq[..., 80:96], wq[..., 64:80]], axis=-1)
    wuq_p = wuq_p.reshape(MLA_Q_RANK, N_HEADS * LANES).astype(BF16)
    wkv = w_ukv.reshape(MLA_KV_RANK, N_HEADS, 2 * HEAD_DIM)
    k_pad = jnp.concatenate([wkv[..., :64], jnp.zeros_like(wkv[..., :64])], axis=-1)
    wukv_p = jnp.concatenate([k_pad.reshape(MLA_KV_RANK, N_HEADS * LANES),
                              wkv[..., 64:].reshape(MLA_KV_RANK, 512)], axis=1).astype(BF16)
    inv = ROPE_THETA ** (-jnp.arange(0, MLA_ROPE, 2, dtype=F32) / MLA_ROPE)
    ang = jnp.arange(seq, dtype=F32)[:, None] * inv[None, :]
    cos, sin = jnp.cos(ang), jnp.sin(ang)
    z64, z32, o64 = jnp.zeros((seq, 64), F32), jnp.zeros((seq, 32), F32), jnp.ones((seq, 64), F32)
    m1k = jnp.concatenate([z64, cos, cos, z32], axis=1)
    m2k = jnp.concatenate([z64, -sin, sin, z32], axis=1)
    q_scale = (MLA_NOPE + MLA_ROPE) ** -0.5 * LOG2E
    m1q = jnp.concatenate([o64, cos, cos, z32], axis=1) * q_scale
    m2q = m2k * q_scale

    row = lambda w: pl.BlockSpec((TM, w), lambda i: (i, 0))
    tab = pl.BlockSpec((TM, LANES), lambda i: (i % nsb, 0))
    out_w = (1024, 1024, 512, 512, 128, 128)
    return pl.pallas_call(
        _proj_ab_kernel,
        grid=(t // TM,),
        in_specs=[row(D_MODEL), _const_spec((D_MODEL, _AB_COLS)), _const_spec((1, MLA_Q_RANK)),
                  _const_spec((MLA_Q_RANK, 1024)), _const_spec((1, MLA_KV_RANK)),
                  _const_spec((MLA_KV_RANK, 1536)), tab, tab, tab, tab],
        out_specs=[row(w) for w in out_w],
        out_shape=[jax.ShapeDtypeStruct((t, w), BF16) for w in out_w],
        compiler_params=pltpu.CompilerParams(dimension_semantics=("arbitrary",),
                                             vmem_limit_bytes=VMEM_LIMIT),
        name="proj_ab",
    )(x2, win_p, q_norm.reshape(1, -1), wuq_p, kv_norm.reshape(1, -1), wukv_p, m1q, m2q, m1k, m2k)


def _proj_cd_kernel(x_ref, win_ref, bf_ref, qkv_ref, cum_ref, carry_ref, *, nsb):
    @pl.when(pl.program_id(0) % nsb == 0)
    def _():
        carry_ref[...] = jnp.zeros_like(carry_ref)

    proj = _dot(x_ref[...].astype(BF16), win_ref[...])
    q_scale = HEAD_DIM ** -0.5 * LOG2E
    for blk, scale in enumerate((q_scale, None, None, q_scale, None, None)):
        part = proj[:, blk * 512:(blk + 1) * 512]
        qkv_ref[:, blk * 512:(blk + 1) * 512] = (part if scale is None else part * scale).astype(BF16)
    z = proj[:, 3072:] + bf_ref[...]
    log_f = jnp.minimum(z, 0.0) - jnp.log(1.0 + jnp.exp(-jnp.abs(z)))
    r = lax.broadcasted_iota(jnp.int32, (TM, TM), 0)
    c = lax.broadcasted_iota(jnp.int32, (TM, TM), 1)
    tri = jnp.where(c <= r, 1.0, 0.0).astype(BF16)
    hi = log_f.astype(BF16)
    rem = log_f - hi.astype(F32)
    mid = rem.astype(BF16)
    lo = (rem - mid.astype(F32)).astype(BF16)
    cum = _dot(tri, hi) + _dot(tri, mid) + _dot(tri, lo) + carry_ref[...]
    cum_ref[...] = cum
    carry_ref[...] = cum[TM - 1:TM, :]


def _proj_cd(x2, w_in, b_forget, seq):
    t = x2.shape[0]
    q_f, k_f, v_f, f_l, q_c, k_c, v_c = jnp.split(w_in, [512, 1024, 1536, 1544, 2056, 2568], axis=1)
    f_pad = jnp.concatenate([f_l, jnp.zeros((D_MODEL, LANES - N_HEADS), F32)], axis=1)
    win_p = jnp.concatenate([q_f, k_f, v_f, q_c, k_c, v_c, f_pad], axis=1).astype(BF16)
    bf = jnp.concatenate([b_forget, jnp.zeros((LANES - N_HEADS,), F32)]).reshape(1, LANES)
    return pl.pallas_call(
        functools.partial(_proj_cd_kernel, nsb=seq // TM),
        grid=(t // TM,),
        in_specs=[pl.BlockSpec((TM, D_MODEL), lambda i: (i, 0)),
                  _const_spec((D_MODEL, 3072 + LANES)), _const_spec((1, LANES))],
        out_specs=[pl.BlockSpec((TM, 3072), lambda i: (i, 0)),
                   pl.BlockSpec((TM, LANES), lambda i: (i, 0))],
        out_shape=[jax.ShapeDtypeStruct((t, 3072), BF16), jax.ShapeDtypeStruct((t, LANES), F32)],
        scratch_shapes=[pltpu.VMEM((1, LANES), F32)],
        compiler_params=pltpu.CompilerParams(dimension_semantics=("arbitrary",),
                                             vmem_limit_bytes=VMEM_LIMIT),
        name="proj_cd",
    )(x2, win_p, bf)


def _attn_kernel(*refs, variant, window, seq, tq, tk):
    if variant == "mla":
        q_ref, k_ref, v_ref, o_ref, vt_ref, m_ref, acc_ref = refs
    elif variant == "fox":
        q_ref, k_ref, v_ref, ck_ref, o_ref, vt_ref, m_ref, acc_ref, ckrep_ref = refs
    elif variant == "swa":
        sink_ref, q_ref, k_ref, v_ref, bias_ref, o_ref, vt_ref, kpad_ref = refs
    else:
        q_ref, k_ref, v_ref, bias_ref, o_ref, vt_ref, kpad_ref = refs

    i = pl.program_id(2)
    banded = window is not None
    pad = window * tk if banded else 0

    @pl.when(i == 0)
    def _():
        for c in range(seq // 512):
            sl = slice(pad + c * 512, pad + (c + 1) * 512)
            vt_ref[0:LANES, sl] = v_ref[0, c * 512:(c + 1) * 512, :].astype(F32).T.astype(BF16)
        vt_ref[LANES:VT_ROWS, :] = jnp.ones((VT_ROWS - LANES, pad + seq), BF16)
        if banded:
            vt_ref[0:LANES, 0:pad] = jnp.zeros((LANES, pad), BF16)
            kpad_ref[0:pad, :] = jnp.zeros((pad, LANES), BF16)
            kpad_ref[pad:pad + seq, :] = k_ref[0]
        if variant == "fox":
            ck = ck_ref[0, 0]
            ckrep_ref[:, 0:LANES] = jnp.broadcast_to(ck[:, 0:1], (seq, LANES))
            ckrep_ref[:, LANES:2 * LANES] = jnp.broadcast_to(ck[:, 1:2], (seq, LANES))

    q_t = q_ref[0].astype(F32).T
    if variant == "mla":
        q_a, q_b = q_t[:LANES].astype(BF16), q_t[LANES:].astype(BF16)
    else:
        sub = lax.broadcasted_iota(jnp.int32, (LANES, tq), 0)
        q_pair = jnp.concatenate([jnp.where(sub < HEAD_DIM, q_t, 0.0),
                                  jnp.where(sub >= HEAD_DIM, q_t, 0.0)], axis=1).astype(BF16)

    def finish(acc, m):
        l = acc[LANES:LANES + 1, :]
        o_t = acc[:LANES, :]
        if variant == "swa":
            pair = pl.program_id(1)
            col = lax.broadcasted_iota(jnp.int32, (1, 2 * tq), 1)
            sink = jnp.where(col < tq, sink_ref[pair], sink_ref[pair + N_HEADS // 2])
            m_fin = jnp.maximum(m, sink)
            scale = jnp.exp2(m - m_fin)
            l = l * scale + jnp.exp2(sink - m_fin)
            o_t = o_t * scale
        o_t = o_t / l
        sub = lax.broadcasted_iota(jnp.int32, (LANES, tq), 0)
        o_ref[0] = jnp.where(sub < HEAD_DIM, o_t[:, :tq], o_t[:, tq:]).T.astype(o_ref.dtype)

    if banded:
        n_blk = window + 1
        start = pl.multiple_of(i * tk, tk)
        tiles = [bias_ref[0, jnp.where(i + r >= window, r, n_blk)] for r in range(n_blk)]
        s = _dot(kpad_ref[pl.ds(start, n_blk * tk), :], q_pair) + jnp.concatenate(tiles, axis=0)
        m = jnp.max(s, axis=0, keepdims=True)
        p = jnp.exp2(s - m).astype(BF16)
        finish(_dot(vt_ref[:, pl.ds(start, n_blk * tk)], p), m)
        return

    m_ref[...] = jnp.full_like(m_ref, MASK)
    acc_ref[...] = jnp.zeros_like(acc_ref)

    def step(j, diagonal):
        start = pl.multiple_of(j * tk, tk)
        k = k_ref[0, pl.ds(start, tk), :]
        if variant == "mla":
            s = jnp.concatenate([_dot(k[:, :LANES], q_a), _dot(k[:, LANES:], q_b)], axis=1)
        else:
            s = _dot(k, q_pair)
        if variant == "fox":
            ck_a = ckrep_ref[pl.ds(start, tk), 0:LANES]
            ck_b = ckrep_ref[pl.ds(start, tk), LANES:2 * LANES]
            s = s - jnp.concatenate([ck_a] * (tq // LANES) + [ck_b] * (tq // LANES), axis=1)
        if diagonal:
            key = lax.broadcasted_iota(jnp.int32, (tk, 2 * tq), 0)
            qry = lax.broadcasted_iota(jnp.int32, (tk, 2 * tq), 1) & (tq - 1)
            if variant == "mla":
                keep = (key >> 6) <= (qry >> 6)
            else:
                keep = key <= qry
            s = jnp.where(keep, s, MASK)
        m_prev = m_ref[...]
        m_new = jnp.maximum(m_prev, jnp.max(s, axis=0, keepdims=True))
        alpha = jnp.exp2(m_prev - m_new)
        p = jnp.exp2(s - m_new).astype(BF16)
        acc_ref[...] = alpha * acc_ref[...] + _dot(vt_ref[:, pl.ds(start, tk)], p)
        m_ref[...] = m_new

    lax.fori_loop(0, i, lambda j, c: step(j, False), None)
    step(i, True)
    finish(acc_ref[...], m_ref[...])


def _attention(variant, q, k, v, q_col, k_col, v_col, *, extra=None, sinks=None):
    b, s, _ = q.shape
    n_pairs = N_HEADS // 2
    tq = tk = FULL_TILE if variant in ("mla", "fox") else TQ
    nq = s // tq
    window = {"mla": None, "fox": None, "swa": 1, "ck": 2}[variant]
    qw = 2 * LANES if variant == "mla" else LANES
    shared_kv = variant == "swa"

    def im(f):
        if sinks is None:
            return f
        return lambda bi, p, i, sink_ref: f(bi, p, i)

    in_specs = [
        pl.BlockSpec((1, tq, qw), im(lambda bi, p, i: (bi, i, q_col + p))),
        pl.BlockSpec((1, s, qw), im(lambda bi, p, i: (bi, 0, k_col + (0 if shared_kv else p)))),
        pl.BlockSpec((1, s, LANES), im(lambda bi, p, i: (bi, 0, v_col + (0 if shared_kv else p)))),
    ]
    args = [q, k, v]
    if window is None:
        scratch = [pltpu.VMEM((VT_ROWS, s), BF16), pltpu.VMEM((1, 2 * tq), F32),
                   pltpu.VMEM((VT_ROWS, 2 * tq), F32)]
    else:
        pad = window * tk
        scratch = [pltpu.VMEM((VT_ROWS, pad + s), BF16), pltpu.VMEM((pad + s, LANES), BF16)]
        in_specs.append(pl.BlockSpec((1, window + 2, tk, 2 * tq), im(lambda bi, p, i: (p, 0, 0, 0))))
        args.append(extra)
    if variant == "fox":
        in_specs.append(pl.BlockSpec((1, 1, s, 2), lambda bi, p, i: (bi, p, 0, 0)))
        args.append(extra)
        scratch.append(pltpu.VMEM((s, 2 * LANES), F32))
    out_spec = pl.BlockSpec((1, tq, LANES), im(lambda bi, p, i: (bi, i, p)))
    kern = functools.partial(_attn_kernel, variant=variant, window=window, seq=s, tq=tq, tk=tk)
    params = pltpu.CompilerParams(dimension_semantics=("arbitrary",) * 3, vmem_limit_bytes=VMEM_LIMIT)
    out_shape = jax.ShapeDtypeStruct((b, s, n_pairs * LANES), BF16)
    if sinks is not None:
        grid_spec = pltpu.PrefetchScalarGridSpec(
            num_scalar_prefetch=1, grid=(b, n_pairs, nq), in_specs=in_specs, out_specs=out_spec,
            scratch_shapes=scratch)
        return pl.pallas_call(kern, grid_spec=grid_spec, out_shape=out_shape,
                              compiler_params=params, name="attn_" + variant)(sinks, *args)
    return pl.pallas_call(kern, grid=(b, n_pairs, nq), in_specs=in_specs, out_specs=out_spec,
                          out_shape=out_shape, scratch_shapes=scratch,
                          compiler_params=params, name="attn_" + variant)(*args)


def _band_geometry(left_blocks):
    r = jnp.arange(left_blocks + 1)[:, None, None]
    c = jnp.arange(TK)[None, :, None]
    a = jnp.arange(TQ)[None, None, :]
    dist = a - c + (left_blocks - r) * TK
    q_chunk = left_blocks * (TK // CHUNK) + a // CHUNK
    k_chunk = r * (TK // CHUNK) + c // CHUNK
    return dist, q_chunk, k_chunk


def _with_masked_tile(bias):
    return jnp.concatenate([bias, jnp.full_like(bias[:, :1], MASK)], axis=1)


def _swa_bias():
    dist, q_chunk, k_chunk = _band_geometry(1)
    valid = (k_chunk <= q_chunk) & (k_chunk >= q_chunk - SWA_LEFT_CHUNKS)
    slopes = jnp.exp2(-8.0 * jnp.arange(1, N_HEADS + 1, dtype=F32) / N_HEADS)
    bias = -slopes[:, None, None, None] * jnp.abs(dist).astype(F32)[None] * LOG2E
    bias = jnp.where(valid[None], bias, MASK)
    return _with_masked_tile(jnp.concatenate([bias[:4], bias[4:]], axis=3))


def _ck_bias(rel_bias):
    dist, q_chunk, k_chunk = _band_geometry(2)
    valid = (k_chunk <= q_chunk) & (k_chunk >= q_chunk - CK_LEFT_CHUNKS)
    idx = jnp.clip(dist, -(CHUNK - 1), REL_MAX) + (CHUNK - 1)
    bias = jnp.moveaxis(rel_bias[idx], -1, 0).astype(F32) * LOG2E
    bias = jnp.where(valid[None], bias, MASK)
    return _with_masked_tile(jnp.concatenate([bias[0::2], bias[1::2]], axis=3))


def _out_ffn_kernel(x_ref, oa_ref, ob_ref, woa_ref, wob_ref, g1_ref, b1_ref,
                    wg_ref, wu_ref, wd_ref, g2_ref, b2_ref, y_ref):
    mix = _dot(oa_ref[...], woa_ref[...]) + _dot(ob_ref[...], wob_ref[...])
    x1 = _layer_norm(DEEPNORM_ALPHA * x_ref[...] + mix, g1_ref[...], b1_ref[...])
    x1b = x1.astype(BF16)
    gate = _dot(x1b, wg_ref[...])
    up = _dot(x1b, wu_ref[...])
    hidden = (gate * jax.nn.sigmoid(gate) * up).astype(BF16)
    ffn = _dot(hidden, wd_ref[...])
    y_ref[...] = _layer_norm(DEEPNORM_ALPHA * x1 + ffn, g2_ref[...], b2_ref[...])


def _out_ffn(x2, o_a, o_b, w_out_a, w_out_b, g1, b1, w_gate, w_up, w_down, g2, b2):
    t = x2.shape[0]
    d_ff = w_gate.shape[1]
    row = lambda w: pl.BlockSpec((TM, w), lambda i: (i, 0))
    vec = _const_spec((1, D_MODEL))
    return pl.pallas_call(
        _out_ffn_kernel,
        grid=(t // TM,),
        in_specs=[row(D_MODEL), row(512), row(512), _const_spec((512, D_MODEL)),
                  _const_spec((512, D_MODEL)), vec, vec, _const_spec((D_MODEL, d_ff)),
                  _const_spec((D_MODEL, d_ff)), _const_spec((d_ff, D_MODEL)), vec, vec],
        out_specs=row(D_MODEL),
        out_shape=jax.ShapeDtypeStruct((t, D_MODEL), F32),
        compiler_params=pltpu.CompilerParams(dimension_semantics=("arbitrary",),
                                             vmem_limit_bytes=VMEM_LIMIT),
        name="out_ffn",
    )(x2, o_a, o_b, w_out_a.astype(BF16), w_out_b.astype(BF16), g1.reshape(1, -1), b1.reshape(1, -1),
      w_gate.astype(BF16), w_up.astype(BF16), w_down.astype(BF16), g2.reshape(1, -1), b2.reshape(1, -1))


def kernel(x, ab_w_in, ab_q_norm, ab_w_uq, ab_kv_norm, ab_w_ukv, ab_sinks, ab_w_out,
           cd_w_in, cd_b_forget, cd_rel_bias, cd_w_out,
           ln1_g, ln1_b, ffn_w_gate, ffn_w_up, ffn_w_down, ln2_g, ln2_b):
    b, s, d = x.shape
    assert d == D_MODEL and s % TM == 0 and TM % TQ == 0 and TQ == TK
    t = b * s
    x2 = x.reshape(t, d)

    qa, ka, va, qs, ks, vs = _proj_ab(x2, ab_w_in[0], ab_q_norm[0], ab_w_uq[0], ab_kv_norm[0],
                                      ab_w_ukv[0], s)
    r3 = lambda a: a.reshape(b, s, a.shape[-1])
    o_a = _attention("mla", r3(qa), r3(ka), r3(va), 0, 0, 0)
    o_b = _attention("swa", r3(qs), r3(ks), r3(vs), 0, 0, 0, extra=_swa_bias(),
                     sinks=ab_sinks[0] * LOG2E)
    w_out = ab_w_out[0]
    w_out_b = w_out[512:].reshape(2, 4, HEAD_DIM, D_MODEL).transpose(1, 0, 2, 3).reshape(512, D_MODEL)
    x2 = _out_ffn(x2, o_a.reshape(t, 512), o_b.reshape(t, 512), w_out[:512], w_out_b,
                  ln1_g[0], ln1_b[0], ffn_w_gate[0], ffn_w_up[0], ffn_w_down[0], ln2_g[0], ln2_b[0])

    qkv, cum = _proj_cd(x2, cd_w_in[0], cd_b_forget[0], s)
    qkv = qkv.reshape(b, s, 3072)
    ck = (cum[:, :N_HEADS] * LOG2E).reshape(b, s, N_HEADS // 2, 2).transpose(0, 2, 1, 3)
    o_c = _attention("fox", qkv, qkv, qkv, 0, 4, 8, extra=ck)
    o_d = _attention("ck", qkv, qkv, qkv, 12, 16, 20, extra=_ck_bias(cd_rel_bias[0]))
    w_out = cd_w_out[0]
    x2 = _out_ffn(x2, o_c.reshape(t, 512), o_d.reshape(t, 512), w_out[:512], w_out[512:],
                  ln1_g[1], ln1_b[1], ffn_w_gate[1], ffn_w_up[1], ffn_w_down[1], ln2_g[1], ln2_b[1])
    return x2.reshape(b, s, d)
```

```python
import functools
import math

import numpy as np
import jax
import jax.numpy as jnp
from jax import lax
from jax.experimental import pallas as pl
from jax.experimental.pallas import tpu as pltpu

D_MODEL = 1024
CHUNK = 64
HEAD_DIM = 64
N_HEADS = 8
LANES = 128
LN_EPS = 1e-5
RMS_EPS = 1e-6
MASK = -1e30
LOG2E = math.log2(math.e)

MLA_Q_RANK = 384
MLA_KV_RANK = 256
MLA_NOPE = 64
MLA_ROPE = 32
ROPE_THETA = 10000.0
SWA_KV_HEADS = 2
SWA_LEFT_CHUNKS = 2
CK_LEFT_CHUNKS = 8
REL_MAX = 256
DEPTH = 2
DEEPNORM_ALPHA = (2 * DEPTH) ** 0.25

TQ = 256
TK = 256
FULL_TILE = 512
TM = 512
VT_ROWS = LANES + 16
VMEM_LIMIT = 56 * 1024 * 1024

BF16 = jnp.bfloat16
F32 = jnp.float32


def _dot(a, b):
    return jnp.dot(a, b, preferred_element_type=F32)


def _layer_norm(y, g, b):
    mu = jnp.mean(y, axis=-1, keepdims=True)
    yc = y - mu
    var = jnp.mean(yc * yc, axis=-1, keepdims=True)
    return yc * lax.rsqrt(var + LN_EPS) * g + b


def _rms_norm(c, g):
    return c * lax.rsqrt(jnp.mean(c * c, axis=-1, keepdims=True) + RMS_EPS) * g


def _const_spec(shape):
    zeros = (0,) * len(shape)
    return pl.BlockSpec(shape, lambda *_: zeros, pipeline_mode=pl.Buffered(1))


_AB_CQ, _AB_CKV, _AB_QS, _AB_KS, _AB_VS, _AB_KR, _AB_COLS = 0, 384, 640, 1152, 1280, 1408, 1536


def _rope_mix(t, m1, m2):
    return t * m1 + pltpu.roll(t, 96, 1) * m2


def _proj_ab_kernel(x_ref, win_ref, qn_ref, wuq_ref, kvn_ref, wukv_ref,
                    m1q_ref, m2q_ref, m1k_ref, m2k_ref,
                    qa_ref, ka_ref, va_ref, qs_ref, ks_ref, vs_ref):
    proj = _dot(x_ref[...].astype(BF16), win_ref[...])
    c_q = _rms_norm(proj[:, _AB_CQ:_AB_CKV], qn_ref[...])
    qf = _dot(c_q.astype(BF16), wuq_ref[...])
    c_kv = _rms_norm(proj[:, _AB_CKV:_AB_QS], kvn_ref[...])
    kvf = _dot(c_kv.astype(BF16), wukv_ref[...])
    k_rope = _rope_mix(proj[:, _AB_KR:_AB_COLS], m1k_ref[...], m2k_ref[...])
    m1q, m2q = m1q_ref[...], m2q_ref[...]
    for h in range(N_HEADS):
        sl = slice(h * LANES, (h + 1) * LANES)
        qa_ref[:, sl] = _rope_mix(qf[:, sl], m1q, m2q).astype(BF16)
        ka_ref[:, sl] = (kvf[:, sl] + k_rope).astype(BF16)
    va_ref[...] = kvf[:, N_HEADS * LANES:].astype(BF16)
    qs_ref[...] = (proj[:, _AB_QS:_AB_KS] * (HEAD_DIM ** -0.5 * LOG2E)).astype(BF16)
    ks_ref[...] = proj[:, _AB_KS:_AB_VS].astype(BF16)
    vs_ref[...] = proj[:, _AB_VS:_AB_KR].astype(BF16)


def _proj_ab(x2, w_in, q_norm, w_uq, kv_norm, w_ukv, seq):
    t = x2.shape[0]
    nsb = seq // TM
    c_q, c_kv, k_r, q_s, k_s, v_s = jnp.split(w_in, [384, 640, 672, 1184, 1312], axis=1)
    q_s = q_s.reshape(D_MODEL, 2, 4, HEAD_DIM).transpose(0, 2, 1, 3).reshape(D_MODEL, 512)
    kr_blk = jnp.concatenate([jnp.zeros((D_MODEL, 64), F32), k_r, k_r[:, 16:], k_r[:, :16]], axis=1)
    win_p = jnp.concatenate([c_q, c_kv, q_s, k_s, v_s, kr_blk], axis=1).astype(BF16)
    wq = w_uq.reshape(MLA_Q_RANK, N_HEADS, MLA_NOPE + MLA_ROPE)
    wuq_p = jnp.concatenate([wq, wq[..., 80:96], wq[..., 64:80]], axis=-1)
    wuq_p = wuq_p.reshape(MLA_Q_RANK, N_HEADS * LANES).astype(BF16)
    wkv = w_ukv.reshape(MLA_KV_RANK, N_HEADS, 2 * HEAD_DIM)
    k_pad = jnp.concatenate([wkv[..., :64], jnp.zeros_like(wkv[..., :64])], axis=-1)
    wukv_p = jnp.concatenate([k_pad.reshape(MLA_KV_RANK, N_HEADS * LANES),
                              wkv[..., 64:].reshape(MLA_KV_RANK, 512)], axis=1).astype(BF16)
    inv = ROPE_THETA ** (-jnp.arange(0, MLA_ROPE, 2, dtype=F32) / MLA_ROPE)
    ang = jnp.arange(seq, dtype=F32)[:, None] * inv[None, :]
    cos, sin = jnp.cos(ang), jnp.sin(ang)
    z64, z32, o64 = jnp.zeros((seq, 64), F32), jnp.zeros((seq, 32), F32), jnp.ones((seq, 64), F32)
    m1k = jnp.concatenate([z64, cos, cos, z32], axis=1)
    m2k = jnp.concatenate([z64, -sin, sin, z32], axis=1)
    q_scale = (MLA_NOPE + MLA_ROPE) ** -0.5 * LOG2E
    m1q = jnp.concatenate([o64, cos, cos, z32], axis=1) * q_scale
    m2q = m2k * q_scale

    row = lambda w: pl.BlockSpec((TM, w), lambda i: (i, 0))
    tab = pl.BlockSpec((TM, LANES), lambda i: (i % nsb, 0))
    out_w = (1024, 1024, 512, 512, 128, 128)
    return pl.pallas_call(
        _proj_ab_kernel,
        grid=(t // TM,),
        in_specs=[row(D_MODEL), _const_spec((D_MODEL, _AB_COLS)), _const_spec((1, MLA_Q_RANK)),
                  _const_spec((MLA_Q_RANK, 1024)), _const_spec((1, MLA_KV_RANK)),
                  _const_spec((MLA_KV_RANK, 1536)), tab, tab, tab, tab],
        out_specs=[row(w) for w in out_w],
        out_shape=[jax.ShapeDtypeStruct((t, w), BF16) for w in out_w],
        compiler_params=pltpu.CompilerParams(dimension_semantics=("arbitrary",),
                                             vmem_limit_bytes=VMEM_LIMIT),
        name="proj_ab",
    )(x2, win_p, q_norm.reshape(1, -1), wuq_p, kv_norm.reshape(1, -1), wukv_p, m1q, m2q, m1k, m2k)


def _proj_cd_kernel(x_ref, win_ref, bf_ref, qkv_ref, cum_ref, carry_ref, *, nsb):
    @pl.when(pl.program_id(0) % nsb == 0)
    def _():
        carry_ref[...] = jnp.zeros_like(carry_ref)

    proj = _dot(x_ref[...].astype(BF16), win_ref[...])
    q_scale = HEAD_DIM ** -0.5 * LOG2E
    for blk, scale in enumerate((q_scale, None, None, q_scale, None, None)):
        part = proj[:, blk * 512:(blk + 1) * 512]
        qkv_ref[:, blk * 512:(blk + 1) * 512] = (part if scale is None else part * scale).astype(BF16)
    z = proj[:, 3072:] + bf_ref[...]
    log_f = jnp.minimum(z, 0.0) - jnp.log(1.0 + jnp.exp(-jnp.abs(z)))
    r = lax.broadcasted_iota(jnp.int32, (TM, TM), 0)
    c = lax.broadcasted_iota(jnp.int32, (TM, TM), 1)
    tri = jnp.where(c <= r, 1.0, 0.0).astype(BF16)
    hi = log_f.astype(BF16)
    rem = log_f - hi.astype(F32)
    mid = rem.astype(BF16)
    lo = (rem - mid.astype(F32)).astype(BF16)
    cum = _dot(tri, hi) + _dot(tri, mid) + _dot(tri, lo) + carry_ref[...]
    cum_ref[...] = cum
    carry_ref[...] = cum[TM - 1:TM, :]


def _proj_cd(x2, w_in, b_forget, seq):
    t = x2.shape[0]
    q_f, k_f, v_f, f_l, q_c, k_c, v_c = jnp.split(w_in, [512, 1024, 1536, 1544, 2056, 2568], axis=1)
    f_pad = jnp.concatenate([f_l, jnp.zeros((D_MODEL, LANES - N_HEADS), F32)], axis=1)
    win_p = jnp.concatenate([q_f, k_f, v_f, q_c, k_c, v_c, f_pad], axis=1).astype(BF16)
    bf = jnp.concatenate([b_forget, jnp.zeros((LANES - N_HEADS,), F32)]).reshape(1, LANES)
    return pl.pallas_call(
        functools.partial(_proj_cd_kernel, nsb=seq // TM),
        grid=(t // TM,),
        in_specs=[pl.BlockSpec((TM, D_MODEL), lambda i: (i, 0)),
                  _const_spec((D_MODEL, 3072 + LANES)), _const_spec((1, LANES))],
        out_specs=[pl.BlockSpec((TM, 3072), lambda i: (i, 0)),
                   pl.BlockSpec((TM, LANES), lambda i: (i, 0))],
        out_shape=[jax.ShapeDtypeStruct((t, 3072), BF16), jax.ShapeDtypeStruct((t, LANES), F32)],
        scratch_shapes=[pltpu.VMEM((1, LANES), F32)],
        compiler_params=pltpu.CompilerParams(dimension_semantics=("arbitrary",),
                                             vmem_limit_bytes=VMEM_LIMIT),
        name="proj_cd",
    )(x2, win_p, bf)


def _attn_kernel(*refs, variant, window, seq, tq, tk):
    if variant == "mla":
        q_ref, k_ref, v_ref, o_ref, vt_ref, m_ref, acc_ref, sa_ref, sb_ref = refs
    elif variant == "fox":
        q_ref, k_ref, v_ref, ck_ref, o_ref, vt_ref, m_ref, acc_ref, sa_ref, sb_ref, ckrep_ref = refs
    elif variant == "swa":
        sink_ref, q_ref, k_ref, v_ref, bias_ref, o_ref, vt_ref, kpad_ref = refs
    else:
        q_ref, k_ref, v_ref, bias_ref, o_ref, vt_ref, kpad_ref = refs

    i = pl.program_id(2)
    banded = window is not None
    pad = window * tk if banded else 0

    @pl.when(i == 0)
    def _():
        for c in range(seq // 512):
            sl = slice(pad + c * 512, pad + (c + 1) * 512)
            vt_ref[0:LANES, sl] = v_ref[0, c * 512:(c + 1) * 512, :].astype(F32).T.astype(BF16)
        vt_ref[LANES:VT_ROWS, :] = jnp.ones((VT_ROWS - LANES, pad + seq), BF16)
        if banded:
            vt_ref[0:LANES, 0:pad] = jnp.zeros((LANES, pad), BF16)
            kpad_ref[0:pad, :] = jnp.zeros((pad, LANES), BF16)
            kpad_ref[pad:pad + seq, :] = k_ref[0]
        if variant == "fox":
            ck = ck_ref[0, 0]
            ckrep_ref[:, 0:LANES] = jnp.broadcast_to(ck[:, 0:1], (seq, LANES))
            ckrep_ref[:, LANES:2 * LANES] = jnp.broadcast_to(ck[:, 1:2], (seq, LANES))

    q_t = q_ref[0].astype(F32).T
    if variant == "mla":
        q_a, q_b = q_t[:LANES].astype(BF16), q_t[LANES:].astype(BF16)
    else:
        sub = lax.broadcasted_iota(jnp.int32, (LANES, tq), 0)
        q_pair = jnp.concatenate([jnp.where(sub < HEAD_DIM, q_t, 0.0),
                                  jnp.where(sub >= HEAD_DIM, q_t, 0.0)], axis=1).astype(BF16)

    def finish(acc, m):
        l = acc[LANES:LANES + 1, :]
        o_t = acc[:LANES, :]
        if variant == "swa":
            pair = pl.program_id(1)
            col = lax.broadcasted_iota(jnp.int32, (1, 2 * tq), 1)
            sink = jnp.where(col < tq, sink_ref[pair], sink_ref[pair + N_HEADS // 2])
            m_fin = jnp.maximum(m, sink)
            scale = jnp.exp2(m - m_fin)
            l = l * scale + jnp.exp2(sink - m_fin)
            o_t = o_t * scale
        o_t = o_t / l
        sub = lax.broadcasted_iota(jnp.int32, (LANES, tq), 0)
        o_ref[0] = jnp.where(sub < HEAD_DIM, o_t[:, :tq], o_t[:, tq:]).T.astype(o_ref.dtype)

    if banded:
        n_blk = window + 1
        start = pl.multiple_of(i * tk, tk)
        tiles = [bias_ref[0, jnp.where(i + r >= window, r, n_blk)] for r in range(n_blk)]
        s = _dot(kpad_ref[pl.ds(start, n_blk * tk), :], q_pair) + jnp.concatenate(tiles, axis=0)
        m = jnp.max(s, axis=0, keepdims=True)
        p = jnp.exp2(s - m).astype(BF16)
        finish(_dot(vt_ref[:, pl.ds(start, n_blk * tk)], p), m)
        return

    m_ref[...] = jnp.full_like(m_ref, MASK)
    acc_ref[...] = jnp.zeros_like(acc_ref)

    def scores(j, s_ref):
        start = pl.multiple_of(j * tk, tk)
        k = k_ref[0, pl.ds(start, tk), :]
        if variant == "mla":
            s_ref[:, 0:tq] = _dot(k[:, :LANES], q_a)
            s_ref[:, tq:2 * tq] = _dot(k[:, LANES:], q_b)
        else:
            s = _dot(k, q_pair)
            if variant == "fox":
                ck_a = ckrep_ref[pl.ds(start, tk), 0:LANES]
                ck_b = ckrep_ref[pl.ds(start, tk), LANES:2 * LANES]
                s = s - jnp.concatenate([ck_a] * (tq // LANES) + [ck_b] * (tq // LANES), axis=1)
            s_ref[...] = s

    def consume(j, s_ref, diagonal):
        start = pl.multiple_of(j * tk, tk)
        s = s_ref[...]
        if diagonal:
            key = lax.broadcasted_iota(jnp.int32, (tk, 2 * tq), 0)
            qry = lax.broadcasted_iota(jnp.int32, (tk, 2 * tq), 1) & (tq - 1)
            if variant == "mla":
                keep = (key >> 6) <= (qry >> 6)
            else:
                keep = key <= qry
            s = jnp.where(keep, s, MASK)
        m_prev = m_ref[...]
        m_new = jnp.maximum(m_prev, jnp.max(s, axis=0, keepdims=True))
        alpha = jnp.exp2(m_prev - m_new)
        p = jnp.exp2(s - m_new).astype(BF16)
        acc_ref[...] = alpha * acc_ref[...] + _dot(vt_ref[:, pl.ds(start, tk)], p)
        m_ref[...] = m_new

    scores(0, sa_ref)

    def two_blocks(u, carry):
        t = 2 * u
        scores(t + 1, sb_ref)
        consume(t, sa_ref, False)
        scores(t + 2, sa_ref)
        consume(t + 1, sb_ref, False)
        return carry

    lax.fori_loop(0, i >> 1, two_blocks, None)

    @pl.when((i & 1) == 0)
    def _():
        consume(i, sa_ref, True)

    @pl.when((i & 1) == 1)
    def _():
        scores(i, sb_ref)
        consume(i - 1, sa_ref, False)
        consume(i, sb_ref, True)

    finish(acc_ref[...], m_ref[...])


def _attention(variant, q, k, v, q_col, k_col, v_col, *, extra=None, sinks=None):
    b, s, _ = q.shape
    n_pairs = N_HEADS // 2
    tq = tk = FULL_TILE if variant in ("mla", "fox") else TQ
    nq = s // tq
    window = {"mla": None, "fox": None, "swa": 1, "ck": 2}[variant]
    qw = 2 * LANES if variant == "mla" else LANES
    shared_kv = variant == "swa"

    def im(f):
        if sinks is None:
            return f
        return lambda bi, p, i, sink_ref: f(bi, p, i)

    in_specs = [
        pl.BlockSpec((1, tq, qw), im(lambda bi, p, i: (bi, i, q_col + p))),
        pl.BlockSpec((1, s, qw), im(lambda bi, p, i: (bi, 0, k_col + (0 if shared_kv else p)))),
        pl.BlockSpec((1, s, LANES), im(lambda bi, p, i: (bi, 0, v_col + (0 if shared_kv else p)))),
    ]
    args = [q, k, v]
    if window is None:
        scratch = [pltpu.VMEM((VT_ROWS, s), BF16), pltpu.VMEM((1, 2 * tq), F32),
                   pltpu.VMEM((VT_ROWS, 2 * tq), F32),
                   pltpu.VMEM((tk, 2 * tq), F32), pltpu.VMEM((tk, 2 * tq), F32)]
    else:
        pad = window * tk
        scratch = [pltpu.VMEM((VT_ROWS, pad + s), BF16), pltpu.VMEM((pad + s, LANES), BF16)]
        in_specs.append(pl.BlockSpec((1, window + 2, tk, 2 * tq), im(lambda bi, p, i: (p, 0, 0, 0))))
        args.append(extra)
    if variant == "fox":
        in_specs.append(pl.BlockSpec((1, 1, s, 2), lambda bi, p, i: (bi, p, 0, 0)))
        args.append(extra)
        scratch.append(pltpu.VMEM((s, 2 * LANES), F32))
    out_spec = pl.BlockSpec((1, tq, LANES), im(lambda bi, p, i: (bi, i, p)))
    kern = functools.partial(_attn_kernel, variant=variant, window=window, seq=s, tq=tq, tk=tk)
    params = pltpu.CompilerParams(dimension_semantics=("arbitrary",) * 3, vmem_limit_bytes=VMEM_LIMIT)
    out_shape = jax.ShapeDtypeStruct((b, s, n_pairs * LANES), BF16)
    if sinks is not None:
        grid_spec = pltpu.PrefetchScalarGridSpec(
            num_scalar_prefetch=1, grid=(b, n_pairs, nq), in_specs=in_specs, out_specs=out_spec,
            scratch_shapes=scratch)
        return pl.pallas_call(kern, grid_spec=grid_spec, out_shape=out_shape,
                              compiler_params=params, name="attn_" + variant)(sinks, *args)
    return pl.pallas_call(kern, grid=(b, n_pairs, nq), in_specs=in_specs, out_specs=out_spec,
                          out_shape=out_shape, scratch_shapes=scratch,
                          compiler_params=params, name="attn_" + variant)(*args)


def _band_geometry(left_blocks):
    r = jnp.arange(left_blocks + 1)[:, None, None]
    c = jnp.arange(TK)[None, :, None]
    a = jnp.arange(TQ)[None, None, :]
    dist = a - c + (left_blocks - r) * TK
    q_chunk = left_blocks * (TK // CHUNK) + a // CHUNK
    k_chunk = r * (TK // CHUNK) + c // CHUNK
    return dist, q_chunk, k_chunk


def _with_masked_tile(bias):
    return jnp.concatenate([bias, jnp.full_like(bias[:, :1], MASK)], axis=1)


def _swa_bias():
    dist, q_chunk, k_chunk = _band_geometry(1)
    valid = (k_chunk <= q_chunk) & (k_chunk >= q_chunk - SWA_LEFT_CHUNKS)
    slopes = jnp.exp2(-8.0 * jnp.arange(1, N_HEADS + 1, dtype=F32) / N_HEADS)
    bias = -slopes[:, None, None, None] * jnp.abs(dist).astype(F32)[None] * LOG2E
    bias = jnp.where(valid[None], bias, MASK)
    return _with_masked_tile(jnp.concatenate([bias[:4], bias[4:]], axis=3))


_REL_ROWS = 384


def _ck_bias_kernel(tab_ref, out_ref):
    span = TQ + TK
    tab = tab_ref[...] * LOG2E
    hi = tab.astype(BF16)
    rem = tab - hi.astype(F32)
    mid = rem.astype(BF16)
    lo = (rem - mid.astype(F32)).astype(BF16)
    t = lax.broadcasted_iota(jnp.int32, (_REL_ROWS, span), 0)
    n = lax.broadcasted_iota(jnp.int32, (_REL_ROWS, span), 1)
    delta = jnp.where(n < TQ, n, n - span)
    c = lax.broadcasted_iota(jnp.int32, (TK, TQ), 0)
    a = lax.broadcasted_iota(jnp.int32, (TK, TQ), 1)
    q_chunk = CK_LEFT_CHUNKS + (a >> 6)
    for r in range(3):
        idx = jnp.clip(delta + (2 - r) * TK, -(CHUNK - 1), REL_MAX) + (CHUNK - 1)
        onehot = jnp.where(t == idx, 1.0, 0.0).astype(BF16)
        base = _dot(hi, onehot) + _dot(mid, onehot) + _dot(lo, onehot)
        k_chunk = r * (TK // CHUNK) + (c >> 6)
        valid = (q_chunk - k_chunk).astype(jnp.uint32) <= CK_LEFT_CHUNKS
        for h in range(N_HEADS):
            rows = jnp.broadcast_to(base[h:h + 1, :], (TK, span))
            tile = pltpu.roll(rows, 0, 1, stride=1, stride_axis=0)[:, :TQ]
            out_ref[h // 2, r, :, (h % 2) * TQ:(h % 2 + 1) * TQ] = jnp.where(valid, tile, MASK)
    out_ref[:, 3] = jnp.full((N_HEADS // 2, TK, 2 * TQ), MASK, F32)


def _ck_bias(rel_bias):
    tab = jnp.pad(rel_bias.T, ((0, 0), (0, _REL_ROWS - rel_bias.shape[0])))
    return pl.pallas_call(
        _ck_bias_kernel,
        out_shape=jax.ShapeDtypeStruct((N_HEADS // 2, 4, TK, 2 * TQ), F32),
        compiler_params=pltpu.CompilerParams(vmem_limit_bytes=VMEM_LIMIT),
        name="ck_bias",
    )(tab)


def _out_ffn_kernel(x_ref, oa_ref, ob_ref, woa_ref, wob_ref, g1_ref, b1_ref,
                    wg_ref, wu_ref, wd_ref, g2_ref, b2_ref, y_ref):
    mix = _dot(oa_ref[...], woa_ref[...]) + _dot(ob_ref[...], wob_ref[...])
    x1 = _layer_norm(DEEPNORM_ALPHA * x_ref[...] + mix, g1_ref[...], b1_ref[...])
    x1b = x1.astype(BF16)
    gate = _dot(x1b, wg_ref[...])
    up = _dot(x1b, wu_ref[...])
    hidden = (gate * jax.nn.sigmoid(gate) * up).astype(BF16)
    ffn = _dot(hidden, wd_ref[...])
    y_ref[...] = _layer_norm(DEEPNORM_ALPHA * x1 + ffn, g2_ref[...], b2_ref[...])


def _out_ffn(x2, o_a, o_b, w_out_a, w_out_b, g1, b1, w_gate, w_up, w_down, g2, b2):
    t = x2.shape[0]
    d_ff = w_gate.shape[1]
    row = lambda w: pl.BlockSpec((TM, w), lambda i: (i, 0))
    vec = _const_spec((1, D_MODEL))
    return pl.pallas_call(
        _out_ffn_kernel,
        grid=(t // TM,),
        in_specs=[row(D_MODEL), row(512), row(512), _const_spec((512, D_MODEL)),
                  _const_spec((512, D_MODEL)), vec, vec, _const_spec((D_MODEL, d_ff)),
                  _const_spec((D_MODEL, d_ff)), _const_spec((d_ff, D_MODEL)), vec, vec],
        out_specs=row(D_MODEL),
        out_shape=jax.ShapeDtypeStruct((t, D_MODEL), F32),
        compiler_params=pltpu.CompilerParams(dimension_semantics=("arbitrary",),
                                             vmem_limit_bytes=VMEM_LIMIT),
        name="out_ffn",
    )(x2, o_a, o_b, w_out_a.astype(BF16), w_out_b.astype(BF16), g1.reshape(1, -1), b1.reshape(1, -1),
      w_gate.astype(BF16), w_up.astype(BF16), w_down.astype(BF16), g2.reshape(1, -1), b2.reshape(1, -1))


def kernel(x, ab_w_in, ab_q_norm, ab_w_uq, ab_kv_norm, ab_w_ukv, ab_sinks, ab_w_out,
           cd_w_in, cd_b_forget, cd_rel_bias, cd_w_out,
           ln1_g, ln1_b, ffn_w_gate, ffn_w_up, ffn_w_down, ln2_g, ln2_b):
    b, s, d = x.shape
    assert d == D_MODEL and s % TM == 0 and TM % TQ == 0 and TQ == TK
    t = b * s
    x2 = x.reshape(t, d)

    qa, ka, va, qs, ks, vs = _proj_ab(x2, ab_w_in[0], ab_q_norm[0], ab_w_uq[0], ab_kv_norm[0],
                                      ab_w_ukv[0], s)
    r3 = lambda a: a.reshape(b, s, a.shape[-1])
    o_a = _attention("mla", r3(qa), r3(ka), r3(va), 0, 0, 0)
    o_b = _attention("swa", r3(qs), r3(ks), r3(vs), 0, 0, 0, extra=_swa_bias(),
                     sinks=ab_sinks[0] * LOG2E)
    w_out = ab_w_out[0]
    w_out_b = w_out[512:].reshape(2, 4, HEAD_DIM, D_MODEL).transpose(1, 0, 2, 3).reshape(512, D_MODEL)
    x2 = _out_ffn(x2, o_a.reshape(t, 512), o_b.reshape(t, 512), w_out[:512], w_out_b,
                  ln1_g[0], ln1_b[0], ffn_w_gate[0], ffn_w_up[0], ffn_w_down[0], ln2_g[0], ln2_b[0])

    qkv, cum = _proj_cd(x2, cd_w_in[0], cd_b_forget[0], s)
    qkv = qkv.reshape(b, s, 3072)
    ck = (cum[:, :N_HEADS] * LOG2E).reshape(b, s, N_HEADS // 2, 2).transpose(0, 2, 1, 3)
    o_c = _attention("fox", qkv, qkv, qkv, 0, 4, 8, extra=ck)
    o_d = _attention("ck", qkv, qkv, qkv, 12, 16, 20, extra=_ck_bias(cd_rel_bias[0]))
    w_out = cd_w_out[0]
    x2 = _out_ffn(x2, o_c.reshape(t, 512), o_d.reshape(t, 512), w_out[:512], w_out[512:],
                  ln1_g[1], ln1_b[1], ffn_w_gate[1], ffn_w_up[1], ffn_w_down[1], ln2_g[1], ln2_b[1])
    return x2.reshape(b, s, d)
```

```python
import functools
import math

import numpy as np
import jax
import jax.numpy as jnp
from jax import lax
from jax.experimental import pallas as pl
from jax.experimental.pallas import tpu as pltpu

D_MODEL = 1024
CHUNK = 64
HEAD_DIM = 64
N_HEADS = 8
LANES = 128
LN_EPS = 1e-5
RMS_EPS = 1e-6
MASK = -1e30
LOG2E = math.log2(math.e)

MLA_Q_RANK = 384
MLA_KV_RANK = 256
MLA_NOPE = 64
MLA_ROPE = 32
ROPE_THETA = 10000.0
SWA_KV_HEADS = 2
SWA_LEFT_CHUNKS = 2
CK_LEFT_CHUNKS = 8
REL_MAX = 256
DEPTH = 2
DEEPNORM_ALPHA = (2 * DEPTH) ** 0.25

TQ = 256
TK = 256
BAND_SUB_BLOCKS = 4
SWA_WINDOW_KEYS = SWA_LEFT_CHUNKS * CHUNK + TQ
CK_WINDOW_KEYS = CK_LEFT_CHUNKS * CHUNK + TQ
FULL_TILE = 512
TM = 512
VT_ROWS = LANES + 16
VMEM_LIMIT = 56 * 1024 * 1024

BF16 = jnp.bfloat16
F32 = jnp.float32


def _dot(a, b):
    return jnp.dot(a, b, preferred_element_type=F32)


def _layer_norm(y, g, b):
    mu = jnp.mean(y, axis=-1, keepdims=True)
    yc = y - mu
    var = jnp.mean(yc * yc, axis=-1, keepdims=True)
    return yc * lax.rsqrt(var + LN_EPS) * g + b


def _rms_norm(c, g):
    return c * lax.rsqrt(jnp.mean(c * c, axis=-1, keepdims=True) + RMS_EPS) * g


def _const_spec(shape):
    zeros = (0,) * len(shape)
    return pl.BlockSpec(shape, lambda *_: zeros, pipeline_mode=pl.Buffered(1))


_AB_CQ, _AB_CKV, _AB_QS, _AB_KS, _AB_VS, _AB_KR, _AB_COLS = 0, 384, 640, 1152, 1280, 1408, 1536


def _rope_mix(t, m1, m2):
    return t * m1 + pltpu.roll(t, 96, 1) * m2


def _proj_ab_kernel(x_ref, win_ref, qn_ref, wuq_ref, kvn_ref, wukv_ref,
                    m1q_ref, m2q_ref, m1k_ref, m2k_ref,
                    qa_ref, ka_ref, va_ref, qs_ref, ks_ref, vs_ref):
    proj = _dot(x_ref[...].astype(BF16), win_ref[...])
    c_q = _rms_norm(proj[:, _AB_CQ:_AB_CKV], qn_ref[...])
    qf = _dot(c_q.astype(BF16), wuq_ref[...])
    c_kv = _rms_norm(proj[:, _AB_CKV:_AB_QS], kvn_ref[...])
    kvf = _dot(c_kv.astype(BF16), wukv_ref[...])
    k_rope = _rope_mix(proj[:, _AB_KR:_AB_COLS], m1k_ref[...], m2k_ref[...])
    m1q, m2q = m1q_ref[...], m2q_ref[...]
    for h in range(N_HEADS):
        sl = slice(h * LANES, (h + 1) * LANES)
        qa_ref[:, sl] = _rope_mix(qf[:, sl], m1q, m2q).astype(BF16)
        ka_ref[:, sl] = (kvf[:, sl] + k_rope).astype(BF16)
    va_ref[...] = kvf[:, N_HEADS * LANES:].astype(BF16)
    qs_ref[...] = (proj[:, _AB_QS:_AB_KS] * (HEAD_DIM ** -0.5 * LOG2E)).astype(BF16)
    ks_ref[...] = proj[:, _AB_KS:_AB_VS].astype(BF16)
    vs_ref[...] = proj[:, _AB_VS:_AB_KR].astype(BF16)


def _proj_ab(x2, w_in, q_norm, w_uq, kv_norm, w_ukv, seq):
    t = x2.shape[0]
    nsb = seq // TM
    c_q, c_kv, k_r, q_s, k_s, v_s = jnp.split(w_in, [384, 640, 672, 1184, 1312], axis=1)
    q_s = q_s.reshape(D_MODEL, 2, 4, HEAD_DIM).transpose(0, 2, 1, 3).reshape(D_MODEL, 512)
    kr_blk = jnp.concatenate([jnp.zeros((D_MODEL, 64), F32), k_r, k_r[:, 16:], k_r[:, :16]], axis=1)
    win_p = jnp.concatenate([c_q, c_kv, q_s, k_s, v_s, kr_blk], axis=1).astype(BF16)
    wq = w_uq.reshape(MLA_Q_RANK, N_HEADS, MLA_NOPE + MLA_ROPE)
    wuq_p = jnp.concatenate([wq, wq[..., 80:96], wq[..., 64:80]], axis=-1)
    wuq_p = wuq_p.reshape(MLA_Q_RANK, N_HEADS * LANES).astype(BF16)
    wkv = w_ukv.reshape(MLA_KV_RANK, N_HEADS, 2 * HEAD_DIM)
    k_pad = jnp.concatenate([wkv[..., :64], jnp.zeros_like(wkv[..., :64])], axis=-1)
    wukv_p = jnp.concatenate([k_pad.reshape(MLA_KV_RANK, N_HEADS * LANES),
                              wkv[..., 64:].reshape(MLA_KV_RANK, 512)], axis=1).astype(BF16)
    inv = ROPE_THETA ** (-jnp.arange(0, MLA_ROPE, 2, dtype=F32) / MLA_ROPE)
    ang = jnp.arange(seq, dtype=F32)[:, None] * inv[None, :]
    cos, sin = jnp.cos(ang), jnp.sin(ang)
    z64, z32, o64 = jnp.zeros((seq, 64), F32), jnp.zeros((seq, 32), F32), jnp.ones((seq, 64), F32)
    m1k = jnp.concatenate([z64, cos, cos, z32], axis=1)
    m2k = jnp.concatenate([z64, -sin, sin, z32], axis=1)
    q_scale = (MLA_NOPE + MLA_ROPE) ** -0.5 * LOG2E
    m1q = jnp.concatenate([o64, cos, cos, z32], axis=1) * q_scale
    m2q = m2k * q_scale

    row = lambda w: pl.BlockSpec((TM, w), lambda i: (i, 0))
    tab = pl.BlockSpec((TM, LANES), lambda i: (i % nsb, 0))
    out_w = (1024, 1024, 512, 512, 128, 128)
    return pl.pallas_call(
        _proj_ab_kernel,
        grid=(t // TM,),
        in_specs=[row(D_MODEL), _const_spec((D_MODEL, _AB_COLS)), _const_spec((1, MLA_Q_RANK)),
                  _const_spec((MLA_Q_RANK, 1024)), _const_spec((1, MLA_KV_RANK)),
                  _const_spec((MLA_KV_RANK, 1536)), tab, tab, tab, tab],
        out_specs=[row(w) for w in out_w],
        out_shape=[jax.ShapeDtypeStruct((t, w), BF16) for w in out_w],
        compiler_params=pltpu.CompilerParams(dimension_semantics=("arbitrary",),
                                             vmem_limit_bytes=VMEM_LIMIT),
        name="proj_ab",
    )(x2, win_p, q_norm.reshape(1, -1), wuq_p, kv_norm.reshape(1, -1), wukv_p, m1q, m2q, m1k, m2k)


def _proj_cd_kernel(x_ref, win_ref, bf_ref, qkv_ref, cum_ref, carry_ref, *, nsb):
    @pl.when(pl.program_id(0) % nsb == 0)
    def _():
        carry_ref[...] = jnp.zeros_like(carry_ref)

    proj = _dot(x_ref[...].astype(BF16), win_ref[...])
    q_scale = HEAD_DIM ** -0.5 * LOG2E
    for blk, scale in enumerate((q_scale, None, None, q_scale, None, None)):
        part = proj[:, blk * 512:(blk + 1) * 512]
        qkv_ref[:, blk * 512:(blk + 1) * 512] = (part if scale is None else part * scale).astype(BF16)
    z = proj[:, 3072:] + bf_ref[...]
    log_f = jnp.minimum(z, 0.0) - jnp.log(1.0 + jnp.exp(-jnp.abs(z)))
    r = lax.broadcasted_iota(jnp.int32, (TM, TM), 0)
    c = lax.broadcasted_iota(jnp.int32, (TM, TM), 1)
    tri = jnp.where(c <= r, 1.0, 0.0).astype(BF16)
    hi = log_f.astype(BF16)
    rem = log_f - hi.astype(F32)
    mid = rem.astype(BF16)
    lo = (rem - mid.astype(F32)).astype(BF16)
    cum = _dot(tri, hi) + _dot(tri, mid) + _dot(tri, lo) + carry_ref[...]
    cum_ref[...] = cum
    carry_ref[...] = cum[TM - 1:TM, :]


def _proj_cd(x2, w_in, b_forget, seq):
    t = x2.shape[0]
    q_f, k_f, v_f, f_l, q_c, k_c, v_c = jnp.split(w_in, [512, 1024, 1536, 1544, 2056, 2568], axis=1)
    f_pad = jnp.concatenate([f_l, jnp.zeros((D_MODEL, LANES - N_HEADS), F32)], axis=1)
    win_p = jnp.concatenate([q_f, k_f, v_f, q_c, k_c, v_c, f_pad], axis=1).astype(BF16)
    bf = jnp.concatenate([b_forget, jnp.zeros((LANES - N_HEADS,), F32)]).reshape(1, LANES)
    return pl.pallas_call(
        functools.partial(_proj_cd_kernel, nsb=seq // TM),
        grid=(t // TM,),
        in_specs=[pl.BlockSpec((TM, D_MODEL), lambda i: (i, 0)),
                  _const_spec((D_MODEL, 3072 + LANES)), _const_spec((1, LANES))],
        out_specs=[pl.BlockSpec((TM, 3072), lambda i: (i, 0)),
                   pl.BlockSpec((TM, LANES), lambda i: (i, 0))],
        out_shape=[jax.ShapeDtypeStruct((t, 3072), BF16), jax.ShapeDtypeStruct((t, LANES), F32)],
        scratch_shapes=[pltpu.VMEM((1, LANES), F32)],
        compiler_params=pltpu.CompilerParams(dimension_semantics=("arbitrary",),
                                             vmem_limit_bytes=VMEM_LIMIT),
        name="proj_cd",
    )(x2, win_p, bf)


def _attn_kernel(*refs, variant, window, seq, tq, tk, n_sub):
    if variant == "mla":
        q_ref, k_ref, v_ref, o_ref, vt_ref, m_ref, acc_ref, sa_ref, sb_ref = refs
    elif variant == "fox":
        q_ref, k_ref, v_ref, ck_ref, o_ref, vt_ref, m_ref, acc_ref, sa_ref, sb_ref, ckrep_ref = refs
    elif variant == "swa":
        sink_ref, q_ref, k_ref, v_ref, bias_ref, o_ref, vt_ref, kpad_ref = refs
    else:
        q_ref, k_ref, v_ref, bias_ref, o_ref, vt_ref, kpad_ref = refs

    i = pl.program_id(2)
    banded = window is not None
    pad = window - tq if banded else 0

    @pl.when(i == 0)
    def _():
        for c in range(seq // 512):
            sl = slice(pad + c * 512, pad + (c + 1) * 512)
            vt_ref[0:LANES, sl] = v_ref[0, c * 512:(c + 1) * 512, :].astype(F32).T.astype(BF16)
        vt_ref[LANES:VT_ROWS, :] = jnp.ones((VT_ROWS - LANES, pad + seq), BF16)
        if banded:
            vt_ref[0:LANES, 0:pad] = jnp.zeros((LANES, pad), BF16)
            kpad_ref[0:pad, :] = jnp.zeros((pad, LANES), BF16)
            kpad_ref[pad:pad + seq, :] = k_ref[0]
        if variant == "fox":
            ck = ck_ref[0, 0]
            ckrep_ref[:, 0:LANES] = jnp.broadcast_to(ck[:, 0:1], (seq, LANES))
            ckrep_ref[:, LANES:2 * LANES] = jnp.broadcast_to(ck[:, 1:2], (seq, LANES))

    q_t = q_ref[0].astype(F32).T
    sub = lax.broadcasted_iota(jnp.int32, (LANES, tq), 0)

    def pair_columns(q_blk):
        return jnp.concatenate([jnp.where(sub < HEAD_DIM, q_blk, 0.0),
                                jnp.where(sub >= HEAD_DIM, q_blk, 0.0)], axis=1).astype(BF16)

    def finish(acc, m, rows):
        l = acc[LANES:LANES + 1, :]
        o_t = acc[:LANES, :]
        if variant == "swa":
            pair = pl.program_id(0)
            col = lax.broadcasted_iota(jnp.int32, (1, 2 * tq), 1)
            sink = jnp.where(col < tq, sink_ref[pair], sink_ref[pair + N_HEADS // 2])
            m_fin = jnp.maximum(m, sink)
            scale = jnp.exp2(m - m_fin)
            l = l * scale + jnp.exp2(sink - m_fin)
            o_t = o_t * scale
        o_t = o_t / l
        o_ref[0, rows, :] = jnp.where(sub < HEAD_DIM, o_t[:, :tq], o_t[:, tq:]).T.astype(o_ref.dtype)

    if banded:
        n_var = bias_ref.shape[1]
        for sb in range(n_sub):
            ib = i * n_sub + sb
            start = pl.multiple_of(ib * tq, tq)
            rows = slice(sb * tq, (sb + 1) * tq)
            v = jnp.maximum(n_var - 1 - ib, 0)
            s = _dot(kpad_ref[pl.ds(start, window), :], pair_columns(q_t[:, rows])) + bias_ref[0, v]
            m = jnp.max(s, axis=0, keepdims=True)
            p = jnp.exp2(s - m).astype(BF16)
            finish(_dot(vt_ref[:, pl.ds(start, window)], p), m, rows)
        return

    if variant == "mla":
        q_a, q_b = q_t[:LANES].astype(BF16), q_t[LANES:].astype(BF16)
    else:
        q_pair = pair_columns(q_t)

    m_ref[...] = jnp.full_like(m_ref, MASK)
    acc_ref[...] = jnp.zeros_like(acc_ref)

    def scores(j, s_ref):
        start = pl.multiple_of(j * tk, tk)
        k = k_ref[0, pl.ds(start, tk), :]
        if variant == "mla":
            s_ref[:, 0:tq] = _dot(k[:, :LANES], q_a)
            s_ref[:, tq:2 * tq] = _dot(k[:, LANES:], q_b)
        else:
            for h in range(2):
                cols = slice(h * tq, (h + 1) * tq)
                s = _dot(k, q_pair[:, cols])
                if variant == "fox":
                    ck_h = ckrep_ref[pl.ds(start, tk), h * LANES:(h + 1) * LANES]
                    s = s - jnp.concatenate([ck_h] * (tq // LANES), axis=1)
                s_ref[:, cols] = s

    def consume(j, s_ref, diagonal):
        start = pl.multiple_of(j * tk, tk)
        v_t = vt_ref[:, pl.ds(start, tk)]
        for h in range(2):
            cols = slice(h * tq, (h + 1) * tq)
            s = s_ref[:, cols]
            if diagonal:
                key = lax.broadcasted_iota(jnp.int32, (tk, tq), 0)
                qry = lax.broadcasted_iota(jnp.int32, (tk, tq), 1)
                if variant == "mla":
                    keep = (key >> 6) <= (qry >> 6)
                else:
                    keep = key <= qry
                s = jnp.where(keep, s, MASK)
            m_prev = m_ref[:, cols]
            m_new = jnp.maximum(m_prev, jnp.max(s, axis=0, keepdims=True))
            alpha = jnp.exp2(m_prev - m_new)
            p = jnp.exp2(s - m_new).astype(BF16)
            acc_ref[:, cols] = alpha * acc_ref[:, cols] + _dot(v_t, p)
            m_ref[:, cols] = m_new

    scores(0, sa_ref)

    def two_blocks(u, carry):
        t = 2 * u
        scores(t + 1, sb_ref)
        consume(t, sa_ref, False)
        scores(t + 2, sa_ref)
        consume(t + 1, sb_ref, False)
        return carry

    lax.fori_loop(0, i >> 1, two_blocks, None)

    @pl.when((i & 1) == 0)
    def _():
        consume(i, sa_ref, True)

    @pl.when((i & 1) == 1)
    def _():
        scores(i, sb_ref)
        consume(i - 1, sa_ref, False)
        consume(i, sb_ref, True)

    finish(acc_ref[...], m_ref[...], slice(0, tq))


def _attention(variant, q, k, v, q_col, k_col, v_col, *, extra=None, sinks=None):
    b, s, _ = q.shape
    n_pairs = N_HEADS // 2
    tq = tk = FULL_TILE if variant in ("mla", "fox") else TQ
    n_sub = 1 if variant in ("mla", "fox") else BAND_SUB_BLOCKS
    nq = s // (tq * n_sub)
    window = {"mla": None, "fox": None, "swa": SWA_WINDOW_KEYS, "ck": CK_WINDOW_KEYS}[variant]
    qw = 2 * LANES if variant == "mla" else LANES
    shared_kv = variant == "swa"

    def im(f):
        if sinks is None:
            return f
        return lambda p, bi, i, sink_ref: f(p, bi, i)

    in_specs = [
        pl.BlockSpec((1, tq * n_sub, qw), im(lambda p, bi, i: (bi, i, q_col + p))),
        pl.BlockSpec((1, s, qw), im(lambda p, bi, i: (bi, 0, k_col + (0 if shared_kv else p)))),
        pl.BlockSpec((1, s, LANES), im(lambda p, bi, i: (bi, 0, v_col + (0 if shared_kv else p)))),
    ]
    args = [q, k, v]
    if window is None:
        scratch = [pltpu.VMEM((VT_ROWS, s), BF16), pltpu.VMEM((1, 2 * tq), F32),
                   pltpu.VMEM((VT_ROWS, 2 * tq), F32),
                   pltpu.VMEM((tk, 2 * tq), F32), pltpu.VMEM((tk, 2 * tq), F32)]
    else:
        pad = window - tq
        scratch = [pltpu.VMEM((VT_ROWS, pad + s), BF16), pltpu.VMEM((pad + s, LANES), BF16)]
        in_specs.append(pl.BlockSpec((1,) + extra.shape[1:], im(lambda p, bi, i: (p, 0, 0, 0))))
        args.append(extra)
    if variant == "fox":
        in_specs.append(pl.BlockSpec((1, 1, s, 2), lambda p, bi, i: (bi, p, 0, 0)))
        args.append(extra)
        scratch.append(pltpu.VMEM((s, 2 * LANES), F32))
    out_spec = pl.BlockSpec((1, tq * n_sub, LANES), im(lambda p, bi, i: (bi, i, p)))
    kern = functools.partial(_attn_kernel, variant=variant, window=window, seq=s, tq=tq, tk=tk,
                             n_sub=n_sub)
    params = pltpu.CompilerParams(dimension_semantics=("arbitrary",) * 3, vmem_limit_bytes=VMEM_LIMIT)
    out_shape = jax.ShapeDtypeStruct((b, s, n_pairs * LANES), BF16)
    if sinks is not None:
        grid_spec = pltpu.PrefetchScalarGridSpec(
            num_scalar_prefetch=1, grid=(n_pairs, b, nq), in_specs=in_specs, out_specs=out_spec,
            scratch_shapes=scratch)
        return pl.pallas_call(kern, grid_spec=grid_spec, out_shape=out_shape,
                              compiler_params=params, name="attn_" + variant)(sinks, *args)
    return pl.pallas_call(kern, grid=(n_pairs, b, nq), in_specs=in_specs, out_specs=out_spec,
                          out_shape=out_shape, scratch_shapes=scratch,
                          compiler_params=params, name="attn_" + variant)(*args)


def _lead_rows(window, n_var, v):
    return max(window - TQ - (n_var - 1 - v) * TQ, 0)


def _swa_bias():
    window, left = SWA_WINDOW_KEYS, SWA_LEFT_CHUNKS
    c = jnp.arange(window)[:, None]
    a = jnp.arange(TQ)[None, :]
    dist = a - c + (window - TQ)
    ahead = (a // CHUNK + left) - c // CHUNK
    slopes = jnp.exp2(-8.0 * jnp.arange(1, N_HEADS + 1, dtype=F32) / N_HEADS)
    bias = -slopes[:, None, None] * jnp.abs(dist).astype(F32)[None] * LOG2E
    bias = jnp.where(((ahead >= 0) & (ahead <= left))[None], bias, MASK)
    bias = jnp.concatenate([bias[:4], bias[4:]], axis=2)
    n_var = 2
    return jnp.stack([jnp.where(c[None] >= _lead_rows(window, n_var, v), bias, MASK)
                      for v in range(n_var)], axis=1)


_REL_ROWS = 384


_CK_VARIANTS = 3


def _ck_bias_kernel(tab_ref, out_ref):
    window = CK_WINDOW_KEYS
    span = TQ + window
    tab = tab_ref[0] * LOG2E
    hi = tab.astype(BF16)
    rem = tab - hi.astype(F32)
    mid = rem.astype(BF16)
    lo = (rem - mid.astype(F32)).astype(BF16)
    t = lax.broadcasted_iota(jnp.int32, (_REL_ROWS, span), 0)
    n = lax.broadcasted_iota(jnp.int32, (_REL_ROWS, span), 1)
    delta = jnp.where(n < TQ, n, n - span)
    idx = jnp.clip(delta + (window - TQ), -(CHUNK - 1), REL_MAX) + (CHUNK - 1)
    onehot = jnp.where(t == idx, 1.0, 0.0).astype(BF16)
    base = _dot(hi, onehot) + _dot(mid, onehot) + _dot(lo, onehot)
    c = lax.broadcasted_iota(jnp.int32, (window, TQ), 0)
    a = lax.broadcasted_iota(jnp.int32, (window, TQ), 1)
    valid = ((a >> 6) + CK_LEFT_CHUNKS - (c >> 6)).astype(jnp.uint32) <= CK_LEFT_CHUNKS
    for h in range(2):
        rows = jnp.broadcast_to(base[h:h + 1, :], (window, span))
        tile = jnp.where(valid, pltpu.roll(rows, 0, 1, stride=1, stride_axis=0)[:, :TQ], MASK)
        for v in range(_CK_VARIANTS):
            lead = _lead_rows(window, _CK_VARIANTS, v)
            out_ref[0, v, :, h * TQ:(h + 1) * TQ] = jnp.where(c >= lead, tile, MASK)


def _ck_bias(rel_bias):
    tab = jnp.pad(rel_bias.T, ((0, 0), (0, _REL_ROWS - rel_bias.shape[0])))
    tab = jnp.pad(tab.reshape(N_HEADS // 2, 2, _REL_ROWS), ((0, 0), (0, 6), (0, 0)))
    return pl.pallas_call(
        _ck_bias_kernel,
        grid=(N_HEADS // 2,),
        in_specs=[pl.BlockSpec((1, 8, _REL_ROWS), lambda p: (p, 0, 0))],
        out_specs=pl.BlockSpec((1, _CK_VARIANTS, CK_WINDOW_KEYS, 2 * TQ), lambda p: (p, 0, 0, 0)),
        out_shape=jax.ShapeDtypeStruct((N_HEADS // 2, _CK_VARIANTS, CK_WINDOW_KEYS, 2 * TQ), F32),
        compiler_params=pltpu.CompilerParams(dimension_semantics=("arbitrary",),
                                             vmem_limit_bytes=VMEM_LIMIT),
        name="ck_bias",
    )(tab)


def _out_ffn_kernel(x_ref, oa_ref, ob_ref, woa_ref, wob_ref, g1_ref, b1_ref,
                    wg_ref, wu_ref, wd_ref, g2_ref, b2_ref, y_ref):
    mix = _dot(oa_ref[...], woa_ref[...]) + _dot(ob_ref[...], wob_ref[...])
    x1 = _layer_norm(DEEPNORM_ALPHA * x_ref[...] + mix, g1_ref[...], b1_ref[...])
    x1b = x1.astype(BF16)
    gate = _dot(x1b, wg_ref[...])
    up = _dot(x1b, wu_ref[...])
    hidden = (gate * jax.nn.sigmoid(gate) * up).astype(BF16)
    ffn = _dot(hidden, wd_ref[...])
    y_ref[...] = _layer_norm(DEEPNORM_ALPHA * x1 + ffn, g2_ref[...], b2_ref[...])


def _out_ffn(x2, o_a, o_b, w_out_a, w_out_b, g1, b1, w_gate, w_up, w_down, g2, b2):
    t = x2.shape[0]
    d_ff = w_gate.shape[1]
    row = lambda w: pl.BlockSpec((TM, w), lambda i: (i, 0))
    vec = _const_spec((1, D_MODEL))
    return pl.pallas_call(
        _out_ffn_kernel,
        grid=(t // TM,),
        in_specs=[row(D_MODEL), row(512), row(512), _const_spec((512, D_MODEL)),
                  _const_spec((512, D_MODEL)), vec, vec, _const_spec((D_MODEL, d_ff)),
                  _const_spec((D_MODEL, d_ff)), _const_spec((d_ff, D_MODEL)), vec, vec],
        out_specs=row(D_MODEL),
        out_shape=jax.ShapeDtypeStruct((t, D_MODEL), F32),
        compiler_params=pltpu.CompilerParams(dimension_semantics=("arbitrary",),
                                             vmem_limit_bytes=VMEM_LIMIT),
        name="out_ffn",
    )(x2, o_a, o_b, w_out_a.astype(BF16), w_out_b.astype(BF16), g1.reshape(1, -1), b1.reshape(1, -1),
      w_gate.astype(BF16), w_up.astype(BF16), w_down.astype(BF16), g2.reshape(1, -1), b2.reshape(1, -1))


def kernel(x, ab_w_in, ab_q_norm, ab_w_uq, ab_kv_norm, ab_w_ukv, ab_sinks, ab_w_out,
           cd_w_in, cd_b_forget, cd_rel_bias, cd_w_out,
           ln1_g, ln1_b, ffn_w_gate, ffn_w_up, ffn_w_down, ln2_g, ln2_b):
    b, s, d = x.shape
    assert d == D_MODEL and TQ == TK
    assert s % TM == 0 and s % FULL_TILE == 0 and s % (TQ * BAND_SUB_BLOCKS) == 0
    t = b * s
    x2 = x.reshape(t, d)

    qa, ka, va, qs, ks, vs = _proj_ab(x2, ab_w_in[0], ab_q_norm[0], ab_w_uq[0], ab_kv_norm[0],
                                      ab_w_ukv[0], s)
    r3 = lambda a: a.reshape(b, s, a.shape[-1])
    o_a = _attention("mla", r3(qa), r3(ka), r3(va), 0, 0, 0)
    o_b = _attention("swa", r3(qs), r3(ks), r3(vs), 0, 0, 0, extra=_swa_bias(),
                     sinks=ab_sinks[0] * LOG2E)
    w_out = ab_w_out[0]
    w_out_b = w_out[512:].reshape(2, 4, HEAD_DIM, D_MODEL).transpose(1, 0, 2, 3).reshape(512, D_MODEL)
    x2 = _out_ffn(x2, o_a.reshape(t, 512), o_b.reshape(t, 512), w_out[:512], w_out_b,
                  ln1_g[0], ln1_b[0], ffn_w_gate[0], ffn_w_up[0], ffn_w_down[0], ln2_g[0], ln2_b[0])

    qkv, cum = _proj_cd(x2, cd_w_in[0], cd_b_forget[0], s)
    qkv = qkv.reshape(b, s, 3072)
    ck = (cum[:, :N_HEADS] * LOG2E).reshape(b, s, N_HEADS // 2, 2).transpose(0, 2, 1, 3)
    o_c = _attention("fox", qkv, qkv, qkv, 0, 4, 8, extra=ck)
    o_d = _attention("ck", qkv, qkv, qkv, 12, 16, 20, extra=_ck_bias(cd_rel_bias[0]))
    w_out = cd_w_out[0]
    x2 = _out_ffn(x2, o_c.reshape(t, 512), o_d.reshape(t, 512), w_out[:512], w_out[512:],
                  ln1_g[1], ln1_b[1], ffn_w_gate[1], ffn_w_up[1], ffn_w_down[1], ln2_g[1], ln2_b[1])
    return x2.reshape(b, s, d)
```

```python
import functools
import math

import numpy as np
import jax
import jax.numpy as jnp
from jax import lax
from jax.experimental import pallas as pl
from jax.experimental.pallas import tpu as pltpu

D_MODEL = 1024
CHUNK = 64
HEAD_DIM = 64
N_HEADS = 8
LANES = 128
LN_EPS = 1e-5
RMS_EPS = 1e-6
MASK = -1e30
LOG2E = math.log2(math.e)

MLA_Q_RANK = 384
MLA_KV_RANK = 256
MLA_NOPE = 64
MLA_ROPE = 32
ROPE_THETA = 10000.0
SWA_KV_HEADS = 2
SWA_LEFT_CHUNKS = 2
CK_LEFT_CHUNKS = 8
REL_MAX = 256
DEPTH = 2
DEEPNORM_ALPHA = (2 * DEPTH) ** 0.25

TQ = 256
TK = 256
BAND_SUB_BLOCKS = 8
SWA_WINDOW_KEYS = SWA_LEFT_CHUNKS * CHUNK + TQ
CK_WINDOW_KEYS = CK_LEFT_CHUNKS * CHUNK + TQ
FULL_TILE = 512
TM = 512
VT_ROWS = LANES + 16
VMEM_LIMIT = 56 * 1024 * 1024

BF16 = jnp.bfloat16
F32 = jnp.float32


def _dot(a, b):
    return jnp.dot(a, b, preferred_element_type=F32)


def _dot_nt(a, b):
    return lax.dot_general(a, b, (((1,), (1,)), ((), ())), preferred_element_type=F32)


def _layer_norm(y, g, b):
    mu = jnp.mean(y, axis=-1, keepdims=True)
    yc = y - mu
    var = jnp.mean(yc * yc, axis=-1, keepdims=True)
    return yc * lax.rsqrt(var + LN_EPS) * g + b


def _rms_norm(c, g):
    return c * lax.rsqrt(jnp.mean(c * c, axis=-1, keepdims=True) + RMS_EPS) * g


def _const_spec(shape):
    zeros = (0,) * len(shape)
    return pl.BlockSpec(shape, lambda *_: zeros, pipeline_mode=pl.Buffered(1))


_AB_CQ, _AB_CKV, _AB_QS, _AB_KS, _AB_VS, _AB_KR, _AB_COLS = 0, 384, 640, 1152, 1280, 1408, 1536


def _rope_mix(t, m1, m2):
    return t * m1 + pltpu.roll(t, 96, 1) * m2


def _proj_ab_kernel(x_ref, win_ref, qn_ref, wuq_ref, kvn_ref, wukv_ref,
                    m1q_ref, m2q_ref, m1k_ref, m2k_ref,
                    qa_ref, ka_ref, va_ref, qs_ref, ks_ref, vs_ref):
    proj = _dot(x_ref[...].astype(BF16), win_ref[...])
    c_q = _rms_norm(proj[:, _AB_CQ:_AB_CKV], qn_ref[...])
    qf = _dot(c_q.astype(BF16), wuq_ref[...])
    c_kv = _rms_norm(proj[:, _AB_CKV:_AB_QS], kvn_ref[...])
    kvf = _dot(c_kv.astype(BF16), wukv_ref[...])
    k_rope = _rope_mix(proj[:, _AB_KR:_AB_COLS], m1k_ref[...], m2k_ref[...])
    m1q, m2q = m1q_ref[...], m2q_ref[...]
    for h in range(N_HEADS):
        sl = slice(h * LANES, (h + 1) * LANES)
        qa_ref[:, sl] = _rope_mix(qf[:, sl], m1q, m2q).astype(BF16)
        ka_ref[:, sl] = (kvf[:, sl] + k_rope).astype(BF16)
    va_ref[...] = kvf[:, N_HEADS * LANES:].astype(BF16)
    qs_ref[...] = (proj[:, _AB_QS:_AB_KS] * (HEAD_DIM ** -0.5 * LOG2E)).astype(BF16)
    ks_ref[...] = proj[:, _AB_KS:_AB_VS].astype(BF16)
    vs_ref[...] = proj[:, _AB_VS:_AB_KR].astype(BF16)


def _proj_ab(x2, w_in, q_norm, w_uq, kv_norm, w_ukv, seq):
    t = x2.shape[0]
    nsb = seq // TM
    c_q, c_kv, k_r, q_s, k_s, v_s = jnp.split(w_in, [384, 640, 672, 1184, 1312], axis=1)
    q_s = q_s.reshape(D_MODEL, 2, 4, HEAD_DIM).transpose(0, 2, 1, 3).reshape(D_MODEL, 512)
    kr_blk = jnp.concatenate([jnp.zeros((D_MODEL, 64), F32), k_r, k_r[:, 16:], k_r[:, :16]], axis=1)
    win_p = jnp.concatenate([c_q, c_kv, q_s, k_s, v_s, kr_blk], axis=1).astype(BF16)
    wq = w_uq.reshape(MLA_Q_RANK, N_HEADS, MLA_NOPE + MLA_ROPE)
    wuq_p = jnp.concatenate([wq, wq[..., 80:96], wq[..., 64:80]], axis=-1)
    wuq_p = wuq_p.reshape(MLA_Q_RANK, N_HEADS * LANES).astype(BF16)
    wkv = w_ukv.reshape(MLA_KV_RANK, N_HEADS, 2 * HEAD_DIM)
    k_pad = jnp.concatenate([wkv[..., :64], jnp.zeros_like(wkv[..., :64])], axis=-1)
    wukv_p = jnp.concatenate([k_pad.reshape(MLA_KV_RANK, N_HEADS * LANES),
                              wkv[..., 64:].reshape(MLA_KV_RANK, 512)], axis=1).astype(BF16)
    inv = ROPE_THETA ** (-jnp.arange(0, MLA_ROPE, 2, dtype=F32) / MLA_ROPE)
    ang = jnp.arange(seq, dtype=F32)[:, None] * inv[None, :]
    cos, sin = jnp.cos(ang), jnp.sin(ang)
    z64, z32, o64 = jnp.zeros((seq, 64), F32), jnp.zeros((seq, 32), F32), jnp.ones((seq, 64), F32)
    m1k = jnp.concatenate([z64, cos, cos, z32], axis=1)
    m2k = jnp.concatenate([z64, -sin, sin, z32], axis=1)
    q_scale = (MLA_NOPE + MLA_ROPE) ** -0.5 * LOG2E
    m1q = jnp.concatenate([o64, cos, cos, z32], axis=1) * q_scale
    m2q = m2k * q_scale

    row = lambda w: pl.BlockSpec((TM, w), lambda i: (i, 0))
    tab = pl.BlockSpec((TM, LANES), lambda i: (i % nsb, 0))
    out_w = (1024, 1024, 512, 512, 128, 128)
    return pl.pallas_call(
        _proj_ab_kernel,
        grid=(t // TM,),
        in_specs=[row(D_MODEL), _const_spec((D_MODEL, _AB_COLS)), _const_spec((1, MLA_Q_RANK)),
                  _const_spec((MLA_Q_RANK, 1024)), _const_spec((1, MLA_KV_RANK)),
                  _const_spec((MLA_KV_RANK, 1536)), tab, tab, tab, tab],
        out_specs=[row(w) for w in out_w],
        out_shape=[jax.ShapeDtypeStruct((t, w), BF16) for w in out_w],
        compiler_params=pltpu.CompilerParams(dimension_semantics=("arbitrary",),
                                             vmem_limit_bytes=VMEM_LIMIT),
        name="proj_ab",
    )(x2, win_p, q_norm.reshape(1, -1), wuq_p, kv_norm.reshape(1, -1), wukv_p, m1q, m2q, m1k, m2k)


def _proj_cd_kernel(x_ref, win_ref, bf_ref, qkv_ref, cum_ref, carry_ref, *, nsb):
    @pl.when(pl.program_id(0) % nsb == 0)
    def _():
        carry_ref[...] = jnp.zeros_like(carry_ref)

    proj = _dot(x_ref[...].astype(BF16), win_ref[...])
    q_scale = HEAD_DIM ** -0.5 * LOG2E
    for blk, scale in enumerate((q_scale, None, None, q_scale, None, None)):
        part = proj[:, blk * 512:(blk + 1) * 512]
        qkv_ref[:, blk * 512:(blk + 1) * 512] = (part if scale is None else part * scale).astype(BF16)
    z = proj[:, 3072:] + bf_ref[...]
    log_f = jnp.minimum(z, 0.0) - jnp.log(1.0 + jnp.exp(-jnp.abs(z)))
    r = lax.broadcasted_iota(jnp.int32, (TM, TM), 0)
    c = lax.broadcasted_iota(jnp.int32, (TM, TM), 1)
    tri = jnp.where(c <= r, 1.0, 0.0).astype(BF16)
    hi = log_f.astype(BF16)
    rem = log_f - hi.astype(F32)
    mid = rem.astype(BF16)
    lo = (rem - mid.astype(F32)).astype(BF16)
    cum = _dot(tri, hi) + _dot(tri, mid) + _dot(tri, lo) + carry_ref[...]
    cum_ref[...] = cum
    carry_ref[...] = cum[TM - 1:TM, :]


def _proj_cd(x2, w_in, b_forget, seq):
    t = x2.shape[0]
    q_f, k_f, v_f, f_l, q_c, k_c, v_c = jnp.split(w_in, [512, 1024, 1536, 1544, 2056, 2568], axis=1)
    f_pad = jnp.concatenate([f_l, jnp.zeros((D_MODEL, LANES - N_HEADS), F32)], axis=1)
    win_p = jnp.concatenate([q_f, k_f, v_f, q_c, k_c, v_c, f_pad], axis=1).astype(BF16)
    bf = jnp.concatenate([b_forget, jnp.zeros((LANES - N_HEADS,), F32)]).reshape(1, LANES)
    return pl.pallas_call(
        functools.partial(_proj_cd_kernel, nsb=seq // TM),
        grid=(t // TM,),
        in_specs=[pl.BlockSpec((TM, D_MODEL), lambda i: (i, 0)),
                  _const_spec((D_MODEL, 3072 + LANES)), _const_spec((1, LANES))],
        out_specs=[pl.BlockSpec((TM, 3072), lambda i: (i, 0)),
                   pl.BlockSpec((TM, LANES), lambda i: (i, 0))],
        out_shape=[jax.ShapeDtypeStruct((t, 3072), BF16), jax.ShapeDtypeStruct((t, LANES), F32)],
        scratch_shapes=[pltpu.VMEM((1, LANES), F32)],
        compiler_params=pltpu.CompilerParams(dimension_semantics=("arbitrary",),
                                             vmem_limit_bytes=VMEM_LIMIT),
        name="proj_cd",
    )(x2, win_p, bf)


def _attn_kernel(*refs, variant, window, seq, tq, tk, n_sub):
    if variant == "mla":
        q_ref, k_ref, v_ref, o_ref, vt_ref, m_ref, acc_ref, sa_ref, sb_ref = refs
    elif variant == "fox":
        q_ref, k_ref, v_ref, ck_ref, o_ref, vt_ref, m_ref, acc_ref, sa_ref, sb_ref, ckrep_ref = refs
    elif variant == "swa":
        sink_ref, q_ref, k_ref, v_ref, bias_ref, o_ref, vt_ref, kpad_ref = refs
    else:
        q_ref, k_ref, v_ref, bias_ref, o_ref, vt_ref, kpad_ref = refs

    i = pl.program_id(2)
    banded = window is not None
    pad = window - tq if banded else 0

    @pl.when(i == 0)
    def _():
        for c in range(seq // 512):
            sl = slice(pad + c * 512, pad + (c + 1) * 512)
            vt_ref[0:LANES, sl] = v_ref[0, c * 512:(c + 1) * 512, :].astype(F32).T.astype(BF16)
        vt_ref[LANES:VT_ROWS, :] = jnp.ones((VT_ROWS - LANES, pad + seq), BF16)
        if banded:
            vt_ref[0:LANES, 0:pad] = jnp.zeros((LANES, pad), BF16)
            kpad_ref[0:pad, :] = jnp.zeros((pad, LANES), BF16)
            kpad_ref[pad:pad + seq, :] = k_ref[0]
        if variant == "fox":
            ck = ck_ref[0, 0]
            ckrep_ref[:, 0:LANES] = jnp.broadcast_to(ck[:, 0:1], (seq, LANES))
            ckrep_ref[:, LANES:2 * LANES] = jnp.broadcast_to(ck[:, 1:2], (seq, LANES))

    sub = lax.broadcasted_iota(jnp.int32, (LANES, tq), 0)
    lane = lax.broadcasted_iota(jnp.int32, (tq, LANES), 1)

    def pair_rows(q_blk):
        zero = jnp.zeros_like(q_blk)
        return jnp.concatenate([jnp.where(lane < HEAD_DIM, q_blk, zero),
                                jnp.where(lane >= HEAD_DIM, q_blk, zero)], axis=0)

    def finish(acc, m, rows):
        l = acc[LANES:LANES + 1, :]
        o_t = acc[:LANES, :]
        if variant == "swa":
            pair = pl.program_id(0)
            col = lax.broadcasted_iota(jnp.int32, (1, 2 * tq), 1)
            sink = jnp.where(col < tq, sink_ref[pair], sink_ref[pair + N_HEADS // 2])
            m_fin = jnp.maximum(m, sink)
            scale = jnp.exp2(m - m_fin)
            l = l * scale + jnp.exp2(sink - m_fin)
            o_t = o_t * scale
        o_t = o_t / l
        o_ref[0, rows, :] = jnp.where(sub < HEAD_DIM, o_t[:, :tq], o_t[:, tq:]).T.astype(o_ref.dtype)

    if banded:
        n_var = bias_ref.shape[1]
        for sb in range(n_sub):
            ib = i * n_sub + sb
            start = pl.multiple_of(ib * tq, tq)
            rows = slice(sb * tq, (sb + 1) * tq)
            v = jnp.maximum(n_var - 1 - ib, 0)
            s = _dot_nt(kpad_ref[pl.ds(start, window), :], pair_rows(q_ref[0, rows, :])) + bias_ref[0, v]
            m = jnp.max(s, axis=0, keepdims=True)
            p = jnp.exp2(s - m).astype(BF16)
            finish(_dot(vt_ref[:, pl.ds(start, window)], p), m, rows)
        return

    if variant == "mla":
        q_a, q_b = q_ref[0, :, :LANES], q_ref[0, :, LANES:]
    else:
        q_pair = pair_rows(q_ref[0])

    m_ref[...] = jnp.full_like(m_ref, MASK)
    acc_ref[...] = jnp.zeros_like(acc_ref)

    first_half = [(h * tq, tk) for h in range(2)]
    second_half = [(h * tq + tk, tk) for h in range(2)]

    def scores(j, s_ref, slabs=((0, 2 * tq),)):
        start = pl.multiple_of(j * tk, tk)
        k = k_ref[0, pl.ds(start, tk), :]
        for c0, w in slabs:
            cols = slice(c0, c0 + w)
            if variant == "mla":
                for h in range(2):
                    lo, hi = max(c0, h * tq), min(c0 + w, (h + 1) * tq)
                    if lo < hi:
                        q_h = (q_a, q_b)[h][lo - h * tq:hi - h * tq, :]
                        s_ref[:, lo:hi] = _dot_nt(k[:, h * LANES:(h + 1) * LANES], q_h)
                continue
            s = _dot_nt(k, q_pair[cols, :])
            if variant == "fox":
                ck_reps = [ckrep_ref[pl.ds(start, tk), (c // tq) * LANES:(c // tq + 1) * LANES]
                           for c in range(c0, c0 + w, LANES)]
                s = s - jnp.concatenate(ck_reps, axis=1)
            s_ref[:, cols] = s

    def consume(j, s_ref, slabs, masked=()):
        start = pl.multiple_of(j * tk, tk)
        v_t = vt_ref[:, pl.ds(start, tk)]
        for c0, w in slabs:
            cols = slice(c0, c0 + w)
            s = s_ref[:, cols]
            if (c0, w) in masked:
                key = lax.broadcasted_iota(jnp.int32, (tk, w), 0)
                qry = lax.broadcasted_iota(jnp.int32, (tk, w), 1)
                if variant == "mla":
                    keep = (key >> 6) <= (qry >> 6)
                else:
                    keep = key <= qry
                s = jnp.where(keep, s, MASK)
            m_prev = m_ref[:, cols]
            m_new = jnp.maximum(m_prev, jnp.max(s, axis=0, keepdims=True))
            alpha = jnp.exp2(m_prev - m_new)
            p = jnp.exp2(s - m_new).astype(BF16)
            acc_ref[:, cols] = alpha * acc_ref[:, cols] + _dot(v_t, p)
            m_ref[:, cols] = m_new

    everything = [(0, 2 * tq)]
    scores(0, sa_ref)

    def two_blocks(u, carry):
        t = 2 * u
        scores(t + 1, sb_ref)
        consume(t, sa_ref, everything)
        scores(t + 2, sa_ref)
        consume(t + 1, sb_ref, everything)
        return carry

    lax.fori_loop(0, i, two_blocks, None)
    scores(2 * i + 1, sb_ref, second_half)
    consume(2 * i, sa_ref, first_half + second_half, masked=first_half)
    consume(2 * i + 1, sb_ref, second_half, masked=second_half)
    finish(acc_ref[...], m_ref[...], slice(0, tq))


def _attention(variant, q, k, v, q_col, k_col, v_col, *, extra=None, sinks=None):
    b, s, _ = q.shape
    n_pairs = N_HEADS // 2
    if variant in ("mla", "fox"):
        tq, tk, n_sub = 2 * FULL_TILE, FULL_TILE, 1
    else:
        tq, tk, n_sub = TQ, TK, BAND_SUB_BLOCKS
    nq = s // (tq * n_sub)
    window = {"mla": None, "fox": None, "swa": SWA_WINDOW_KEYS, "ck": CK_WINDOW_KEYS}[variant]
    qw = 2 * LANES if variant == "mla" else LANES
    shared_kv = variant == "swa"

    def im(f):
        if sinks is None:
            return f
        return lambda p, bi, i, sink_ref: f(p, bi, i)

    in_specs = [
        pl.BlockSpec((1, tq * n_sub, qw), im(lambda p, bi, i: (bi, i, q_col + p))),
        pl.BlockSpec((1, s, qw), im(lambda p, bi, i: (bi, 0, k_col + (0 if shared_kv else p)))),
        pl.BlockSpec((1, s, LANES), im(lambda p, bi, i: (bi, 0, v_col + (0 if shared_kv else p)))),
    ]
    args = [q, k, v]
    if window is None:
        scratch = [pltpu.VMEM((VT_ROWS, s), BF16), pltpu.VMEM((1, 2 * tq), F32),
                   pltpu.VMEM((VT_ROWS, 2 * tq), F32),
                   pltpu.VMEM((tk, 2 * tq), F32), pltpu.VMEM((tk, 2 * tq), F32)]
    else:
        pad = window - tq
        scratch = [pltpu.VMEM((VT_ROWS, pad + s), BF16), pltpu.VMEM((pad + s, LANES), BF16)]
        in_specs.append(pl.BlockSpec((1,) + extra.shape[1:], im(lambda p, bi, i: (p, 0, 0, 0))))
        args.append(extra)
    if variant == "fox":
        in_specs.append(pl.BlockSpec((1, 1, s, 2), lambda p, bi, i: (bi, p, 0, 0)))
        args.append(extra)
        scratch.append(pltpu.VMEM((s, 2 * LANES), F32))
    out_spec = pl.BlockSpec((1, tq * n_sub, LANES), im(lambda p, bi, i: (bi, i, p)))
    kern = functools.partial(_attn_kernel, variant=variant, window=window, seq=s, tq=tq, tk=tk,
                             n_sub=n_sub)
    params = pltpu.CompilerParams(dimension_semantics=("arbitrary",) * 3, vmem_limit_bytes=VMEM_LIMIT)
    out_shape = jax.ShapeDtypeStruct((b, s, n_pairs * LANES), BF16)
    if sinks is not None:
        grid_spec = pltpu.PrefetchScalarGridSpec(
            num_scalar_prefetch=1, grid=(n_pairs, b, nq), in_specs=in_specs, out_specs=out_spec,
            scratch_shapes=scratch)
        return pl.pallas_call(kern, grid_spec=grid_spec, out_shape=out_shape,
                              compiler_params=params, name="attn_" + variant)(sinks, *args)
    return pl.pallas_call(kern, grid=(n_pairs, b, nq), in_specs=in_specs, out_specs=out_spec,
                          out_shape=out_shape, scratch_shapes=scratch,
                          compiler_params=params, name="attn_" + variant)(*args)


def _lead_rows(window, n_var, v):
    return max(window - TQ - (n_var - 1 - v) * TQ, 0)


def _swa_bias():
    window, left = SWA_WINDOW_KEYS, SWA_LEFT_CHUNKS
    c = jnp.arange(window)[:, None]
    a = jnp.arange(TQ)[None, :]
    dist = a - c + (window - TQ)
    ahead = (a // CHUNK + left) - c // CHUNK
    slopes = jnp.exp2(-8.0 * jnp.arange(1, N_HEADS + 1, dtype=F32) / N_HEADS)
    bias = -slopes[:, None, None] * jnp.abs(dist).astype(F32)[None] * LOG2E
    bias = jnp.where(((ahead >= 0) & (ahead <= left))[None], bias, MASK)
    bias = jnp.concatenate([bias[:4], bias[4:]], axis=2)
    n_var = 2
    return jnp.stack([jnp.where(c[None] >= _lead_rows(window, n_var, v), bias, MASK)
                      for v in range(n_var)], axis=1)


_REL_ROWS = 384


_CK_VARIANTS = 3


def _ck_bias_kernel(tab_ref, out_ref):
    window = CK_WINDOW_KEYS
    span = TQ + window
    tab = tab_ref[0] * LOG2E
    hi = tab.astype(BF16)
    rem = tab - hi.astype(F32)
    mid = rem.astype(BF16)
    lo = (rem - mid.astype(F32)).astype(BF16)
    t = lax.broadcasted_iota(jnp.int32, (_REL_ROWS, span), 0)
    n = lax.broadcasted_iota(jnp.int32, (_REL_ROWS, span), 1)
    delta = jnp.where(n < TQ, n, n - span)
    idx = jnp.clip(delta + (window - TQ), -(CHUNK - 1), REL_MAX) + (CHUNK - 1)
    onehot = jnp.where(t == idx, 1.0, 0.0).astype(BF16)
    base = _dot(hi, onehot) + _dot(mid, onehot) + _dot(lo, onehot)
    c = lax.broadcasted_iota(jnp.int32, (window, TQ), 0)
    a = lax.broadcasted_iota(jnp.int32, (window, TQ), 1)
    valid = ((a >> 6) + CK_LEFT_CHUNKS - (c >> 6)).astype(jnp.uint32) <= CK_LEFT_CHUNKS
    for h in range(2):
        rows = jnp.broadcast_to(base[h:h + 1, :], (window, span))
        tile = jnp.where(valid, pltpu.roll(rows, 0, 1, stride=1, stride_axis=0)[:, :TQ], MASK)
        for v in range(_CK_VARIANTS):
            lead = _lead_rows(window, _CK_VARIANTS, v)
            out_ref[0, v, :, h * TQ:(h + 1) * TQ] = jnp.where(c >= lead, tile, MASK)


def _ck_bias(rel_bias):
    tab = jnp.pad(rel_bias.T, ((0, 0), (0, _REL_ROWS - rel_bias.shape[0])))
    tab = jnp.pad(tab.reshape(N_HEADS // 2, 2, _REL_ROWS), ((0, 0), (0, 6), (0, 0)))
    return pl.pallas_call(
        _ck_bias_kernel,
        grid=(N_HEADS // 2,),
        in_specs=[pl.BlockSpec((1, 8, _REL_ROWS), lambda p: (p, 0, 0))],
        out_specs=pl.BlockSpec((1, _CK_VARIANTS, CK_WINDOW_KEYS, 2 * TQ), lambda p: (p, 0, 0, 0)),
        out_shape=jax.ShapeDtypeStruct((N_HEADS // 2, _CK_VARIANTS, CK_WINDOW_KEYS, 2 * TQ), F32),
        compiler_params=pltpu.CompilerParams(dimension_semantics=("arbitrary",),
                                             vmem_limit_bytes=VMEM_LIMIT),
        name="ck_bias",
    )(tab)


def _out_ffn_kernel(x_ref, oa_ref, ob_ref, woa_ref, wob_ref, g1_ref, b1_ref,
                    wg_ref, wu_ref, wd_ref, g2_ref, b2_ref, y_ref):
    mix = _dot(oa_ref[...], woa_ref[...]) + _dot(ob_ref[...], wob_ref[...])
    x1 = _layer_norm(DEEPNORM_ALPHA * x_ref[...] + mix, g1_ref[...], b1_ref[...])
    x1b = x1.astype(BF16)
    gate = _dot(x1b, wg_ref[...])
    up = _dot(x1b, wu_ref[...])
    hidden = (gate * jax.nn.sigmoid(gate) * up).astype(BF16)
    ffn = _dot(hidden, wd_ref[...])
    y_ref[...] = _layer_norm(DEEPNORM_ALPHA * x1 + ffn, g2_ref[...], b2_ref[...])


def _out_ffn(x2, o_a, o_b, w_out_a, w_out_b, g1, b1, w_gate, w_up, w_down, g2, b2):
    t = x2.shape[0]
    d_ff = w_gate.shape[1]
    row = lambda w: pl.BlockSpec((TM, w), lambda i: (i, 0))
    vec = _const_spec((1, D_MODEL))
    return pl.pallas_call(
        _out_ffn_kernel,
        grid=(t // TM,),
        in_specs=[row(D_MODEL), row(512), row(512), _const_spec((512, D_MODEL)),
                  _const_spec((512, D_MODEL)), vec, vec, _const_spec((D_MODEL, d_ff)),
                  _const_spec((D_MODEL, d_ff)), _const_spec((d_ff, D_MODEL)), vec, vec],
        out_specs=row(D_MODEL),
        out_shape=jax.ShapeDtypeStruct((t, D_MODEL), F32),
        compiler_params=pltpu.CompilerParams(dimension_semantics=("arbitrary",),
                                             vmem_limit_bytes=VMEM_LIMIT),
        name="out_ffn",
    )(x2, o_a, o_b, w_out_a.astype(BF16), w_out_b.astype(BF16), g1.reshape(1, -1), b1.reshape(1, -1),
      w_gate.astype(BF16), w_up.astype(BF16), w_down.astype(BF16), g2.reshape(1, -1), b2.reshape(1, -1))


def kernel(x, ab_w_in, ab_q_norm, ab_w_uq, ab_kv_norm, ab_w_ukv, ab_sinks, ab_w_out,
           cd_w_in, cd_b_forget, cd_rel_bias, cd_w_out,
           ln1_g, ln1_b, ffn_w_gate, ffn_w_up, ffn_w_down, ln2_g, ln2_b):
    b, s, d = x.shape
    assert d == D_MODEL and TQ == TK
    assert s % TM == 0 and s % (2 * FULL_TILE) == 0 and s % (TQ * BAND_SUB_BLOCKS) == 0
    t = b * s
    x2 = x.reshape(t, d)

    qa, ka, va, qs, ks, vs = _proj_ab(x2, ab_w_in[0], ab_q_norm[0], ab_w_uq[0], ab_kv_norm[0],
                                      ab_w_ukv[0], s)
    r3 = lambda a: a.reshape(b, s, a.shape[-1])
    o_a = _attention("mla", r3(qa), r3(ka), r3(va), 0, 0, 0)
    o_b = _attention("swa", r3(qs), r3(ks), r3(vs), 0, 0, 0, extra=_swa_bias(),
                     sinks=ab_sinks[0] * LOG2E)
    w_out = ab_w_out[0]
    w_out_b = w_out[512:].reshape(2, 4, HEAD_DIM, D_MODEL).transpose(1, 0, 2, 3).reshape(512, D_MODEL)
    x2 = _out_ffn(x2, o_a.reshape(t, 512), o_b.reshape(t, 512), w_out[:512], w_out_b,
                  ln1_g[0], ln1_b[0], ffn_w_gate[0], ffn_w_up[0], ffn_w_down[0], ln2_g[0], ln2_b[0])

    qkv, cum = _proj_cd(x2, cd_w_in[0], cd_b_forget[0], s)
    qkv = qkv.reshape(b, s, 3072)
    ck = (cum[:, :N_HEADS] * LOG2E).reshape(b, s, N_HEADS // 2, 2).transpose(0, 2, 1, 3)
    o_c = _attention("fox", qkv, qkv, qkv, 0, 4, 8, extra=ck)
    o_d = _attention("ck", qkv, qkv, qkv, 12, 16, 20, extra=_ck_bias(cd_rel_bias[0]))
    w_out = cd_w_out[0]
    x2 = _out_ffn(x2, o_c.reshape(t, 512), o_d.reshape(t, 512), w_out[:512], w_out[512:],
                  ln1_g[1], ln1_b[1], ffn_w_gate[1], ffn_w_up[1], ffn_w_down[1], ln2_g[1], ln2_b[1])
    return x2.reshape(b, s, d)
```

```python
import functools
import math

import numpy as np
import jax
import jax.numpy as jnp
from jax import lax
from jax.experimental import pallas as pl
from jax.experimental.pallas import tpu as pltpu

D_MODEL = 1024
CHUNK = 64
HEAD_DIM = 64
N_HEADS = 8
LANES = 128
LN_EPS = 1e-5
RMS_EPS = 1e-6
MASK = -1e30
LOG2E = math.log2(math.e)

MLA_Q_RANK = 384
MLA_KV_RANK = 256
MLA_NOPE = 64
MLA_ROPE = 32
ROPE_THETA = 10000.0
SWA_KV_HEADS = 2
SWA_LEFT_CHUNKS = 2
CK_LEFT_CHUNKS = 8
REL_MAX = 256
DEPTH = 2
DEEPNORM_ALPHA = (2 * DEPTH) ** 0.25

TQ = 256
TK = 256
BAND_SUB_BLOCKS = 8
SWA_WINDOW_KEYS = SWA_LEFT_CHUNKS * CHUNK + TQ
CK_WINDOW_KEYS = CK_LEFT_CHUNKS * CHUNK + TQ
FULL_TILE = 512
TM = 512
VT_ROWS = LANES + 16
VMEM_LIMIT = 56 * 1024 * 1024

BF16 = jnp.bfloat16
F32 = jnp.float32


def _dot(a, b):
    return jnp.dot(a, b, preferred_element_type=F32)


def _dot_nt(a, b):
    return lax.dot_general(a, b, (((1,), (1,)), ((), ())), preferred_element_type=F32)


def _layer_norm(y, g, b):
    mu = jnp.mean(y, axis=-1, keepdims=True)
    yc = y - mu
    var = jnp.mean(yc * yc, axis=-1, keepdims=True)
    return yc * lax.rsqrt(var + LN_EPS) * g + b


def _rms_norm(c, g):
    return c * lax.rsqrt(jnp.mean(c * c, axis=-1, keepdims=True) + RMS_EPS) * g


def _const_spec(shape):
    zeros = (0,) * len(shape)
    return pl.BlockSpec(shape, lambda *_: zeros, pipeline_mode=pl.Buffered(1))


_AB_CQ, _AB_CKV, _AB_QS, _AB_KS, _AB_VS, _AB_KR, _AB_COLS = 0, 384, 640, 1152, 1280, 1408, 1536


def _rope_mix(t, m1, m2):
    return t * m1 + pltpu.roll(t, 96, 1) * m2


def _proj_ab_kernel(x_ref, win_ref, qn_ref, wuq_ref, kvn_ref, wukv_ref,
                    m1q_ref, m2q_ref, m1k_ref, m2k_ref,
                    qa_ref, ka_ref, va_ref, qs_ref, ks_ref, vs_ref):
    proj = _dot(x_ref[...].astype(BF16), win_ref[...])
    c_q = _rms_norm(proj[:, _AB_CQ:_AB_CKV], qn_ref[...])
    qf = _dot(c_q.astype(BF16), wuq_ref[...])
    c_kv = _rms_norm(proj[:, _AB_CKV:_AB_QS], kvn_ref[...])
    kvf = _dot(c_kv.astype(BF16), wukv_ref[...])
    k_rope = _rope_mix(proj[:, _AB_KR:_AB_COLS], m1k_ref[...], m2k_ref[...])
    m1q, m2q = m1q_ref[...], m2q_ref[...]
    for h in range(N_HEADS):
        sl = slice(h * LANES, (h + 1) * LANES)
        qa_ref[:, sl] = _rope_mix(qf[:, sl], m1q, m2q).astype(BF16)
        ka_ref[:, sl] = (kvf[:, sl] + k_rope).astype(BF16)
    va_ref[...] = kvf[:, N_HEADS * LANES:].astype(BF16)
    qs_ref[...] = (proj[:, _AB_QS:_AB_KS] * (HEAD_DIM ** -0.5 * LOG2E)).astype(BF16)
    ks_ref[...] = proj[:, _AB_KS:_AB_VS].astype(BF16)
    vs_ref[...] = proj[:, _AB_VS:_AB_KR].astype(BF16)


def _proj_ab(x2, w_in, q_norm, w_uq, kv_norm, w_ukv, seq):
    t = x2.shape[0]
    nsb = seq // TM
    c_q, c_kv, k_r, q_s, k_s, v_s = jnp.split(w_in, [384, 640, 672, 1184, 1312], axis=1)
    q_s = q_s.reshape(D_MODEL, 2, 4, HEAD_DIM).transpose(0, 2, 1, 3).reshape(D_MODEL, 512)
    kr_blk = jnp.concatenate([jnp.zeros((D_MODEL, 64), F32), k_r, k_r[:, 16:], k_r[:, :16]], axis=1)
    win_p = jnp.concatenate([c_q, c_kv, q_s, k_s, v_s, kr_blk], axis=1).astype(BF16)
    wq = w_uq.reshape(MLA_Q_RANK, N_HEADS, MLA_NOPE + MLA_ROPE)
    wuq_p = jnp.concatenate([wq, wq[..., 80:96], wq[..., 64:80]], axis=-1)
    wuq_p = wuq_p.reshape(MLA_Q_RANK, N_HEADS * LANES).astype(BF16)
    wkv = w_ukv.reshape(MLA_KV_RANK, N_HEADS, 2 * HEAD_DIM)
    k_pad = jnp.concatenate([wkv[..., :64], jnp.zeros_like(wkv[..., :64])], axis=-1)
    wukv_p = jnp.concatenate([k_pad.reshape(MLA_KV_RANK, N_HEADS * LANES),
                              wkv[..., 64:].reshape(MLA_KV_RANK, 512)], axis=1).astype(BF16)
    inv = ROPE_THETA ** (-jnp.arange(0, MLA_ROPE, 2, dtype=F32) / MLA_ROPE)
    ang = jnp.arange(seq, dtype=F32)[:, None] * inv[None, :]
    cos, sin = jnp.cos(ang), jnp.sin(ang)
    z64, z32, o64 = jnp.zeros((seq, 64), F32), jnp.zeros((seq, 32), F32), jnp.ones((seq, 64), F32)
    m1k = jnp.concatenate([z64, cos, cos, z32], axis=1)
    m2k = jnp.concatenate([z64, -sin, sin, z32], axis=1)
    q_scale = (MLA_NOPE + MLA_ROPE) ** -0.5 * LOG2E
    m1q = jnp.concatenate([o64, cos, cos, z32], axis=1) * q_scale
    m2q = m2k * q_scale

    row = lambda w: pl.BlockSpec((TM, w), lambda i: (i, 0))
    tab = pl.BlockSpec((TM, LANES), lambda i: (i % nsb, 0))
    out_w = (1024, 1024, 512, 512, 128, 128)
    return pl.pallas_call(
        _proj_ab_kernel,
        grid=(t // TM,),
        in_specs=[row(D_MODEL), _const_spec((D_MODEL, _AB_COLS)), _const_spec((1, MLA_Q_RANK)),
                  _const_spec((MLA_Q_RANK, 1024)), _const_spec((1, MLA_KV_RANK)),
                  _const_spec((MLA_KV_RANK, 1536)), tab, tab, tab, tab],
        out_specs=[row(w) for w in out_w],
        out_shape=[jax.ShapeDtypeStruct((t, w), BF16) for w in out_w],
        compiler_params=pltpu.CompilerParams(dimension_semantics=("arbitrary",),
                                             vmem_limit_bytes=VMEM_LIMIT),
        name="proj_ab",
    )(x2, win_p, q_norm.reshape(1, -1), wuq_p, kv_norm.reshape(1, -1), wukv_p, m1q, m2q, m1k, m2k)


def _proj_cd_kernel(x_ref, win_ref, bf_ref, qkv_ref, cum_ref, carry_ref, *, nsb):
    @pl.when(pl.program_id(0) % nsb == 0)
    def _():
        carry_ref[...] = jnp.zeros_like(carry_ref)

    proj = _dot(x_ref[...].astype(BF16), win_ref[...])
    q_scale = HEAD_DIM ** -0.5 * LOG2E
    for blk, scale in enumerate((q_scale, None, None, q_scale, None, None)):
        part = proj[:, blk * 512:(blk + 1) * 512]
        qkv_ref[:, blk * 512:(blk + 1) * 512] = (part if scale is None else part * scale).astype(BF16)
    z = proj[:, 3072:] + bf_ref[...]
    log_f = jnp.minimum(z, 0.0) - jnp.log(1.0 + jnp.exp(-jnp.abs(z)))
    r = lax.broadcasted_iota(jnp.int32, (TM, TM), 0)
    c = lax.broadcasted_iota(jnp.int32, (TM, TM), 1)
    tri = jnp.where(c <= r, 1.0, 0.0).astype(BF16)
    hi = log_f.astype(BF16)
    rem = log_f - hi.astype(F32)
    mid = rem.astype(BF16)
    lo = (rem - mid.astype(F32)).astype(BF16)
    cum = _dot(tri, hi) + _dot(tri, mid) + _dot(tri, lo) + carry_ref[...]
    cum_ref[...] = cum
    carry_ref[...] = cum[TM - 1:TM, :]


def _proj_cd(x2, w_in, b_forget, seq):
    t = x2.shape[0]
    q_f, k_f, v_f, f_l, q_c, k_c, v_c = jnp.split(w_in, [512, 1024, 1536, 1544, 2056, 2568], axis=1)
    f_pad = jnp.concatenate([f_l, jnp.zeros((D_MODEL, LANES - N_HEADS), F32)], axis=1)
    win_p = jnp.concatenate([q_f, k_f, v_f, q_c, k_c, v_c, f_pad], axis=1).astype(BF16)
    bf = jnp.concatenate([b_forget, jnp.zeros((LANES - N_HEADS,), F32)]).reshape(1, LANES)
    return pl.pallas_call(
        functools.partial(_proj_cd_kernel, nsb=seq // TM),
        grid=(t // TM,),
        in_specs=[pl.BlockSpec((TM, D_MODEL), lambda i: (i, 0)),
                  _const_spec((D_MODEL, 3072 + LANES)), _const_spec((1, LANES))],
        out_specs=[pl.BlockSpec((TM, 3072), lambda i: (i, 0)),
                   pl.BlockSpec((TM, LANES), lambda i: (i, 0))],
        out_shape=[jax.ShapeDtypeStruct((t, 3072), BF16), jax.ShapeDtypeStruct((t, LANES), F32)],
        scratch_shapes=[pltpu.VMEM((1, LANES), F32)],
        compiler_params=pltpu.CompilerParams(dimension_semantics=("arbitrary",),
                                             vmem_limit_bytes=VMEM_LIMIT),
        name="proj_cd",
    )(x2, win_p, bf)


def _attn_kernel(*refs, variant, window, seq, tq, tk, n_sub):
    if variant == "mla":
        q_ref, k_ref, v_ref, qn_ref, o_ref, vt_ref, m_ref, acc_ref, sa_ref, sb_ref = refs
    elif variant == "fox":
        (q_ref, k_ref, v_ref, qn_ref, ck_ref, o_ref, vt_ref, m_ref, acc_ref, sa_ref, sb_ref,
         ckrep_ref) = refs
    elif variant == "swa":
        sink_ref, q_ref, k_ref, v_ref, bias_ref, o_ref, vt_ref, kpad_ref = refs
    else:
        q_ref, k_ref, v_ref, bias_ref, o_ref, vt_ref, kpad_ref = refs

    i = pl.program_id(2)
    banded = window is not None
    pad = window - tq if banded else 0

    @pl.when(i == 0)
    def _():
        for c in range(seq // 512):
            sl = slice(pad + c * 512, pad + (c + 1) * 512)
            vt_ref[0:LANES, sl] = v_ref[0, c * 512:(c + 1) * 512, :].astype(F32).T.astype(BF16)
        vt_ref[LANES:VT_ROWS, :] = jnp.ones((VT_ROWS - LANES, pad + seq), BF16)
        if banded:
            vt_ref[0:LANES, 0:pad] = jnp.zeros((LANES, pad), BF16)
            kpad_ref[0:pad, :] = jnp.zeros((pad, LANES), BF16)
            kpad_ref[pad:pad + seq, :] = k_ref[0]
        if variant == "fox":
            src = lax.broadcasted_iota(jnp.int32, (LANES, 2 * LANES), 0)
            dst = lax.broadcasted_iota(jnp.int32, (LANES, 2 * LANES), 1)
            pick = jnp.where(src == 2 * pl.program_id(0) + (dst >> 7), 1.0, 0.0).astype(BF16)
            for c in range(seq // 512):
                rows = slice(c * 512, (c + 1) * 512)
                cum = ck_ref[0, rows, :]
                hi = cum.astype(BF16)
                rem = cum - hi.astype(F32)
                mid = rem.astype(BF16)
                lo = (rem - mid.astype(F32)).astype(BF16)
                ckrep_ref[rows, :] = (_dot(hi, pick) + _dot(mid, pick) + _dot(lo, pick)) * LOG2E

    sub = lax.broadcasted_iota(jnp.int32, (LANES, tq), 0)
    lane = lax.broadcasted_iota(jnp.int32, (tq, LANES), 1)

    def pair_rows(q_blk):
        zero = jnp.zeros_like(q_blk)
        return jnp.concatenate([jnp.where(lane < HEAD_DIM, q_blk, zero),
                                jnp.where(lane >= HEAD_DIM, q_blk, zero)], axis=0)

    def finish(acc, m, rows):
        l = acc[LANES:LANES + 1, :]
        o_t = acc[:LANES, :]
        if variant == "swa":
            pair = pl.program_id(0)
            col = lax.broadcasted_iota(jnp.int32, (1, 2 * tq), 1)
            sink = jnp.where(col < tq, sink_ref[pair], sink_ref[pair + N_HEADS // 2])
            m_fin = jnp.maximum(m, sink)
            scale = jnp.exp2(m - m_fin)
            l = l * scale + jnp.exp2(sink - m_fin)
            o_t = o_t * scale
        o_t = o_t / l
        o_ref[0, rows, :] = jnp.where(sub < HEAD_DIM, o_t[:, :tq], o_t[:, tq:]).T.astype(o_ref.dtype)

    if banded:
        n_var = bias_ref.shape[1]
        for sb in range(n_sub):
            ib = i * n_sub + sb
            start = pl.multiple_of(ib * tq, tq)
            rows = slice(sb * tq, (sb + 1) * tq)
            v = jnp.maximum(n_var - 1 - ib, 0)
            s = _dot_nt(kpad_ref[pl.ds(start, window), :], pair_rows(q_ref[0, rows, :])) + bias_ref[0, v]
            m = jnp.max(s, axis=0, keepdims=True)
            p = jnp.exp2(s - m).astype(BF16)
            finish(_dot(vt_ref[:, pl.ds(start, window)], p), m, rows)
        return

    if variant == "mla":
        query_operands = lambda ref: (ref[0, :, :LANES], ref[0, :, LANES:])
    else:
        query_operands = lambda ref: pair_rows(ref[0])

    m_ref[...] = jnp.full_like(m_ref, MASK)
    acc_ref[...] = jnp.zeros_like(acc_ref)

    first_half = [(h * tq, tk) for h in range(2)]
    second_half = [(h * tq + tk, tk) for h in range(2)]

    q_now = query_operands(q_ref)

    def scores(j, s_ref, slabs=((0, 2 * tq),), queries=q_now):
        start = pl.multiple_of(j * tk, tk)
        k = k_ref[0, pl.ds(start, tk), :]
        for c0, w in slabs:
            cols = slice(c0, c0 + w)
            if variant == "mla":
                for h in range(2):
                    lo, hi = max(c0, h * tq), min(c0 + w, (h + 1) * tq)
                    if lo < hi:
                        q_h = queries[h][lo - h * tq:hi - h * tq, :]
                        s_ref[:, lo:hi] = _dot_nt(k[:, h * LANES:(h + 1) * LANES], q_h)
                continue
            s = _dot_nt(k, queries[cols, :])
            if variant == "fox":
                ck_reps = [ckrep_ref[pl.ds(start, tk), (c // tq) * LANES:(c // tq + 1) * LANES]
                           for c in range(c0, c0 + w, LANES)]
                s = s - jnp.concatenate(ck_reps, axis=1)
            s_ref[:, cols] = s

    def consume(j, s_ref, slabs, masked=()):
        start = pl.multiple_of(j * tk, tk)
        v_t = vt_ref[:, pl.ds(start, tk)]
        for c0, w in slabs:
            cols = slice(c0, c0 + w)
            s = s_ref[:, cols]
            if (c0, w) in masked:
                key = lax.broadcasted_iota(jnp.int32, (tk, w), 0)
                qry = lax.broadcasted_iota(jnp.int32, (tk, w), 1)
                if variant == "mla":
                    keep = (key >> 6) <= (qry >> 6)
                else:
                    keep = key <= qry
                s = jnp.where(keep, s, MASK)
            m_prev = m_ref[:, cols]
            m_new = jnp.maximum(m_prev, jnp.max(s, axis=0, keepdims=True))
            alpha = jnp.exp2(m_prev - m_new)
            p = jnp.exp2(s - m_new).astype(BF16)
            acc_ref[:, cols] = alpha * acc_ref[:, cols] + _dot(v_t, p)
            m_ref[:, cols] = m_new

    everything = [(0, 2 * tq)]

    @pl.when(i == 0)
    def _():
        scores(0, sa_ref)

    def two_blocks(u, carry):
        t = 2 * u
        scores(t + 1, sb_ref)
        consume(t, sa_ref, everything)
        scores(t + 2, sa_ref)
        consume(t + 1, sb_ref, everything)
        return carry

    lax.fori_loop(0, i, two_blocks, None)

    def diagonal_blocks(prefetch_next):
        q_next = query_operands(qn_ref) if prefetch_next else None
        scores(2 * i + 1, sb_ref, second_half)
        consume(2 * i, sa_ref, first_half, masked=first_half)
        if prefetch_next:
            scores(0, sa_ref, first_half, queries=q_next)
        consume(2 * i, sa_ref, second_half)
        if prefetch_next:
            scores(0, sa_ref, second_half, queries=q_next)
        consume(2 * i + 1, sb_ref, second_half, masked=second_half)

    last = pl.num_programs(2) - 1
    pl.when(i < last)(lambda: diagonal_blocks(True))
    pl.when(i == last)(lambda: diagonal_blocks(False))
    finish(acc_ref[...], m_ref[...], slice(0, tq))


def _attention(variant, q, k, v, q_col, k_col, v_col, *, extra=None, sinks=None):
    b, s, _ = q.shape
    n_pairs = N_HEADS // 2
    if variant in ("mla", "fox"):
        tq, tk, n_sub = 2 * FULL_TILE, FULL_TILE, 1
    else:
        tq, tk, n_sub = TQ, TK, BAND_SUB_BLOCKS
    nq = s // (tq * n_sub)
    window = {"mla": None, "fox": None, "swa": SWA_WINDOW_KEYS, "ck": CK_WINDOW_KEYS}[variant]
    qw = 2 * LANES if variant == "mla" else LANES
    shared_kv = variant == "swa"

    def im(f):
        if sinks is None:
            return f
        return lambda p, bi, i, sink_ref: f(p, bi, i)

    in_specs = [
        pl.BlockSpec((1, tq * n_sub, qw), im(lambda p, bi, i: (bi, i, q_col + p))),
        pl.BlockSpec((1, s, qw), im(lambda p, bi, i: (bi, 0, k_col + (0 if shared_kv else p)))),
        pl.BlockSpec((1, s, LANES), im(lambda p, bi, i: (bi, 0, v_col + (0 if shared_kv else p)))),
    ]
    args = [q, k, v]
    if window is None:
        in_specs.append(pl.BlockSpec((1, tq, qw), lambda p, bi, i: (bi, jnp.minimum(i + 1, nq - 1),
                                                                     q_col + p)))
        args.append(q)
        scratch = [pltpu.VMEM((VT_ROWS, s), BF16), pltpu.VMEM((1, 2 * tq), F32),
                   pltpu.VMEM((VT_ROWS, 2 * tq), F32),
                   pltpu.VMEM((tk, 2 * tq), F32), pltpu.VMEM((tk, 2 * tq), F32)]
    else:
        pad = window - tq
        scratch = [pltpu.VMEM((VT_ROWS, pad + s), BF16), pltpu.VMEM((pad + s, LANES), BF16)]
        in_specs.append(pl.BlockSpec((1,) + extra.shape[1:], im(lambda p, bi, i: (p, 0, 0, 0))))
        args.append(extra)
    if variant == "fox":
        in_specs.append(pl.BlockSpec((1, s, LANES), lambda p, bi, i: (bi, 0, 0)))
        args.append(extra)
        scratch.append(pltpu.VMEM((s, 2 * LANES), F32))
    out_spec = pl.BlockSpec((1, tq * n_sub, LANES), im(lambda p, bi, i: (bi, i, p)))
    kern = functools.partial(_attn_kernel, variant=variant, window=window, seq=s, tq=tq, tk=tk,
                             n_sub=n_sub)
    params = pltpu.CompilerParams(dimension_semantics=("arbitrary",) * 3, vmem_limit_bytes=VMEM_LIMIT)
    out_shape = jax.ShapeDtypeStruct((b, s, n_pairs * LANES), BF16)
    if sinks is not None:
        grid_spec = pltpu.PrefetchScalarGridSpec(
            num_scalar_prefetch=1, grid=(n_pairs, b, nq), in_specs=in_specs, out_specs=out_spec,
            scratch_shapes=scratch)
        return pl.pallas_call(kern, grid_spec=grid_spec, out_shape=out_shape,
                              compiler_params=params, name="attn_" + variant)(sinks, *args)
    return pl.pallas_call(kern, grid=(n_pairs, b, nq), in_specs=in_specs, out_specs=out_spec,
                          out_shape=out_shape, scratch_shapes=scratch,
                          compiler_params=params, name="attn_" + variant)(*args)


def _lead_rows(window, n_var, v):
    return max(window - TQ - (n_var - 1 - v) * TQ, 0)


def _swa_bias():
    window, left = SWA_WINDOW_KEYS, SWA_LEFT_CHUNKS
    c = jnp.arange(window)[:, None]
    a = jnp.arange(TQ)[None, :]
    dist = a - c + (window - TQ)
    ahead = (a // CHUNK + left) - c // CHUNK
    slopes = jnp.exp2(-8.0 * jnp.arange(1, N_HEADS + 1, dtype=F32) / N_HEADS)
    bias = -slopes[:, None, None] * jnp.abs(dist).astype(F32)[None] * LOG2E
    bias = jnp.where(((ahead >= 0) & (ahead <= left))[None], bias, MASK)
    bias = jnp.concatenate([bias[:4], bias[4:]], axis=2)
    n_var = 2
    return jnp.stack([jnp.where(c[None] >= _lead_rows(window, n_var, v), bias, MASK)
                      for v in range(n_var)], axis=1)


_REL_ROWS = 384


_CK_VARIANTS = 3


def _ck_bias_kernel(tab_ref, out_ref):
    window = CK_WINDOW_KEYS
    span = TQ + window
    tab = tab_ref[0] * LOG2E
    hi = tab.astype(BF16)
    rem = tab - hi.astype(F32)
    mid = rem.astype(BF16)
    lo = (rem - mid.astype(F32)).astype(BF16)
    t = lax.broadcasted_iota(jnp.int32, (_REL_ROWS, span), 0)
    n = lax.broadcasted_iota(jnp.int32, (_REL_ROWS, span), 1)
    delta = jnp.where(n < TQ, n, n - span)
    idx = jnp.clip(delta + (window - TQ), -(CHUNK - 1), REL_MAX) + (CHUNK - 1)
    onehot = jnp.where(t == idx, 1.0, 0.0).astype(BF16)
    base = _dot(hi, onehot) + _dot(mid, onehot) + _dot(lo, onehot)
    c = lax.broadcasted_iota(jnp.int32, (window, TQ), 0)
    a = lax.broadcasted_iota(jnp.int32, (window, TQ), 1)
    valid = ((a >> 6) + CK_LEFT_CHUNKS - (c >> 6)).astype(jnp.uint32) <= CK_LEFT_CHUNKS
    for h in range(2):
        rows = jnp.broadcast_to(base[h:h + 1, :], (window, span))
        tile = jnp.where(valid, pltpu.roll(rows, 0, 1, stride=1, stride_axis=0)[:, :TQ], MASK)
        for v in range(_CK_VARIANTS):
            lead = _lead_rows(window, _CK_VARIANTS, v)
            out_ref[0, v, :, h * TQ:(h + 1) * TQ] = jnp.where(c >= lead, tile, MASK)


def _ck_bias(rel_bias):
    tab = jnp.pad(rel_bias.T, ((0, 0), (0, _REL_ROWS - rel_bias.shape[0])))
    tab = jnp.pad(tab.reshape(N_HEADS // 2, 2, _REL_ROWS), ((0, 0), (0, 6), (0, 0)))
    return pl.pallas_call(
        _ck_bias_kernel,
        grid=(N_HEADS // 2,),
        in_specs=[pl.BlockSpec((1, 8, _REL_ROWS), lambda p: (p, 0, 0))],
        out_specs=pl.BlockSpec((1, _CK_VARIANTS, CK_WINDOW_KEYS, 2 * TQ), lambda p: (p, 0, 0, 0)),
        out_shape=jax.ShapeDtypeStruct((N_HEADS // 2, _CK_VARIANTS, CK_WINDOW_KEYS, 2 * TQ), F32),
        compiler_params=pltpu.CompilerParams(dimension_semantics=("arbitrary",),
                                             vmem_limit_bytes=VMEM_LIMIT),
        name="ck_bias",
    )(tab)


def _out_ffn_kernel(x_ref, oa_ref, ob_ref, woa_ref, wob_ref, g1_ref, b1_ref,
                    wg_ref, wu_ref, wd_ref, g2_ref, b2_ref, y_ref):
    mix = _dot(oa_ref[...], woa_ref[...]) + _dot(ob_ref[...], wob_ref[...])
    x1 = _layer_norm(DEEPNORM_ALPHA * x_ref[...] + mix, g1_ref[...], b1_ref[...])
    x1b = x1.astype(BF16)
    gate = _dot(x1b, wg_ref[...])
    up = _dot(x1b, wu_ref[...])
    hidden = (gate * jax.nn.sigmoid(gate) * up).astype(BF16)
    ffn = _dot(hidden, wd_ref[...])
    y_ref[...] = _layer_norm(DEEPNORM_ALPHA * x1 + ffn, g2_ref[...], b2_ref[...])


def _out_ffn(x2, o_a, o_b, w_out_a, w_out_b, g1, b1, w_gate, w_up, w_down, g2, b2):
    t = x2.shape[0]
    d_ff = w_gate.shape[1]
    row = lambda w: pl.BlockSpec((TM, w), lambda i: (i, 0))
    vec = _const_spec((1, D_MODEL))
    return pl.pallas_call(
        _out_ffn_kernel,
        grid=(t // TM,),
        in_specs=[row(D_MODEL), row(512), row(512), _const_spec((512, D_MODEL)),
                  _const_spec((512, D_MODEL)), vec, vec, _const_spec((D_MODEL, d_ff)),
                  _const_spec((D_MODEL, d_ff)), _const_spec((d_ff, D_MODEL)), vec, vec],
        out_specs=row(D_MODEL),
        out_shape=jax.ShapeDtypeStruct((t, D_MODEL), F32),
        compiler_params=pltpu.CompilerParams(dimension_semantics=("arbitrary",),
                                             vmem_limit_bytes=VMEM_LIMIT),
        name="out_ffn",
    )(x2, o_a, o_b, w_out_a.astype(BF16), w_out_b.astype(BF16), g1.reshape(1, -1), b1.reshape(1, -1),
      w_gate.astype(BF16), w_up.astype(BF16), w_down.astype(BF16), g2.reshape(1, -1), b2.reshape(1, -1))


def kernel(x, ab_w_in, ab_q_norm, ab_w_uq, ab_kv_norm, ab_w_ukv, ab_sinks, ab_w_out,
           cd_w_in, cd_b_forget, cd_rel_bias, cd_w_out,
           ln1_g, ln1_b, ffn_w_gate, ffn_w_up, ffn_w_down, ln2_g, ln2_b):
    b, s, d = x.shape
    assert d == D_MODEL and TQ == TK
    assert s % TM == 0 and s % (2 * FULL_TILE) == 0 and s % (TQ * BAND_SUB_BLOCKS) == 0
    t = b * s
    x2 = x.reshape(t, d)

    qa, ka, va, qs, ks, vs = _proj_ab(x2, ab_w_in[0], ab_q_norm[0], ab_w_uq[0], ab_kv_norm[0],
                                      ab_w_ukv[0], s)
    r3 = lambda a: a.reshape(b, s, a.shape[-1])
    o_a = _attention("mla", r3(qa), r3(ka), r3(va), 0, 0, 0)
    o_b = _attention("swa", r3(qs), r3(ks), r3(vs), 0, 0, 0, extra=_swa_bias(),
                     sinks=ab_sinks[0] * LOG2E)
    w_out = ab_w_out[0]
    w_out_b = w_out[512:].reshape(2, 4, HEAD_DIM, D_MODEL).transpose(1, 0, 2, 3).reshape(512, D_MODEL)
    x2 = _out_ffn(x2, o_a.reshape(t, 512), o_b.reshape(t, 512), w_out[:512], w_out_b,
                  ln1_g[0], ln1_b[0], ffn_w_gate[0], ffn_w_up[0], ffn_w_down[0], ln2_g[0], ln2_b[0])

    qkv, cum = _proj_cd(x2, cd_w_in[0], cd_b_forget[0], s)
    qkv = qkv.reshape(b, s, 3072)
    o_c = _attention("fox", qkv, qkv, qkv, 0, 4, 8, extra=cum.reshape(b, s, LANES))
    o_d = _attention("ck", qkv, qkv, qkv, 12, 16, 20, extra=_ck_bias(cd_rel_bias[0]))
    w_out = cd_w_out[0]
    x2 = _out_ffn(x2, o_c.reshape(t, 512), o_d.reshape(t, 512), w_out[:512], w_out[512:],
                  ln1_g[1], ln1_b[1], ffn_w_gate[1], ffn_w_up[1], ffn_w_down[1], ln2_g[1], ln2_b[1])
    return x2.reshape(b, s, d)
```

```python
import functools
import math

import numpy as np
import jax
import jax.numpy as jnp
from jax import lax
from jax.experimental import pallas as pl
from jax.experimental.pallas import tpu as pltpu

D_MODEL = 1024
CHUNK = 64
HEAD_DIM = 64
N_HEADS = 8
LANES = 128
LN_EPS = 1e-5
RMS_EPS = 1e-6
MASK = -1e30
LOG2E = math.log2(math.e)

MLA_Q_RANK = 384
MLA_KV_RANK = 256
MLA_NOPE = 64
MLA_ROPE = 32
ROPE_THETA = 10000.0
SWA_KV_HEADS = 2
SWA_LEFT_CHUNKS = 2
CK_LEFT_CHUNKS = 8
REL_MAX = 256
DEPTH = 2
DEEPNORM_ALPHA = (2 * DEPTH) ** 0.25

TQ = 256
TK = 256
BAND_SUB_BLOCKS = 16
SWA_WINDOW_KEYS = SWA_LEFT_CHUNKS * CHUNK + TQ
CK_WINDOW_KEYS = CK_LEFT_CHUNKS * CHUNK + TQ
FULL_TILE = 512
TM = 512
VT_ROWS = LANES + 16
VMEM_LIMIT = 56 * 1024 * 1024

BF16 = jnp.bfloat16
F32 = jnp.float32


def _dot(a, b):
    return jnp.dot(a, b, preferred_element_type=F32)


def _dot_nt(a, b):
    return lax.dot_general(a, b, (((1,), (1,)), ((), ())), preferred_element_type=F32)


def _layer_norm(y, g, b):
    mu = jnp.mean(y, axis=-1, keepdims=True)
    yc = y - mu
    var = jnp.mean(yc * yc, axis=-1, keepdims=True)
    return yc * lax.rsqrt(var + LN_EPS) * g + b


def _rms_norm(c, g):
    return c * lax.rsqrt(jnp.mean(c * c, axis=-1, keepdims=True) + RMS_EPS) * g


def _const_spec(shape):
    zeros = (0,) * len(shape)
    return pl.BlockSpec(shape, lambda *_: zeros, pipeline_mode=pl.Buffered(1))


_AB_CQ, _AB_CKV, _AB_QS, _AB_KS, _AB_VS, _AB_KR, _AB_COLS = 0, 384, 640, 1152, 1280, 1408, 1536


def _rope_mix(t, m1, m2):
    return t * m1 + pltpu.roll(t, 96, 1) * m2


def _proj_ab_kernel(x_ref, win_ref, qn_ref, wuq_ref, kvn_ref, wukv_ref,
                    m1q_ref, m2q_ref, m1k_ref, m2k_ref,
                    qa_ref, ka_ref, va_ref, qs_ref, ks_ref, vs_ref):
    proj = _dot(x_ref[...].astype(BF16), win_ref[...])
    c_q = _rms_norm(proj[:, _AB_CQ:_AB_CKV], qn_ref[...])
    qf = _dot(c_q.astype(BF16), wuq_ref[...])
    c_kv = _rms_norm(proj[:, _AB_CKV:_AB_QS], kvn_ref[...])
    kvf = _dot(c_kv.astype(BF16), wukv_ref[...])
    k_rope = _rope_mix(proj[:, _AB_KR:_AB_COLS], m1k_ref[...], m2k_ref[...])
    m1q, m2q = m1q_ref[...], m2q_ref[...]
    for h in range(N_HEADS):
        sl = slice(h * LANES, (h + 1) * LANES)
        qa_ref[:, sl] = _rope_mix(qf[:, sl], m1q, m2q).astype(BF16)
        ka_ref[:, sl] = (kvf[:, sl] + k_rope).astype(BF16)
    va_ref[...] = kvf[:, N_HEADS * LANES:].astype(BF16)
    qs_ref[...] = (proj[:, _AB_QS:_AB_KS] * (HEAD_DIM ** -0.5 * LOG2E)).astype(BF16)
    ks_ref[...] = proj[:, _AB_KS:_AB_VS].astype(BF16)
    vs_ref[...] = proj[:, _AB_VS:_AB_KR].astype(BF16)


def _proj_ab(x2, w_in, q_norm, w_uq, kv_norm, w_ukv, seq):
    t = x2.shape[0]
    nsb = seq // TM
    c_q, c_kv, k_r, q_s, k_s, v_s = jnp.split(w_in, [384, 640, 672, 1184, 1312], axis=1)
    q_s = q_s.reshape(D_MODEL, 2, 4, HEAD_DIM).transpose(0, 2, 1, 3).reshape(D_MODEL, 512)
    kr_blk = jnp.concatenate([jnp.zeros((D_MODEL, 64), F32), k_r, k_r[:, 16:], k_r[:, :16]], axis=1)
    win_p = jnp.concatenate([c_q, c_kv, q_s, k_s, v_s, kr_blk], axis=1).astype(BF16)
    wq = w_uq.reshape(MLA_Q_RANK, N_HEADS, MLA_NOPE + MLA_ROPE)
    wuq_p = jnp.concatenate([wq, wq[..., 80:96], wq[..., 64:80]], axis=-1)
    wuq_p = wuq_p.reshape(MLA_Q_RANK, N_HEADS * LANES).astype(BF16)
    wkv = w_ukv.reshape(MLA_KV_RANK, N_HEADS, 2 * HEAD_DIM)
    k_pad = jnp.concatenate([wkv[..., :64], jnp.zeros_like(wkv[..., :64])], axis=-1)
    wukv_p = jnp.concatenate([k_pad.reshape(MLA_KV_RANK, N_HEADS * LANES),
                              wkv[..., 64:].reshape(MLA_KV_RANK, 512)], axis=1).astype(BF16)
    inv = ROPE_THETA ** (-jnp.arange(0, MLA_ROPE, 2, dtype=F32) / MLA_ROPE)
    ang = jnp.arange(seq, dtype=F32)[:, None] * inv[None, :]
    cos, sin = jnp.cos(ang), jnp.sin(ang)
    z64, z32, o64 = jnp.zeros((seq, 64), F32), jnp.zeros((seq, 32), F32), jnp.ones((seq, 64), F32)
    m1k = jnp.concatenate([z64, cos, cos, z32], axis=1)
    m2k = jnp.concatenate([z64, -sin, sin, z32], axis=1)
    q_scale = (MLA_NOPE + MLA_ROPE) ** -0.5 * LOG2E
    m1q = jnp.concatenate([o64, cos, cos, z32], axis=1) * q_scale
    m2q = m2k * q_scale

    row = lambda w: pl.BlockSpec((TM, w), lambda i: (i, 0))
    tab = pl.BlockSpec((TM, LANES), lambda i: (i % nsb, 0))
    out_w = (1024, 1024, 512, 512, 128, 128)
    return pl.pallas_call(
        _proj_ab_kernel,
        grid=(t // TM,),
        in_specs=[row(D_MODEL), _const_spec((D_MODEL, _AB_COLS)), _const_spec((1, MLA_Q_RANK)),
                  _const_spec((MLA_Q_RANK, 1024)), _const_spec((1, MLA_KV_RANK)),
                  _const_spec((MLA_KV_RANK, 1536)), tab, tab, tab, tab],
        out_specs=[row(w) for w in out_w],
        out_shape=[jax.ShapeDtypeStruct((t, w), BF16) for w in out_w],
        compiler_params=pltpu.CompilerParams(dimension_semantics=("arbitrary",),
                                             vmem_limit_bytes=VMEM_LIMIT),
        name="proj_ab",
    )(x2, win_p, q_norm.reshape(1, -1), wuq_p, kv_norm.reshape(1, -1), wukv_p, m1q, m2q, m1k, m2k)


def _proj_cd_kernel(x_ref, win_ref, bf_ref, qkv_ref, cum_ref, carry_ref, *, nsb):
    @pl.when(pl.program_id(0) % nsb == 0)
    def _():
        carry_ref[...] = jnp.zeros_like(carry_ref)

    proj = _dot(x_ref[...].astype(BF16), win_ref[...])
    q_scale = HEAD_DIM ** -0.5 * LOG2E
    for blk, scale in enumerate((q_scale, None, None, q_scale, None, None)):
        part = proj[:, blk * 512:(blk + 1) * 512]
        qkv_ref[:, blk * 512:(blk + 1) * 512] = (part if scale is None else part * scale).astype(BF16)
    z = proj[:, 3072:] + bf_ref[...]
    log_f = jnp.minimum(z, 0.0) - jnp.log(1.0 + jnp.exp(-jnp.abs(z)))
    r = lax.broadcasted_iota(jnp.int32, (TM, TM), 0)
    c = lax.broadcasted_iota(jnp.int32, (TM, TM), 1)
    tri = jnp.where(c <= r, 1.0, 0.0).astype(BF16)
    hi = log_f.astype(BF16)
    rem = log_f - hi.astype(F32)
    mid = rem.astype(BF16)
    lo = (rem - mid.astype(F32)).astype(BF16)
    cum = _dot(tri, hi) + _dot(tri, mid) + _dot(tri, lo) + carry_ref[...]
    cum_ref[...] = cum
    carry_ref[...] = cum[TM - 1:TM, :]


def _proj_cd(x2, w_in, b_forget, seq):
    t = x2.shape[0]
    q_f, k_f, v_f, f_l, q_c, k_c, v_c = jnp.split(w_in, [512, 1024, 1536, 1544, 2056, 2568], axis=1)
    f_pad = jnp.concatenate([f_l, jnp.zeros((D_MODEL, LANES - N_HEADS), F32)], axis=1)
    win_p = jnp.concatenate([q_f, k_f, v_f, q_c, k_c, v_c, f_pad], axis=1).astype(BF16)
    bf = jnp.concatenate([b_forget, jnp.zeros((LANES - N_HEADS,), F32)]).reshape(1, LANES)
    return pl.pallas_call(
        functools.partial(_proj_cd_kernel, nsb=seq // TM),
        grid=(t // TM,),
        in_specs=[pl.BlockSpec((TM, D_MODEL), lambda i: (i, 0)),
                  _const_spec((D_MODEL, 3072 + LANES)), _const_spec((1, LANES))],
        out_specs=[pl.BlockSpec((TM, 3072), lambda i: (i, 0)),
                   pl.BlockSpec((TM, LANES), lambda i: (i, 0))],
        out_shape=[jax.ShapeDtypeStruct((t, 3072), BF16), jax.ShapeDtypeStruct((t, LANES), F32)],
        scratch_shapes=[pltpu.VMEM((1, LANES), F32)],
        compiler_params=pltpu.CompilerParams(dimension_semantics=("arbitrary",),
                                             vmem_limit_bytes=VMEM_LIMIT),
        name="proj_cd",
    )(x2, win_p, bf)


def _attn_kernel(*refs, variant, window, seq, tq, tk, n_sub):
    if variant == "mla":
        q_ref, k_ref, v_ref, qn_ref, o_ref, vt_ref, m_ref, acc_ref, sa_ref, sb_ref = refs
    elif variant == "fox":
        (q_ref, k_ref, v_ref, qn_ref, ck_ref, o_ref, vt_ref, m_ref, acc_ref, sa_ref, sb_ref,
         ckrep_ref) = refs
    elif variant == "swa":
        sink_ref, q_ref, k_ref, v_ref, bias_ref, o_ref, vt_ref, kpad_ref, sa_ref, sb_ref = refs
    else:
        q_ref, k_ref, v_ref, bias_ref, o_ref, vt_ref, kpad_ref, sa_ref, sb_ref = refs

    i = pl.program_id(2)
    banded = window is not None
    pad = window - tq if banded else 0

    @pl.when(i == 0)
    def _():
        for c in range(seq // 512):
            sl = slice(pad + c * 512, pad + (c + 1) * 512)
            vt_ref[0:LANES, sl] = v_ref[0, c * 512:(c + 1) * 512, :].astype(F32).T.astype(BF16)
        vt_ref[LANES:VT_ROWS, :] = jnp.ones((VT_ROWS - LANES, pad + seq), BF16)
        if banded:
            vt_ref[0:LANES, 0:pad] = jnp.zeros((LANES, pad), BF16)
            kpad_ref[0:pad, :] = jnp.zeros((pad, LANES), BF16)
            kpad_ref[pad:pad + seq, :] = k_ref[0]
        if variant == "fox":
            src = lax.broadcasted_iota(jnp.int32, (LANES, 2 * LANES), 0)
            dst = lax.broadcasted_iota(jnp.int32, (LANES, 2 * LANES), 1)
            pick = jnp.where(src == 2 * pl.program_id(0) + (dst >> 7), 1.0, 0.0).astype(BF16)
            for c in range(seq // 512):
                rows = slice(c * 512, (c + 1) * 512)
                cum = ck_ref[0, rows, :]
                hi = cum.astype(BF16)
                rem = cum - hi.astype(F32)
                mid = rem.astype(BF16)
                lo = (rem - mid.astype(F32)).astype(BF16)
                ckrep_ref[rows, :] = (_dot(hi, pick) + _dot(mid, pick) + _dot(lo, pick)) * LOG2E

    sub = lax.broadcasted_iota(jnp.int32, (LANES, tq), 0)
    lane = lax.broadcasted_iota(jnp.int32, (tq, LANES), 1)

    def pair_rows(q_blk):
        zero = jnp.zeros_like(q_blk)
        return jnp.concatenate([jnp.where(lane < HEAD_DIM, q_blk, zero),
                                jnp.where(lane >= HEAD_DIM, q_blk, zero)], axis=0)

    def finish(acc, m, rows):
        l = acc[LANES:LANES + 1, :]
        o_t = acc[:LANES, :]
        if variant == "swa":
            pair = pl.program_id(0)
            col = lax.broadcasted_iota(jnp.int32, (1, 2 * tq), 1)
            sink = jnp.where(col < tq, sink_ref[pair], sink_ref[pair + N_HEADS // 2])
            m_fin = jnp.maximum(m, sink)
            scale = jnp.exp2(m - m_fin)
            l = l * scale + jnp.exp2(sink - m_fin)
            o_t = o_t * scale
        o_t = o_t / l
        o_ref[0, rows, :] = jnp.where(sub < HEAD_DIM, o_t[:, :tq], o_t[:, tq:]).T.astype(o_ref.dtype)

    if banded:
        n_var = bias_ref.shape[1]

        def band_scores(sb, s_ref):
            ib = i * n_sub + sb
            start = pl.multiple_of(ib * tq, tq)
            rows = pl.ds(pl.multiple_of(sb * tq, tq), tq)
            v = jnp.maximum(n_var - 1 - ib, 0)
            s_ref[...] = (_dot_nt(kpad_ref[pl.ds(start, window), :], pair_rows(q_ref[0, rows, :]))
                          + bias_ref[0, v])

        def band_softmax(sb, s_ref):
            start = pl.multiple_of((i * n_sub + sb) * tq, tq)
            s = s_ref[...]
            m = jnp.max(s, axis=0, keepdims=True)
            p = jnp.exp2(s - m).astype(BF16)
            finish(_dot(vt_ref[:, pl.ds(start, window)], p), m,
                   pl.ds(pl.multiple_of(sb * tq, tq), tq))

        band_scores(0, sa_ref)

        def two_sub_blocks(u, carry):
            sb = 2 * u
            band_scores(sb + 1, sb_ref)
            band_softmax(sb, sa_ref)
            band_scores(sb + 2, sa_ref)
            band_softmax(sb + 1, sb_ref)
            return carry

        lax.fori_loop(0, n_sub // 2 - 1, two_sub_blocks, None)
        band_scores(n_sub - 1, sb_ref)
        band_softmax(n_sub - 2, sa_ref)
        band_softmax(n_sub - 1, sb_ref)
        return

    if variant == "mla":
        query_operands = lambda ref: (ref[0, :, :LANES], ref[0, :, LANES:])
    else:
        query_operands = lambda ref: pair_rows(ref[0])

    m_ref[...] = jnp.full_like(m_ref, MASK)
    acc_ref[...] = jnp.zeros_like(acc_ref)

    first_half = [(h * tq, tk) for h in range(2)]
    second_half = [(h * tq + tk, tk) for h in range(2)]

    q_now = query_operands(q_ref)

    def scores(j, s_ref, slabs=((0, 2 * tq),), queries=q_now):
        start = pl.multiple_of(j * tk, tk)
        k = k_ref[0, pl.ds(start, tk), :]
        for c0, w in slabs:
            cols = slice(c0, c0 + w)
            if variant == "mla":
                for h in range(2):
                    lo, hi = max(c0, h * tq), min(c0 + w, (h + 1) * tq)
                    if lo < hi:
                        q_h = queries[h][lo - h * tq:hi - h * tq, :]
                        s_ref[:, lo:hi] = _dot_nt(k[:, h * LANES:(h + 1) * LANES], q_h)
                continue
            s = _dot_nt(k, queries[cols, :])
            if variant == "fox":
                ck_reps = [ckrep_ref[pl.ds(start, tk), (c // tq) * LANES:(c // tq + 1) * LANES]
                           for c in range(c0, c0 + w, LANES)]
                s = s - jnp.concatenate(ck_reps, axis=1)
            s_ref[:, cols] = s

    def consume(j, s_ref, slabs, masked=()):
        start = pl.multiple_of(j * tk, tk)
        v_t = vt_ref[:, pl.ds(start, tk)]
        for c0, w in slabs:
            cols = slice(c0, c0 + w)
            s = s_ref[:, cols]
            if (c0, w) in masked:
                key = lax.broadcasted_iota(jnp.int32, (tk, w), 0)
                qry = lax.broadcasted_iota(jnp.int32, (tk, w), 1)
                if variant == "mla":
                    keep = (key >> 6) <= (qry >> 6)
                else:
                    keep = key <= qry
                s = jnp.where(keep, s, MASK)
            m_prev = m_ref[:, cols]
            m_new = jnp.maximum(m_prev, jnp.max(s, axis=0, keepdims=True))
            alpha = jnp.exp2(m_prev - m_new)
            p = jnp.exp2(s - m_new).astype(BF16)
            acc_ref[:, cols] = alpha * acc_ref[:, cols] + _dot(v_t, p)
            m_ref[:, cols] = m_new

    everything = [(0, 2 * tq)]

    @pl.when(i == 0)
    def _():
        scores(0, sa_ref)

    def two_blocks(u, carry):
        t = 2 * u
        scores(t + 1, sb_ref)
        consume(t, sa_ref, everything)
        scores(t + 2, sa_ref)
        consume(t + 1, sb_ref, everything)
        return carry

    lax.fori_loop(0, i, two_blocks, None)

    def diagonal_blocks(prefetch_next):
        q_next = query_operands(qn_ref) if prefetch_next else None
        scores(2 * i + 1, sb_ref, second_half)
        consume(2 * i, sa_ref, first_half, masked=first_half)
        if prefetch_next:
            scores(0, sa_ref, first_half, queries=q_next)
        consume(2 * i, sa_ref, second_half)
        if prefetch_next:
            scores(0, sa_ref, second_half, queries=q_next)
        consume(2 * i + 1, sb_ref, second_half, masked=second_half)

    last = pl.num_programs(2) - 1
    pl.when(i < last)(lambda: diagonal_blocks(True))
    pl.when(i == last)(lambda: diagonal_blocks(False))
    finish(acc_ref[...], m_ref[...], slice(0, tq))


def _attention(variant, q, k, v, q_col, k_col, v_col, *, extra=None, sinks=None):
    b, s, _ = q.shape
    n_pairs = N_HEADS // 2
    if variant in ("mla", "fox"):
        tq, tk, n_sub = 2 * FULL_TILE, FULL_TILE, 1
    else:
        tq, tk, n_sub = TQ, TK, BAND_SUB_BLOCKS
    nq = s // (tq * n_sub)
    window = {"mla": None, "fox": None, "swa": SWA_WINDOW_KEYS, "ck": CK_WINDOW_KEYS}[variant]
    qw = 2 * LANES if variant == "mla" else LANES
    shared_kv = variant == "swa"

    def im(f):
        if sinks is None:
            return f
        return lambda p, bi, i, sink_ref: f(p, bi, i)

    in_specs = [
        pl.BlockSpec((1, tq * n_sub, qw), im(lambda p, bi, i: (bi, i, q_col + p))),
        pl.BlockSpec((1, s, qw), im(lambda p, bi, i: (bi, 0, k_col + (0 if shared_kv else p)))),
        pl.BlockSpec((1, s, LANES), im(lambda p, bi, i: (bi, 0, v_col + (0 if shared_kv else p)))),
    ]
    args = [q, k, v]
    if window is None:
        in_specs.append(pl.BlockSpec((1, tq, qw), lambda p, bi, i: (bi, jnp.minimum(i + 1, nq - 1),
                                                                     q_col + p)))
        args.append(q)
        scratch = [pltpu.VMEM((VT_ROWS, s), BF16), pltpu.VMEM((1, 2 * tq), F32),
                   pltpu.VMEM((VT_ROWS, 2 * tq), F32),
                   pltpu.VMEM((tk, 2 * tq), F32), pltpu.VMEM((tk, 2 * tq), F32)]
    else:
        pad = window - tq
        scratch = [pltpu.VMEM((VT_ROWS, pad + s), BF16), pltpu.VMEM((pad + s, LANES), BF16),
                   pltpu.VMEM((window, 2 * tq), F32), pltpu.VMEM((window, 2 * tq), F32)]
        in_specs.append(pl.BlockSpec((1,) + extra.shape[1:], im(lambda p, bi, i: (p, 0, 0, 0))))
        args.append(extra)
    if variant == "fox":
        in_specs.append(pl.BlockSpec((1, s, LANES), lambda p, bi, i: (bi, 0, 0)))
        args.append(extra)
        scratch.append(pltpu.VMEM((s, 2 * LANES), F32))
    out_spec = pl.BlockSpec((1, tq * n_sub, LANES), im(lambda p, bi, i: (bi, i, p)))
    kern = functools.partial(_attn_kernel, variant=variant, window=window, seq=s, tq=tq, tk=tk,
                             n_sub=n_sub)
    params = pltpu.CompilerParams(dimension_semantics=("arbitrary",) * 3, vmem_limit_bytes=VMEM_LIMIT)
    out_shape = jax.ShapeDtypeStruct((b, s, n_pairs * LANES), BF16)
    if sinks is not None:
        grid_spec = pltpu.PrefetchScalarGridSpec(
            num_scalar_prefetch=1, grid=(n_pairs, b, nq), in_specs=in_specs, out_specs=out_spec,
            scratch_shapes=scratch)
        return pl.pallas_call(kern, grid_spec=grid_spec, out_shape=out_shape,
                              compiler_params=params, name="attn_" + variant)(sinks, *args)
    return pl.pallas_call(kern, grid=(n_pairs, b, nq), in_specs=in_specs, out_specs=out_spec,
                          out_shape=out_shape, scratch_shapes=scratch,
                          compiler_params=params, name="attn_" + variant)(*args)


def _lead_rows(window, n_var, v):
    return max(window - TQ - (n_var - 1 - v) * TQ, 0)


def _swa_bias():
    window, left = SWA_WINDOW_KEYS, SWA_LEFT_CHUNKS
    c = jnp.arange(window)[:, None]
    a = jnp.arange(TQ)[None, :]
    dist = a - c + (window - TQ)
    ahead = (a // CHUNK + left) - c // CHUNK
    slopes = jnp.exp2(-8.0 * jnp.arange(1, N_HEADS + 1, dtype=F32) / N_HEADS)
    bias = -slopes[:, None, None] * jnp.abs(dist).astype(F32)[None] * LOG2E
    bias = jnp.where(((ahead >= 0) & (ahead <= left))[None], bias, MASK)
    bias = jnp.concatenate([bias[:4], bias[4:]], axis=2)
    n_var = 2
    return jnp.stack([jnp.where(c[None] >= _lead_rows(window, n_var, v), bias, MASK)
                      for v in range(n_var)], axis=1)


_REL_ROWS = 384


_CK_VARIANTS = 3


def _ck_bias_kernel(tab_ref, out_ref):
    window = CK_WINDOW_KEYS
    span = TQ + window
    tab = tab_ref[0] * LOG2E
    hi = tab.astype(BF16)
    rem = tab - hi.astype(F32)
    mid = rem.astype(BF16)
    lo = (rem - mid.astype(F32)).astype(BF16)
    t = lax.broadcasted_iota(jnp.int32, (_REL_ROWS, span), 0)
    n = lax.broadcasted_iota(jnp.int32, (_REL_ROWS, span), 1)
    delta = jnp.where(n < TQ, n, n - span)
    idx = jnp.clip(delta + (window - TQ), -(CHUNK - 1), REL_MAX) + (CHUNK - 1)
    onehot = jnp.where(t == idx, 1.0, 0.0).astype(BF16)
    base = _dot(hi, onehot) + _dot(mid, onehot) + _dot(lo, onehot)
    c = lax.broadcasted_iota(jnp.int32, (window, TQ), 0)
    a = lax.broadcasted_iota(jnp.int32, (window, TQ), 1)
    valid = ((a >> 6) + CK_LEFT_CHUNKS - (c >> 6)).astype(jnp.uint32) <= CK_LEFT_CHUNKS
    for h in range(2):
        rows = jnp.broadcast_to(base[h:h + 1, :], (window, span))
        tile = jnp.where(valid, pltpu.roll(rows, 0, 1, stride=1, stride_axis=0)[:, :TQ], MASK)
        for v in range(_CK_VARIANTS):
            lead = _lead_rows(window, _CK_VARIANTS, v)
            out_ref[0, v, :, h * TQ:(h + 1) * TQ] = jnp.where(c >= lead, tile, MASK)


def _ck_bias(rel_bias):
    tab = jnp.pad(rel_bias.T, ((0, 0), (0, _REL_ROWS - rel_bias.shape[0])))
    tab = jnp.pad(tab.reshape(N_HEADS // 2, 2, _REL_ROWS), ((0, 0), (0, 6), (0, 0)))
    return pl.pallas_call(
        _ck_bias_kernel,
        grid=(N_HEADS // 2,),
        in_specs=[pl.BlockSpec((1, 8, _REL_ROWS), lambda p: (p, 0, 0))],
        out_specs=pl.BlockSpec((1, _CK_VARIANTS, CK_WINDOW_KEYS, 2 * TQ), lambda p: (p, 0, 0, 0)),
        out_shape=jax.ShapeDtypeStruct((N_HEADS // 2, _CK_VARIANTS, CK_WINDOW_KEYS, 2 * TQ), F32),
        compiler_params=pltpu.CompilerParams(dimension_semantics=("arbitrary",),
                                             vmem_limit_bytes=VMEM_LIMIT),
        name="ck_bias",
    )(tab)


def _out_ffn_kernel(x_ref, oa_ref, ob_ref, woa_ref, wob_ref, g1_ref, b1_ref,
                    wg_ref, wu_ref, wd_ref, g2_ref, b2_ref, y_ref):
    mix = _dot(oa_ref[...], woa_ref[...]) + _dot(ob_ref[...], wob_ref[...])
    x1 = _layer_norm(DEEPNORM_ALPHA * x_ref[...] + mix, g1_ref[...], b1_ref[...])
    x1b = x1.astype(BF16)
    gate = _dot(x1b, wg_ref[...])
    up = _dot(x1b, wu_ref[...])
    hidden = (gate * jax.nn.sigmoid(gate) * up).astype(BF16)
    ffn = _dot(hidden, wd_ref[...])
    y_ref[...] = _layer_norm(DEEPNORM_ALPHA * x1 + ffn, g2_ref[...], b2_ref[...])


def _out_ffn(x2, o_a, o_b, w_out_a, w_out_b, g1, b1, w_gate, w_up, w_down, g2, b2):
    t = x2.shape[0]
    d_ff = w_gate.shape[1]
    row = lambda w: pl.BlockSpec((TM, w), lambda i: (i, 0))
    vec = _const_spec((1, D_MODEL))
    return pl.pallas_call(
        _out_ffn_kernel,
        grid=(t // TM,),
        in_specs=[row(D_MODEL), row(512), row(512), _const_spec((512, D_MODEL)),
                  _const_spec((512, D_MODEL)), vec, vec, _const_spec((D_MODEL, d_ff)),
                  _const_spec((D_MODEL, d_ff)), _const_spec((d_ff, D_MODEL)), vec, vec],
        out_specs=row(D_MODEL),
        out_shape=jax.ShapeDtypeStruct((t, D_MODEL), F32),
        compiler_params=pltpu.CompilerParams(dimension_semantics=("arbitrary",),
                                             vmem_limit_bytes=VMEM_LIMIT),
        name="out_ffn",
    )(x2, o_a, o_b, w_out_a.astype(BF16), w_out_b.astype(BF16), g1.reshape(1, -1), b1.reshape(1, -1),
      w_gate.astype(BF16), w_up.astype(BF16), w_down.astype(BF16), g2.reshape(1, -1), b2.reshape(1, -1))


def kernel(x, ab_w_in, ab_q_norm, ab_w_uq, ab_kv_norm, ab_w_ukv, ab_sinks, ab_w_out,
           cd_w_in, cd_b_forget, cd_rel_bias, cd_w_out,
           ln1_g, ln1_b, ffn_w_gate, ffn_w_up, ffn_w_down, ln2_g, ln2_b):
    b, s, d = x.shape
    assert d == D_MODEL and TQ == TK
    assert s % TM == 0 and s % (2 * FULL_TILE) == 0 and s % (TQ * BAND_SUB_BLOCKS) == 0
    t = b * s
    x2 = x.reshape(t, d)

    qa, ka, va, qs, ks, vs = _proj_ab(x2, ab_w_in[0], ab_q_norm[0], ab_w_uq[0], ab_kv_norm[0],
                                      ab_w_ukv[0], s)
    r3 = lambda a: a.reshape(b, s, a.shape[-1])
    o_a = _attention("mla", r3(qa), r3(ka), r3(va), 0, 0, 0)
    o_b = _attention("swa", r3(qs), r3(ks), r3(vs), 0, 0, 0, extra=_swa_bias(),
                     sinks=ab_sinks[0] * LOG2E)
    w_out = ab_w_out[0]
    w_out_b = w_out[512:].reshape(2, 4, HEAD_DIM, D_MODEL).transpose(1, 0, 2, 3).reshape(512, D_MODEL)
    x2 = _out_ffn(x2, o_a.reshape(t, 512), o_b.reshape(t, 512), w_out[:512], w_out_b,
                  ln1_g[0], ln1_b[0], ffn_w_gate[0], ffn_w_up[0], ffn_w_down[0], ln2_g[0], ln2_b[0])

    qkv, cum = _proj_cd(x2, cd_w_in[0], cd_b_forget[0], s)
    qkv = qkv.reshape(b, s, 3072)
    o_c = _attention("fox", qkv, qkv, qkv, 0, 4, 8, extra=cum.reshape(b, s, LANES))
    o_d = _attention("ck", qkv, qkv, qkv, 12, 16, 20, extra=_ck_bias(cd_rel_bias[0]))
    w_out = cd_w_out[0]
    x2 = _out_ffn(x2, o_c.reshape(t, 512), o_d.reshape(t, 512), w_out[:512], w_out[512:],
                  ln1_g[1], ln1_b[1], ffn_w_gate[1], ffn_w_up[1], ffn_w_down[1], ln2_g[1], ln2_b[1])
    return x2.reshape(b, s, d)
```

```python
import functools
import math

import numpy as np
import jax
import jax.numpy as jnp
from jax import lax
from jax.experimental import pallas as pl
from jax.experimental.pallas import tpu as pltpu

D_MODEL = 1024
CHUNK = 64
HEAD_DIM = 64
N_HEADS = 8
LANES = 128
LN_EPS = 1e-5
RMS_EPS = 1e-6
MASK = -1e30
LOG2E = math.log2(math.e)

MLA_Q_RANK = 384
MLA_KV_RANK = 256
MLA_NOPE = 64
MLA_ROPE = 32
ROPE_THETA = 10000.0
SWA_KV_HEADS = 2
SWA_LEFT_CHUNKS = 2
CK_LEFT_CHUNKS = 8
REL_MAX = 256
DEPTH = 2
DEEPNORM_ALPHA = (2 * DEPTH) ** 0.25

TQ = 256
TK = 256
BAND_SUB_BLOCKS = 16
BAND_GROUP = 2
SWA_WINDOW_KEYS = SWA_LEFT_CHUNKS * CHUNK + TQ
CK_WINDOW_KEYS = CK_LEFT_CHUNKS * CHUNK + TQ
FULL_TILE = 512
TM = 512
CUMSUM_ROWS = 128
VT_ROWS = LANES + 16
VMEM_LIMIT = 56 * 1024 * 1024

BF16 = jnp.bfloat16
F32 = jnp.float32


def _dot(a, b):
    return jnp.dot(a, b, preferred_element_type=F32)


def _dot_nt(a, b):
    return lax.dot_general(a, b, (((1,), (1,)), ((), ())), preferred_element_type=F32)


def _layer_norm(y, g, b):
    mu = jnp.mean(y, axis=-1, keepdims=True)
    yc = y - mu
    var = jnp.mean(yc * yc, axis=-1, keepdims=True)
    return yc * lax.rsqrt(var + LN_EPS) * g + b


def _rms_norm(c, g):
    return c * lax.rsqrt(jnp.mean(c * c, axis=-1, keepdims=True) + RMS_EPS) * g


def _const_spec(shape):
    zeros = (0,) * len(shape)
    return pl.BlockSpec(shape, lambda *_: zeros, pipeline_mode=pl.Buffered(1))


_AB_CQ, _AB_CKV, _AB_QS, _AB_KS, _AB_VS, _AB_KR, _AB_COLS = 0, 384, 640, 1152, 1280, 1408, 1536


def _rope_mix(t, m1, m2):
    return t * m1 + pltpu.roll(t, 96, 1) * m2


def _proj_ab_kernel(x_ref, win_ref, qn_ref, wuq_ref, kvn_ref, wukv_ref,
                    m1q_ref, m2q_ref, m1k_ref, m2k_ref,
                    qa_ref, ka_ref, va_ref, qs_ref, ks_ref, vs_ref):
    proj = _dot(x_ref[...].astype(BF16), win_ref[...])
    c_q = _rms_norm(proj[:, _AB_CQ:_AB_CKV], qn_ref[...])
    qf = _dot(c_q.astype(BF16), wuq_ref[...])
    c_kv = _rms_norm(proj[:, _AB_CKV:_AB_QS], kvn_ref[...])
    kvf = _dot(c_kv.astype(BF16), wukv_ref[...])
    k_rope = _rope_mix(proj[:, _AB_KR:_AB_COLS], m1k_ref[...], m2k_ref[...])
    m1q, m2q = m1q_ref[...], m2q_ref[...]
    for h in range(N_HEADS):
        sl = slice(h * LANES, (h + 1) * LANES)
        qa_ref[:, sl] = _rope_mix(qf[:, sl], m1q, m2q).astype(BF16)
        ka_ref[:, sl] = (kvf[:, sl] + k_rope).astype(BF16)
    va_ref[...] = kvf[:, N_HEADS * LANES:].astype(BF16)
    qs_ref[...] = (proj[:, _AB_QS:_AB_KS] * (HEAD_DIM ** -0.5 * LOG2E)).astype(BF16)
    ks_ref[...] = proj[:, _AB_KS:_AB_VS].astype(BF16)
    vs_ref[...] = proj[:, _AB_VS:_AB_KR].astype(BF16)


def _proj_ab(x2, w_in, q_norm, w_uq, kv_norm, w_ukv, seq):
    t = x2.shape[0]
    nsb = seq // TM
    c_q, c_kv, k_r, q_s, k_s, v_s = jnp.split(w_in, [384, 640, 672, 1184, 1312], axis=1)
    q_s = q_s.reshape(D_MODEL, 2, 4, HEAD_DIM).transpose(0, 2, 1, 3).reshape(D_MODEL, 512)
    kr_blk = jnp.concatenate([jnp.zeros((D_MODEL, 64), F32), k_r, k_r[:, 16:], k_r[:, :16]], axis=1)
    win_p = jnp.concatenate([c_q, c_kv, q_s, k_s, v_s, kr_blk], axis=1).astype(BF16)
    wq = w_uq.reshape(MLA_Q_RANK, N_HEADS, MLA_NOPE + MLA_ROPE)
    wuq_p = jnp.concatenate([wq, wq[..., 80:96], wq[..., 64:80]], axis=-1)
    wuq_p = wuq_p.reshape(MLA_Q_RANK, N_HEADS * LANES).astype(BF16)
    wkv = w_ukv.reshape(MLA_KV_RANK, N_HEADS, 2 * HEAD_DIM)
    k_pad = jnp.concatenate([wkv[..., :64], jnp.zeros_like(wkv[..., :64])], axis=-1)
    wukv_p = jnp.concatenate([k_pad.reshape(MLA_KV_RANK, N_HEADS * LANES),
                              wkv[..., 64:].reshape(MLA_KV_RANK, 512)], axis=1).astype(BF16)
    inv = ROPE_THETA ** (-jnp.arange(0, MLA_ROPE, 2, dtype=F32) / MLA_ROPE)
    ang = jnp.arange(seq, dtype=F32)[:, None] * inv[None, :]
    cos, sin = jnp.cos(ang), jnp.sin(ang)
    z64, z32, o64 = jnp.zeros((seq, 64), F32), jnp.zeros((seq, 32), F32), jnp.ones((seq, 64), F32)
    m1k = jnp.concatenate([z64, cos, cos, z32], axis=1)
    m2k = jnp.concatenate([z64, -sin, sin, z32], axis=1)
    q_scale = (MLA_NOPE + MLA_ROPE) ** -0.5 * LOG2E
    m1q = jnp.concatenate([o64, cos, cos, z32], axis=1) * q_scale
    m2q = m2k * q_scale

    row = lambda w: pl.BlockSpec((TM, w), lambda i: (i, 0))
    tab = pl.BlockSpec((TM, LANES), lambda i: (i % nsb, 0))
    out_w = (1024, 1024, 512, 512, 128, 128)
    return pl.pallas_call(
        _proj_ab_kernel,
        grid=(t // TM,),
        in_specs=[row(D_MODEL), _const_spec((D_MODEL, _AB_COLS)), _const_spec((1, MLA_Q_RANK)),
                  _const_spec((MLA_Q_RANK, 1024)), _const_spec((1, MLA_KV_RANK)),
                  _const_spec((MLA_KV_RANK, 1536)), tab, tab, tab, tab],
        out_specs=[row(w) for w in out_w],
        out_shape=[jax.ShapeDtypeStruct((t, w), BF16) for w in out_w],
        compiler_params=pltpu.CompilerParams(dimension_semantics=("arbitrary",),
                                             vmem_limit_bytes=VMEM_LIMIT),
        name="proj_ab",
    )(x2, win_p, q_norm.reshape(1, -1), wuq_p, kv_norm.reshape(1, -1), wukv_p, m1q, m2q, m1k, m2k)


def _proj_cd_kernel(x_ref, win_ref, bf_ref, qkv_ref, cum_ref, carry_ref, *, nsb):
    @pl.when(pl.program_id(0) % nsb == 0)
    def _():
        carry_ref[...] = jnp.zeros_like(carry_ref)

    proj = _dot(x_ref[...].astype(BF16), win_ref[...])
    q_scale = HEAD_DIM ** -0.5 * LOG2E
    for blk, scale in enumerate((q_scale, None, None, q_scale, None, None)):
        part = proj[:, blk * 512:(blk + 1) * 512]
        qkv_ref[:, blk * 512:(blk + 1) * 512] = (part if scale is None else part * scale).astype(BF16)
    z = proj[:, 3072:] + bf_ref[...]
    log_f = jnp.minimum(z, 0.0) - jnp.log(1.0 + jnp.exp(-jnp.abs(z)))
    r = lax.broadcasted_iota(jnp.int32, (CUMSUM_ROWS, CUMSUM_ROWS), 0)
    c = lax.broadcasted_iota(jnp.int32, (CUMSUM_ROWS, CUMSUM_ROWS), 1)
    tri = jnp.where(c <= r, 1.0, 0.0).astype(BF16)
    hi = log_f.astype(BF16)
    rem = log_f - hi.astype(F32)
    mid = rem.astype(BF16)
    lo = (rem - mid.astype(F32)).astype(BF16)
    carry = carry_ref[...]
    for blk in range(TM // CUMSUM_ROWS):
        rows = slice(blk * CUMSUM_ROWS, (blk + 1) * CUMSUM_ROWS)
        cum = _dot(tri, hi[rows]) + _dot(tri, mid[rows]) + _dot(tri, lo[rows]) + carry
        cum_ref[rows, :] = cum
        carry = cum[CUMSUM_ROWS - 1:CUMSUM_ROWS, :]
    carry_ref[...] = carry


def _proj_cd(x2, w_in, b_forget, seq):
    t = x2.shape[0]
    q_f, k_f, v_f, f_l, q_c, k_c, v_c = jnp.split(w_in, [512, 1024, 1536, 1544, 2056, 2568], axis=1)
    f_pad = jnp.concatenate([f_l, jnp.zeros((D_MODEL, LANES - N_HEADS), F32)], axis=1)
    win_p = jnp.concatenate([q_f, k_f, v_f, q_c, k_c, v_c, f_pad], axis=1).astype(BF16)
    bf = jnp.concatenate([b_forget, jnp.zeros((LANES - N_HEADS,), F32)]).reshape(1, LANES)
    return pl.pallas_call(
        functools.partial(_proj_cd_kernel, nsb=seq // TM),
        grid=(t // TM,),
        in_specs=[pl.BlockSpec((TM, D_MODEL), lambda i: (i, 0)),
                  _const_spec((D_MODEL, 3072 + LANES)), _const_spec((1, LANES))],
        out_specs=[pl.BlockSpec((TM, 3072), lambda i: (i, 0)),
                   pl.BlockSpec((TM, LANES), lambda i: (i, 0))],
        out_shape=[jax.ShapeDtypeStruct((t, 3072), BF16), jax.ShapeDtypeStruct((t, LANES), F32)],
        scratch_shapes=[pltpu.VMEM((1, LANES), F32)],
        compiler_params=pltpu.CompilerParams(dimension_semantics=("arbitrary",),
                                             vmem_limit_bytes=VMEM_LIMIT),
        name="proj_cd",
    )(x2, win_p, bf)


def _attn_kernel(*refs, variant, window, seq, tq, tk, n_sub):
    if variant == "mla":
        q_ref, k_ref, v_ref, qn_ref, o_ref, vt_ref, m_ref, acc_ref, sa_ref, sb_ref = refs
    elif variant == "fox":
        (q_ref, k_ref, v_ref, qn_ref, ck_ref, o_ref, vt_ref, m_ref, acc_ref, sa_ref, sb_ref,
         ckrep_ref) = refs
    elif variant == "swa":
        sink_ref, q_ref, k_ref, v_ref, bias_ref, o_ref, vt_ref, kpad_ref, sbuf_ref = refs
    else:
        q_ref, k_ref, v_ref, bias_ref, o_ref, vt_ref, kpad_ref, sbuf_ref = refs

    i = pl.program_id(2)
    banded = window is not None
    pad = window - tq if banded else 0

    @pl.when(i == 0)
    def _():
        for c in range(seq // 512):
            sl = slice(pad + c * 512, pad + (c + 1) * 512)
            vt_ref[0:LANES, sl] = v_ref[0, c * 512:(c + 1) * 512, :].astype(F32).T.astype(BF16)
        vt_ref[LANES:VT_ROWS, :] = jnp.ones((VT_ROWS - LANES, pad + seq), BF16)
        if banded:
            vt_ref[0:LANES, 0:pad] = jnp.zeros((LANES, pad), BF16)
            kpad_ref[0:pad, :] = jnp.zeros((pad, LANES), BF16)
            kpad_ref[pad:pad + seq, :] = k_ref[0]
        if variant == "fox":
            src = lax.broadcasted_iota(jnp.int32, (LANES, 2 * LANES), 0)
            dst = lax.broadcasted_iota(jnp.int32, (LANES, 2 * LANES), 1)
            pick = jnp.where(src == 2 * pl.program_id(0) + (dst >> 7), 1.0, 0.0).astype(BF16)
            for c in range(seq // 512):
                rows = slice(c * 512, (c + 1) * 512)
                cum = ck_ref[0, rows, :]
                hi = cum.astype(BF16)
                rem = cum - hi.astype(F32)
                mid = rem.astype(BF16)
                lo = (rem - mid.astype(F32)).astype(BF16)
                ckrep_ref[rows, :] = (_dot(hi, pick) + _dot(mid, pick) + _dot(lo, pick)) * LOG2E

    sub = lax.broadcasted_iota(jnp.int32, (LANES, tq), 0)
    lane = lax.broadcasted_iota(jnp.int32, (tq, LANES), 1)

    def pair_rows(q_blk):
        zero = jnp.zeros_like(q_blk)
        return jnp.concatenate([jnp.where(lane < HEAD_DIM, q_blk, zero),
                                jnp.where(lane >= HEAD_DIM, q_blk, zero)], axis=0)

    def finish(acc, m, rows):
        l = acc[LANES:LANES + 1, :]
        o_t = acc[:LANES, :]
        if variant == "swa":
            pair = pl.program_id(0)
            col = lax.broadcasted_iota(jnp.int32, (1, 2 * tq), 1)
            sink = jnp.where(col < tq, sink_ref[pair], sink_ref[pair + N_HEADS // 2])
            m_fin = jnp.maximum(m, sink)
            scale = jnp.exp2(m - m_fin)
            l = l * scale + jnp.exp2(sink - m_fin)
            o_t = o_t * scale
        o_t = o_t / l
        o_ref[0, rows, :] = jnp.where(sub < HEAD_DIM, o_t[:, :tq], o_t[:, tq:]).T.astype(o_ref.dtype)

    if banded:
        n_var = bias_ref.shape[1]

        def band_scores(sb, s_ref):
            ib = i * n_sub + sb
            start = pl.multiple_of(ib * tq, tq)
            rows = pl.ds(pl.multiple_of(sb * tq, tq), tq)
            v = jnp.maximum(n_var - 1 - ib, 0)
            s_ref[...] = (_dot_nt(kpad_ref[pl.ds(start, window), :], pair_rows(q_ref[0, rows, :]))
                          + bias_ref[0, v])

        def band_softmax(sb, s_ref):
            start = pl.multiple_of((i * n_sub + sb) * tq, tq)
            s = s_ref[...]
            m = jnp.max(s, axis=0, keepdims=True)
            p = jnp.exp2(s - m).astype(BF16)
            finish(_dot(vt_ref[:, pl.ds(start, window)], p), m,
                   pl.ds(pl.multiple_of(sb * tq, tq), tq))

        group = BAND_GROUP
        n_groups = n_sub // group

        def group_scores(g, parity):
            for r in range(group):
                band_scores(g * group + r, sbuf_ref.at[parity * group + r])

        def group_softmax(g, parity):
            for r in range(group):
                band_softmax(g * group + r, sbuf_ref.at[parity * group + r])

        group_scores(0, 0)

        def two_groups(u, carry):
            g = 2 * u
            group_scores(g + 1, 1)
            group_softmax(g, 0)
            group_scores(g + 2, 0)
            group_softmax(g + 1, 1)
            return carry

        lax.fori_loop(0, n_groups // 2 - 1, two_groups, None)
        group_scores(n_groups - 1, 1)
        group_softmax(n_groups - 2, 0)
        group_softmax(n_groups - 1, 1)
        return

    if variant == "mla":
        query_operands = lambda ref: (ref[0, :, :LANES], ref[0, :, LANES:])
    else:
        query_operands = lambda ref: pair_rows(ref[0])

    m_ref[...] = jnp.full_like(m_ref, MASK)
    acc_ref[...] = jnp.zeros_like(acc_ref)

    first_half = [(h * tq, tk) for h in range(2)]
    second_half = [(h * tq + tk, tk) for h in range(2)]

    q_now = query_operands(q_ref)

    def scores(j, s_ref, slabs=((0, 2 * tq),), queries=q_now):
        start = pl.multiple_of(j * tk, tk)
        k = k_ref[0, pl.ds(start, tk), :]
        for c0, w in slabs:
            cols = slice(c0, c0 + w)
            if variant == "mla":
                for h in range(2):
                    lo, hi = max(c0, h * tq), min(c0 + w, (h + 1) * tq)
                    if lo < hi:
                        q_h = queries[h][lo - h * tq:hi - h * tq, :]
                        s_ref[:, lo:hi] = _dot_nt(k[:, h * LANES:(h + 1) * LANES], q_h)
                continue
            s = _dot_nt(k, queries[cols, :])
            if variant == "fox":
                ck_reps = [ckrep_ref[pl.ds(start, tk), (c // tq) * LANES:(c // tq + 1) * LANES]
                           for c in range(c0, c0 + w, LANES)]
                s = s - jnp.concatenate(ck_reps, axis=1)
            s_ref[:, cols] = s

    def consume(j, s_ref, slabs, masked=()):
        start = pl.multiple_of(j * tk, tk)
        v_t = vt_ref[:, pl.ds(start, tk)]
        for c0, w in slabs:
            cols = slice(c0, c0 + w)
            s = s_ref[:, cols]
            if (c0, w) in masked:
                key = lax.broadcasted_iota(jnp.int32, (tk, w), 0)
                qry = lax.broadcasted_iota(jnp.int32, (tk, w), 1)
                if variant == "mla":
                    keep = (key >> 6) <= (qry >> 6)
                else:
                    keep = key <= qry
                s = jnp.where(keep, s, MASK)
            m_prev = m_ref[:, cols]
            m_new = jnp.maximum(m_prev, jnp.max(s, axis=0, keepdims=True))
            alpha = jnp.exp2(m_prev - m_new)
            p = jnp.exp2(s - m_new).astype(BF16)
            acc_ref[:, cols] = alpha * acc_ref[:, cols] + _dot(v_t, p)
            m_ref[:, cols] = m_new

    everything = [(0, 2 * tq)]

    @pl.when(i == 0)
    def _():
        scores(0, sa_ref)

    def two_blocks(u, carry):
        t = 2 * u
        scores(t + 1, sb_ref)
        consume(t, sa_ref, everything)
        scores(t + 2, sa_ref)
        consume(t + 1, sb_ref, everything)
        return carry

    lax.fori_loop(0, i, two_blocks, None)

    def diagonal_blocks(prefetch_next):
        q_next = query_operands(qn_ref) if prefetch_next else None
        scores(2 * i + 1, sb_ref, second_half)
        consume(2 * i, sa_ref, first_half, masked=first_half)
        if prefetch_next:
            scores(0, sa_ref, first_half, queries=q_next)
        consume(2 * i, sa_ref, second_half)
        if prefetch_next:
            scores(0, sa_ref, second_half, queries=q_next)
        consume(2 * i + 1, sb_ref, second_half, masked=second_half)

    last = pl.num_programs(2) - 1
    pl.when(i < last)(lambda: diagonal_blocks(True))
    pl.when(i == last)(lambda: diagonal_blocks(False))
    finish(acc_ref[...], m_ref[...], slice(0, tq))


def _attention(variant, q, k, v, q_col, k_col, v_col, *, extra=None, sinks=None):
    b, s, _ = q.shape
    n_pairs = N_HEADS // 2
    if variant in ("mla", "fox"):
        tq, tk, n_sub = 2 * FULL_TILE, FULL_TILE, 1
    else:
        tq, tk, n_sub = TQ, TK, BAND_SUB_BLOCKS
    nq = s // (tq * n_sub)
    window = {"mla": None, "fox": None, "swa": SWA_WINDOW_KEYS, "ck": CK_WINDOW_KEYS}[variant]
    qw = 2 * LANES if variant == "mla" else LANES
    shared_kv = variant == "swa"

    def im(f):
        if sinks is None:
            return f
        return lambda p, bi, i, sink_ref: f(p, bi, i)

    in_specs = [
        pl.BlockSpec((1, tq * n_sub, qw), im(lambda p, bi, i: (bi, i, q_col + p))),
        pl.BlockSpec((1, s, qw), im(lambda p, bi, i: (bi, 0, k_col + (0 if shared_kv else p)))),
        pl.BlockSpec((1, s, LANES), im(lambda p, bi, i: (bi, 0, v_col + (0 if shared_kv else p)))),
    ]
    args = [q, k, v]
    if window is None:
        in_specs.append(pl.BlockSpec((1, tq, qw), lambda p, bi, i: (bi, jnp.minimum(i + 1, nq - 1),
                                                                     q_col + p)))
        args.append(q)
        scratch = [pltpu.VMEM((VT_ROWS, s), BF16), pltpu.VMEM((1, 2 * tq), F32),
                   pltpu.VMEM((VT_ROWS, 2 * tq), F32),
                   pltpu.VMEM((tk, 2 * tq), F32), pltpu.VMEM((tk, 2 * tq), F32)]
    else:
        pad = window - tq
        scratch = [pltpu.VMEM((VT_ROWS, pad + s), BF16), pltpu.VMEM((pad + s, LANES), BF16),
                   pltpu.VMEM((2 * BAND_GROUP, window, 2 * tq), F32)]
        in_specs.append(pl.BlockSpec((1,) + extra.shape[1:], im(lambda p, bi, i: (p, 0, 0, 0))))
        args.append(extra)
    if variant == "fox":
        in_specs.append(pl.BlockSpec((1, s, LANES), lambda p, bi, i: (bi, 0, 0)))
        args.append(extra)
        scratch.append(pltpu.VMEM((s, 2 * LANES), F32))
    out_spec = pl.BlockSpec((1, tq * n_sub, LANES), im(lambda p, bi, i: (bi, i, p)))
    kern = functools.partial(_attn_kernel, variant=variant, window=window, seq=s, tq=tq, tk=tk,
                             n_sub=n_sub)
    params = pltpu.CompilerParams(dimension_semantics=("arbitrary",) * 3, vmem_limit_bytes=VMEM_LIMIT)
    out_shape = jax.ShapeDtypeStruct((b, s, n_pairs * LANES), BF16)
    if sinks is not None:
        grid_spec = pltpu.PrefetchScalarGridSpec(
            num_scalar_prefetch=1, grid=(n_pairs, b, nq), in_specs=in_specs, out_specs=out_spec,
            scratch_shapes=scratch)
        return pl.pallas_call(kern, grid_spec=grid_spec, out_shape=out_shape,
                              compiler_params=params, name="attn_" + variant)(sinks, *args)
    return pl.pallas_call(kern, grid=(n_pairs, b, nq), in_specs=in_specs, out_specs=out_spec,
                          out_shape=out_shape, scratch_shapes=scratch,
                          compiler_params=params, name="attn_" + variant)(*args)


def _lead_rows(window, n_var, v):
    return max(window - TQ - (n_var - 1 - v) * TQ, 0)


def _swa_bias():
    window, left = SWA_WINDOW_KEYS, SWA_LEFT_CHUNKS
    c = jnp.arange(window)[:, None]
    a = jnp.arange(TQ)[None, :]
    dist = a - c + (window - TQ)
    ahead = (a // CHUNK + left) - c // CHUNK
    slopes = jnp.exp2(-8.0 * jnp.arange(1, N_HEADS + 1, dtype=F32) / N_HEADS)
    bias = -slopes[:, None, None] * jnp.abs(dist).astype(F32)[None] * LOG2E
    bias = jnp.where(((ahead >= 0) & (ahead <= left))[None], bias, MASK)
    bias = jnp.concatenate([bias[:4], bias[4:]], axis=2)
    n_var = 2
    return jnp.stack([jnp.where(c[None] >= _lead_rows(window, n_var, v), bias, MASK)
                      for v in range(n_var)], axis=1)


_REL_ROWS = 384


_CK_VARIANTS = 3


def _ck_bias_kernel(tab_ref, out_ref):
    window = CK_WINDOW_KEYS
    span = TQ + window
    tab = tab_ref[0] * LOG2E
    hi = tab.astype(BF16)
    rem = tab - hi.astype(F32)
    mid = rem.astype(BF16)
    lo = (rem - mid.astype(F32)).astype(BF16)
    t = lax.broadcasted_iota(jnp.int32, (_REL_ROWS, span), 0)
    n = lax.broadcasted_iota(jnp.int32, (_REL_ROWS, span), 1)
    delta = jnp.where(n < TQ, n, n - span)
    idx = jnp.clip(delta + (window - TQ), -(CHUNK - 1), REL_MAX) + (CHUNK - 1)
    onehot = jnp.where(t == idx, 1.0, 0.0).astype(BF16)
    base = _dot(hi, onehot) + _dot(mid, onehot) + _dot(lo, onehot)
    c = lax.broadcasted_iota(jnp.int32, (window, TQ), 0)
    a = lax.broadcasted_iota(jnp.int32, (window, TQ), 1)
    valid = ((a >> 6) + CK_LEFT_CHUNKS - (c >> 6)).astype(jnp.uint32) <= CK_LEFT_CHUNKS
    for h in range(2):
        rows = jnp.broadcast_to(base[h:h + 1, :], (window, span))
        tile = jnp.where(valid, pltpu.roll(rows, 0, 1, stride=1, stride_axis=0)[:, :TQ], MASK)
        for v in range(_CK_VARIANTS):
            lead = _lead_rows(window, _CK_VARIANTS, v)
            out_ref[0, v, :, h * TQ:(h + 1) * TQ] = jnp.where(c >= lead, tile, MASK)


def _ck_bias(rel_bias):
    tab = jnp.pad(rel_bias.T, ((0, 0), (0, _REL_ROWS - rel_bias.shape[0])))
    tab = jnp.pad(tab.reshape(N_HEADS // 2, 2, _REL_ROWS), ((0, 0), (0, 6), (0, 0)))
    return pl.pallas_call(
        _ck_bias_kernel,
        grid=(N_HEADS // 2,),
        in_specs=[pl.BlockSpec((1, 8, _REL_ROWS), lambda p: (p, 0, 0))],
        out_specs=pl.BlockSpec((1, _CK_VARIANTS, CK_WINDOW_KEYS, 2 * TQ), lambda p: (p, 0, 0, 0)),
        out_shape=jax.ShapeDtypeStruct((N_HEADS // 2, _CK_VARIANTS, CK_WINDOW_KEYS, 2 * TQ), F32),
        compiler_params=pltpu.CompilerParams(dimension_semantics=("arbitrary",),
                                             vmem_limit_bytes=VMEM_LIMIT),
        name="ck_bias",
    )(tab)


def _out_ffn_kernel(x_ref, oa_ref, ob_ref, woa_ref, wob_ref, g1_ref, b1_ref,
                    wg_ref, wu_ref, wd_ref, g2_ref, b2_ref, y_ref):
    mix = _dot(oa_ref[...], woa_ref[...]) + _dot(ob_ref[...], wob_ref[...])
    x1 = _layer_norm(DEEPNORM_ALPHA * x_ref[...] + mix, g1_ref[...], b1_ref[...])
    x1b = x1.astype(BF16)
    gate = _dot(x1b, wg_ref[...])
    up = _dot(x1b, wu_ref[...])
    hidden = (gate * jax.nn.sigmoid(gate) * up).astype(BF16)
    ffn = _dot(hidden, wd_ref[...])
    y_ref[...] = _layer_norm(DEEPNORM_ALPHA * x1 + ffn, g2_ref[...], b2_ref[...])


def _out_ffn(x2, o_a, o_b, w_out_a, w_out_b, g1, b1, w_gate, w_up, w_down, g2, b2):
    t = x2.shape[0]
    d_ff = w_gate.shape[1]
    row = lambda w: pl.BlockSpec((TM, w), lambda i: (i, 0))
    vec = _const_spec((1, D_MODEL))
    return pl.pallas_call(
        _out_ffn_kernel,
        grid=(t // TM,),
        in_specs=[row(D_MODEL), row(512), row(512), _const_spec((512, D_MODEL)),
                  _const_spec((512, D_MODEL)), vec, vec, _const_spec((D_MODEL, d_ff)),
                  _const_spec((D_MODEL, d_ff)), _const_spec((d_ff, D_MODEL)), vec, vec],
        out_specs=row(D_MODEL),
        out_shape=jax.ShapeDtypeStruct((t, D_MODEL), F32),
        compiler_params=pltpu.CompilerParams(dimension_semantics=("arbitrary",),
                                             vmem_limit_bytes=VMEM_LIMIT),
        name="out_ffn",
    )(x2, o_a, o_b, w_out_a.astype(BF16), w_out_b.astype(BF16), g1.reshape(1, -1), b1.reshape(1, -1),
      w_gate.astype(BF16), w_up.astype(BF16), w_down.astype(BF16), g2.reshape(1, -1), b2.reshape(1, -1))


def kernel(x, ab_w_in, ab_q_norm, ab_w_uq, ab_kv_norm, ab_w_ukv, ab_sinks, ab_w_out,
           cd_w_in, cd_b_forget, cd_rel_bias, cd_w_out,
           ln1_g, ln1_b, ffn_w_gate, ffn_w_up, ffn_w_down, ln2_g, ln2_b):
    b, s, d = x.shape
    assert d == D_MODEL and TQ == TK
    assert s % TM == 0 and s % (2 * FULL_TILE) == 0 and s % (TQ * BAND_SUB_BLOCKS) == 0
    t = b * s
    x2 = x.reshape(t, d)

    qa, ka, va, qs, ks, vs = _proj_ab(x2, ab_w_in[0], ab_q_norm[0], ab_w_uq[0], ab_kv_norm[0],
                                      ab_w_ukv[0], s)
    r3 = lambda a: a.reshape(b, s, a.shape[-1])
    o_a = _attention("mla", r3(qa), r3(ka), r3(va), 0, 0, 0)
    o_b = _attention("swa", r3(qs), r3(ks), r3(vs), 0, 0, 0, extra=_swa_bias(),
                     sinks=ab_sinks[0] * LOG2E)
    w_out = ab_w_out[0]
    w_out_b = w_out[512:].reshape(2, 4, HEAD_DIM, D_MODEL).transpose(1, 0, 2, 3).reshape(512, D_MODEL)
    x2 = _out_ffn(x2, o_a.reshape(t, 512), o_b.reshape(t, 512), w_out[:512], w_out_b,
                  ln1_g[0], ln1_b[0], ffn_w_gate[0], ffn_w_up[0], ffn_w_down[0], ln2_g[0], ln2_b[0])

    qkv, cum = _proj_cd(x2, cd_w_in[0], cd_b_forget[0], s)
    qkv = qkv.reshape(b, s, 3072)
    o_c = _attention("fox", qkv, qkv, qkv, 0, 4, 8, extra=cum.reshape(b, s, LANES))
    o_d = _attention("ck", qkv, qkv, qkv, 12, 16, 20, extra=_ck_bias(cd_rel_bias[0]))
    w_out = cd_w_out[0]
    x2 = _out_ffn(x2, o_c.reshape(t, 512), o_d.reshape(t, 512), w_out[:512], w_out[512:],
                  ln1_g[1], ln1_b[1], ffn_w_gate[1], ffn_w_up[1], ffn_w_down[1], ln2_g[1], ln2_b[1])
    return x2.reshape(b, s, d)
```

```python
import functools
import math

import numpy as np
import jax
import jax.numpy as jnp
from jax import lax
from jax.experimental import pallas as pl
from jax.experimental.pallas import tpu as pltpu

D_MODEL = 1024
CHUNK = 64
HEAD_DIM = 64
N_HEADS = 8
LANES = 128
LN_EPS = 1e-5
RMS_EPS = 1e-6
MASK = -1e30
LOG2E = math.log2(math.e)

MLA_Q_RANK = 384
MLA_KV_RANK = 256
MLA_NOPE = 64
MLA_ROPE = 32
ROPE_THETA = 10000.0
SWA_KV_HEADS = 2
SWA_LEFT_CHUNKS = 2
CK_LEFT_CHUNKS = 8
REL_MAX = 256
DEPTH = 2
DEEPNORM_ALPHA = (2 * DEPTH) ** 0.25

TQ = 256
TK = 256
BAND_SUB_BLOCKS = 16
BAND_GROUP = 2
SWA_WINDOW_KEYS = SWA_LEFT_CHUNKS * CHUNK + TQ
CK_WINDOW_KEYS = CK_LEFT_CHUNKS * CHUNK + TQ
FULL_TILE = 512
TM = 1024
TM_FFN = 512
CUMSUM_ROWS = 128
VT_ROWS = LANES + 16
VMEM_LIMIT = 56 * 1024 * 1024

BF16 = jnp.bfloat16
F32 = jnp.float32


def _dot(a, b):
    return jnp.dot(a, b, preferred_element_type=F32)


def _dot_nt(a, b):
    return lax.dot_general(a, b, (((1,), (1,)), ((), ())), preferred_element_type=F32)


def _layer_norm(y, g, b):
    mu = jnp.mean(y, axis=-1, keepdims=True)
    yc = y - mu
    var = jnp.mean(yc * yc, axis=-1, keepdims=True)
    return yc * lax.rsqrt(var + LN_EPS) * g + b


def _rms_norm(c, g):
    return c * lax.rsqrt(jnp.mean(c * c, axis=-1, keepdims=True) + RMS_EPS) * g


def _const_spec(shape):
    zeros = (0,) * len(shape)
    return pl.BlockSpec(shape, lambda *_: zeros, pipeline_mode=pl.Buffered(1))


_AB_CQ, _AB_CKV, _AB_QS, _AB_KS, _AB_VS, _AB_KR, _AB_COLS = 0, 384, 640, 1152, 1280, 1408, 1536


def _rope_mix(t, m1, m2):
    return t * m1 + pltpu.roll(t, 96, 1) * m2


def _proj_ab_kernel(x_ref, win_ref, qn_ref, wuq_ref, kvn_ref, wukv_ref,
                    m1q_ref, m2q_ref, m1k_ref, m2k_ref,
                    qa_ref, ka_ref, va_ref, qs_ref, ks_ref, vs_ref):
    proj = _dot(x_ref[...].astype(BF16), win_ref[...])
    c_q = _rms_norm(proj[:, _AB_CQ:_AB_CKV], qn_ref[...])
    qf = _dot(c_q.astype(BF16), wuq_ref[...])
    c_kv = _rms_norm(proj[:, _AB_CKV:_AB_QS], kvn_ref[...])
    kvf = _dot(c_kv.astype(BF16), wukv_ref[...])
    k_rope = _rope_mix(proj[:, _AB_KR:_AB_COLS], m1k_ref[...], m2k_ref[...])
    m1q, m2q = m1q_ref[...], m2q_ref[...]
    for h in range(N_HEADS):
        sl = slice(h * LANES, (h + 1) * LANES)
        qa_ref[:, sl] = _rope_mix(qf[:, sl], m1q, m2q).astype(BF16)
        ka_ref[:, sl] = (kvf[:, sl] + k_rope).astype(BF16)
    va_ref[...] = kvf[:, N_HEADS * LANES:].astype(BF16)
    qs_ref[...] = (proj[:, _AB_QS:_AB_KS] * (HEAD_DIM ** -0.5 * LOG2E)).astype(BF16)
    ks_ref[...] = proj[:, _AB_KS:_AB_VS].astype(BF16)
    vs_ref[...] = proj[:, _AB_VS:_AB_KR].astype(BF16)


def _proj_ab(x2, w_in, q_norm, w_uq, kv_norm, w_ukv, seq):
    t = x2.shape[0]
    nsb = seq // TM
    c_q, c_kv, k_r, q_s, k_s, v_s = jnp.split(w_in, [384, 640, 672, 1184, 1312], axis=1)
    q_s = q_s.reshape(D_MODEL, 2, 4, HEAD_DIM).transpose(0, 2, 1, 3).reshape(D_MODEL, 512)
    kr_blk = jnp.concatenate([jnp.zeros((D_MODEL, 64), F32), k_r, k_r[:, 16:], k_r[:, :16]], axis=1)
    win_p = jnp.concatenate([c_q, c_kv, q_s, k_s, v_s, kr_blk], axis=1).astype(BF16)
    wq = w_uq.reshape(MLA_Q_RANK, N_HEADS, MLA_NOPE + MLA_ROPE)
    wuq_p = jnp.concatenate([wq, wq[..., 80:96], wq[..., 64:80]], axis=-1)
    wuq_p = wuq_p.reshape(MLA_Q_RANK, N_HEADS * LANES).astype(BF16)
    wkv = w_ukv.reshape(MLA_KV_RANK, N_HEADS, 2 * HEAD_DIM)
    k_pad = jnp.concatenate([wkv[..., :64], jnp.zeros_like(wkv[..., :64])], axis=-1)
    wukv_p = jnp.concatenate([k_pad.reshape(MLA_KV_RANK, N_HEADS * LANES),
                              wkv[..., 64:].reshape(MLA_KV_RANK, 512)], axis=1).astype(BF16)
    inv = ROPE_THETA ** (-jnp.arange(0, MLA_ROPE, 2, dtype=F32) / MLA_ROPE)
    ang = jnp.arange(seq, dtype=F32)[:, None] * inv[None, :]
    cos, sin = jnp.cos(ang), jnp.sin(ang)
    z64, z32, o64 = jnp.zeros((seq, 64), F32), jnp.zeros((seq, 32), F32), jnp.ones((seq, 64), F32)
    m1k = jnp.concatenate([z64, cos, cos, z32], axis=1)
    m2k = jnp.concatenate([z64, -sin, sin, z32], axis=1)
    q_scale = (MLA_NOPE + MLA_ROPE) ** -0.5 * LOG2E
    m1q = jnp.concatenate([o64, cos, cos, z32], axis=1) * q_scale
    m2q = m2k * q_scale

    row = lambda w: pl.BlockSpec((TM, w), lambda i: (i, 0))
    tab = pl.BlockSpec((TM, LANES), lambda i: (i % nsb, 0))
    out_w = (1024, 1024, 512, 512, 128, 128)
    return pl.pallas_call(
        _proj_ab_kernel,
        grid=(t // TM,),
        in_specs=[row(D_MODEL), _const_spec((D_MODEL, _AB_COLS)), _const_spec((1, MLA_Q_RANK)),
                  _const_spec((MLA_Q_RANK, 1024)), _const_spec((1, MLA_KV_RANK)),
                  _const_spec((MLA_KV_RANK, 1536)), tab, tab, tab, tab],
        out_specs=[row(w) for w in out_w],
        out_shape=[jax.ShapeDtypeStruct((t, w), BF16) for w in out_w],
        compiler_params=pltpu.CompilerParams(dimension_semantics=("arbitrary",),
                                             vmem_limit_bytes=VMEM_LIMIT),
        name="proj_ab",
    )(x2, win_p, q_norm.reshape(1, -1), wuq_p, kv_norm.reshape(1, -1), wukv_p, m1q, m2q, m1k, m2k)


def _proj_cd_kernel(x_ref, win_ref, bf_ref, qkv_ref, cum_ref, carry_ref, *, nsb):
    @pl.when(pl.program_id(0) % nsb == 0)
    def _():
        carry_ref[...] = jnp.zeros_like(carry_ref)

    proj = _dot(x_ref[...].astype(BF16), win_ref[...])
    q_scale = HEAD_DIM ** -0.5 * LOG2E
    for blk, scale in enumerate((q_scale, None, None, q_scale, None, None)):
        part = proj[:, blk * 512:(blk + 1) * 512]
        qkv_ref[:, blk * 512:(blk + 1) * 512] = (part if scale is None else part * scale).astype(BF16)
    z = proj[:, 3072:] + bf_ref[...]
    log_f = jnp.minimum(z, 0.0) - jnp.log(1.0 + jnp.exp(-jnp.abs(z)))
    r = lax.broadcasted_iota(jnp.int32, (CUMSUM_ROWS, CUMSUM_ROWS), 0)
    c = lax.broadcasted_iota(jnp.int32, (CUMSUM_ROWS, CUMSUM_ROWS), 1)
    tri = jnp.where(c <= r, 1.0, 0.0).astype(BF16)
    hi = log_f.astype(BF16)
    rem = log_f - hi.astype(F32)
    mid = rem.astype(BF16)
    lo = (rem - mid.astype(F32)).astype(BF16)
    carry = carry_ref[...]
    for blk in range(TM // CUMSUM_ROWS):
        rows = slice(blk * CUMSUM_ROWS, (blk + 1) * CUMSUM_ROWS)
        cum = _dot(tri, hi[rows]) + _dot(tri, mid[rows]) + _dot(tri, lo[rows]) + carry
        cum_ref[rows, :] = cum
        carry = cum[CUMSUM_ROWS - 1:CUMSUM_ROWS, :]
    carry_ref[...] = carry


def _proj_cd(x2, w_in, b_forget, seq):
    t = x2.shape[0]
    q_f, k_f, v_f, f_l, q_c, k_c, v_c = jnp.split(w_in, [512, 1024, 1536, 1544, 2056, 2568], axis=1)
    f_pad = jnp.concatenate([f_l, jnp.zeros((D_MODEL, LANES - N_HEADS), F32)], axis=1)
    win_p = jnp.concatenate([q_f, k_f, v_f, q_c, k_c, v_c, f_pad], axis=1).astype(BF16)
    bf = jnp.concatenate([b_forget, jnp.zeros((LANES - N_HEADS,), F32)]).reshape(1, LANES)
    return pl.pallas_call(
        functools.partial(_proj_cd_kernel, nsb=seq // TM),
        grid=(t // TM,),
        in_specs=[pl.BlockSpec((TM, D_MODEL), lambda i: (i, 0)),
                  _const_spec((D_MODEL, 3072 + LANES)), _const_spec((1, LANES))],
        out_specs=[pl.BlockSpec((TM, 3072), lambda i: (i, 0)),
                   pl.BlockSpec((TM, LANES), lambda i: (i, 0))],
        out_shape=[jax.ShapeDtypeStruct((t, 3072), BF16), jax.ShapeDtypeStruct((t, LANES), F32)],
        scratch_shapes=[pltpu.VMEM((1, LANES), F32)],
        compiler_params=pltpu.CompilerParams(dimension_semantics=("arbitrary",),
                                             vmem_limit_bytes=VMEM_LIMIT),
        name="proj_cd",
    )(x2, win_p, bf)


def _attn_kernel(*refs, variant, window, seq, tq, tk, n_sub):
    if variant == "mla":
        q_ref, k_ref, v_ref, qn_ref, o_ref, vt_ref, m_ref, acc_ref, sa_ref, sb_ref = refs
    elif variant == "fox":
        (q_ref, k_ref, v_ref, qn_ref, ck_ref, o_ref, vt_ref, m_ref, acc_ref, sa_ref, sb_ref,
         ckrep_ref) = refs
    elif variant == "swa":
        sink_ref, q_ref, k_ref, v_ref, bias_ref, o_ref, vt_ref, kpad_ref, sbuf_ref = refs
    else:
        q_ref, k_ref, v_ref, bias_ref, o_ref, vt_ref, kpad_ref, sbuf_ref = refs

    i = pl.program_id(2)
    banded = window is not None
    pad = window - tq if banded else 0

    def setup():
        for c in range(seq // 512):
            sl = slice(pad + c * 512, pad + (c + 1) * 512)
            vt_ref[0:LANES, sl] = v_ref[0, c * 512:(c + 1) * 512, :].astype(F32).T.astype(BF16)
        vt_ref[LANES:VT_ROWS, :] = jnp.ones((VT_ROWS - LANES, pad + seq), BF16)
        if banded:
            vt_ref[0:LANES, 0:pad] = jnp.zeros((LANES, pad), BF16)
            kpad_ref[0:pad, :] = jnp.zeros((pad, LANES), BF16)
            kpad_ref[pad:pad + seq, :] = k_ref[0]
        if variant == "fox":
            src = lax.broadcasted_iota(jnp.int32, (LANES, 2 * LANES), 0)
            dst = lax.broadcasted_iota(jnp.int32, (LANES, 2 * LANES), 1)
            pick = jnp.where(src == 2 * pl.program_id(0) + (dst >> 7), 1.0, 0.0).astype(BF16)
            for c in range(seq // 512):
                rows = slice(c * 512, (c + 1) * 512)
                cum = ck_ref[0, rows, :]
                hi = cum.astype(BF16)
                rem = cum - hi.astype(F32)
                mid = rem.astype(BF16)
                lo = (rem - mid.astype(F32)).astype(BF16)
                ckrep_ref[rows, :] = (_dot(hi, pick) + _dot(mid, pick) + _dot(lo, pick)) * LOG2E

    sub = lax.broadcasted_iota(jnp.int32, (LANES, tq), 0)
    lane = lax.broadcasted_iota(jnp.int32, (tq, LANES), 1)

    def pair_rows(q_blk):
        zero = jnp.zeros_like(q_blk)
        return jnp.concatenate([jnp.where(lane < HEAD_DIM, q_blk, zero),
                                jnp.where(lane >= HEAD_DIM, q_blk, zero)], axis=0)

    def finish(acc, m, rows):
        l = acc[LANES:LANES + 1, :]
        o_t = acc[:LANES, :]
        if variant == "swa":
            pair = pl.program_id(0)
            col = lax.broadcasted_iota(jnp.int32, (1, 2 * tq), 1)
            sink = jnp.where(col < tq, sink_ref[pair], sink_ref[pair + N_HEADS // 2])
            m_fin = jnp.maximum(m, sink)
            scale = jnp.exp2(m - m_fin)
            l = l * scale + jnp.exp2(sink - m_fin)
            o_t = o_t * scale
        o_t = o_t / l
        o_ref[0, rows, :] = jnp.where(sub < HEAD_DIM, o_t[:, :tq], o_t[:, tq:]).T.astype(o_ref.dtype)

    if banded:
        pl.when(i == 0)(setup)
        n_var = bias_ref.shape[1]

        def band_scores(sb, s_ref):
            ib = i * n_sub + sb
            start = pl.multiple_of(ib * tq, tq)
            rows = pl.ds(pl.multiple_of(sb * tq, tq), tq)
            v = jnp.maximum(n_var - 1 - ib, 0)
            s_ref[...] = (_dot_nt(kpad_ref[pl.ds(start, window), :], pair_rows(q_ref[0, rows, :]))
                          + bias_ref[0, v])

        def band_softmax(sb, s_ref):
            start = pl.multiple_of((i * n_sub + sb) * tq, tq)
            s = s_ref[...]
            m = jnp.max(s, axis=0, keepdims=True)
            p = jnp.exp2(s - m).astype(BF16)
            finish(_dot(vt_ref[:, pl.ds(start, window)], p), m,
                   pl.ds(pl.multiple_of(sb * tq, tq), tq))

        group = BAND_GROUP
        n_groups = n_sub // group

        def group_scores(g, parity):
            for r in range(group):
                band_scores(g * group + r, sbuf_ref.at[parity * group + r])

        def group_softmax(g, parity):
            for r in range(group):
                band_softmax(g * group + r, sbuf_ref.at[parity * group + r])

        group_scores(0, 0)

        def two_groups(u, carry):
            g = 2 * u
            group_scores(g + 1, 1)
            group_softmax(g, 0)
            group_scores(g + 2, 0)
            group_softmax(g + 1, 1)
            return carry

        lax.fori_loop(0, n_groups // 2 - 1, two_groups, None)
        group_scores(n_groups - 1, 1)
        group_softmax(n_groups - 2, 0)
        group_softmax(n_groups - 1, 1)
        return

    if variant == "mla":
        query_operands = lambda ref: (ref[0, :, :LANES], ref[0, :, LANES:])
    else:
        query_operands = lambda ref: pair_rows(ref[0])

    m_ref[...] = jnp.full_like(m_ref, MASK)
    acc_ref[...] = jnp.zeros_like(acc_ref)

    first_half = [(h * tq, tk) for h in range(2)]
    second_half = [(h * tq + tk, tk) for h in range(2)]

    q_now = query_operands(q_ref)

    def scores(j, s_ref, slabs=((0, 2 * tq),), queries=q_now):
        start = pl.multiple_of(j * tk, tk)
        k = k_ref[0, pl.ds(start, tk), :]
        for c0, w in slabs:
            cols = slice(c0, c0 + w)
            if variant == "mla":
                for h in range(2):
                    lo, hi = max(c0, h * tq), min(c0 + w, (h + 1) * tq)
                    if lo < hi:
                        q_h = queries[h][lo - h * tq:hi - h * tq, :]
                        s_ref[:, lo:hi] = _dot_nt(k[:, h * LANES:(h + 1) * LANES], q_h)
                continue
            s = _dot_nt(k, queries[cols, :])
            if variant == "fox":
                ck_reps = [ckrep_ref[pl.ds(start, tk), (c // tq) * LANES:(c // tq + 1) * LANES]
                           for c in range(c0, c0 + w, LANES)]
                s = s - jnp.concatenate(ck_reps, axis=1)
            s_ref[:, cols] = s

    def consume(j, s_ref, slabs, masked=()):
        start = pl.multiple_of(j * tk, tk)
        v_t = vt_ref[:, pl.ds(start, tk)]
        for c0, w in slabs:
            cols = slice(c0, c0 + w)
            s = s_ref[:, cols]
            if (c0, w) in masked:
                key = lax.broadcasted_iota(jnp.int32, (tk, w), 0)
                qry = lax.broadcasted_iota(jnp.int32, (tk, w), 1)
                if variant == "mla":
                    keep = (key >> 6) <= (qry >> 6)
                else:
                    keep = key <= qry
                s = jnp.where(keep, s, MASK)
            m_prev = m_ref[:, cols]
            m_new = jnp.maximum(m_prev, jnp.max(s, axis=0, keepdims=True))
            alpha = jnp.exp2(m_prev - m_new)
            p = jnp.exp2(s - m_new).astype(BF16)
            acc_ref[:, cols] = alpha * acc_ref[:, cols] + _dot(v_t, p)
            m_ref[:, cols] = m_new

    everything = [(0, 2 * tq)]

    @pl.when(i == 0)
    def _():
        setup()
        scores(0, sa_ref)

    def two_blocks(u, carry):
        t = 2 * u
        scores(t + 1, sb_ref)
        consume(t, sa_ref, everything)
        scores(t + 2, sa_ref)
        consume(t + 1, sb_ref, everything)
        return carry

    lax.fori_loop(0, i, two_blocks, None)

    def diagonal_blocks(prefetch_next):
        q_next = query_operands(qn_ref) if prefetch_next else None
        scores(2 * i + 1, sb_ref, second_half)
        consume(2 * i, sa_ref, first_half, masked=first_half)
        if prefetch_next:
            scores(0, sa_ref, first_half, queries=q_next)
        consume(2 * i, sa_ref, second_half)
        if prefetch_next:
            scores(0, sa_ref, second_half, queries=q_next)
        consume(2 * i + 1, sb_ref, second_half, masked=second_half)

    last = pl.num_programs(2) - 1
    pl.when(i < last)(lambda: diagonal_blocks(True))
    pl.when(i == last)(lambda: diagonal_blocks(False))
    finish(acc_ref[...], m_ref[...], slice(0, tq))


def _attention(variant, q, k, v, q_col, k_col, v_col, *, extra=None, sinks=None):
    b, s, _ = q.shape
    n_pairs = N_HEADS // 2
    if variant in ("mla", "fox"):
        tq, tk, n_sub = 2 * FULL_TILE, FULL_TILE, 1
    else:
        tq, tk, n_sub = TQ, TK, BAND_SUB_BLOCKS
    nq = s // (tq * n_sub)
    window = {"mla": None, "fox": None, "swa": SWA_WINDOW_KEYS, "ck": CK_WINDOW_KEYS}[variant]
    qw = 2 * LANES if variant == "mla" else LANES
    shared_kv = variant == "swa"

    def im(f):
        if sinks is None:
            return f
        return lambda p, bi, i, sink_ref: f(p, bi, i)

    in_specs = [
        pl.BlockSpec((1, tq * n_sub, qw), im(lambda p, bi, i: (bi, i, q_col + p))),
        pl.BlockSpec((1, s, qw), im(lambda p, bi, i: (bi, 0, k_col + (0 if shared_kv else p)))),
        pl.BlockSpec((1, s, LANES), im(lambda p, bi, i: (bi, 0, v_col + (0 if shared_kv else p)))),
    ]
    args = [q, k, v]
    if window is None:
        in_specs.append(pl.BlockSpec((1, tq, qw), lambda p, bi, i: (bi, jnp.minimum(i + 1, nq - 1),
                                                                     q_col + p)))
        args.append(q)
        scratch = [pltpu.VMEM((VT_ROWS, s), BF16), pltpu.VMEM((1, 2 * tq), F32),
                   pltpu.VMEM((VT_ROWS, 2 * tq), F32),
                   pltpu.VMEM((tk, 2 * tq), F32), pltpu.VMEM((tk, 2 * tq), F32)]
    else:
        pad = window - tq
        scratch = [pltpu.VMEM((VT_ROWS, pad + s), BF16), pltpu.VMEM((pad + s, LANES), BF16),
                   pltpu.VMEM((2 * BAND_GROUP, window, 2 * tq), F32)]
        in_specs.append(pl.BlockSpec((1,) + extra.shape[1:], im(lambda p, bi, i: (p, 0, 0, 0))))
        args.append(extra)
    if variant == "fox":
        in_specs.append(pl.BlockSpec((1, s, LANES), lambda p, bi, i: (bi, 0, 0)))
        args.append(extra)
        scratch.append(pltpu.VMEM((s, 2 * LANES), F32))
    out_spec = pl.BlockSpec((1, tq * n_sub, LANES), im(lambda p, bi, i: (bi, i, p)))
    kern = functools.partial(_attn_kernel, variant=variant, window=window, seq=s, tq=tq, tk=tk,
                             n_sub=n_sub)
    params = pltpu.CompilerParams(dimension_semantics=("arbitrary",) * 3, vmem_limit_bytes=VMEM_LIMIT)
    out_shape = jax.ShapeDtypeStruct((b, s, n_pairs * LANES), BF16)
    if sinks is not None:
        grid_spec = pltpu.PrefetchScalarGridSpec(
            num_scalar_prefetch=1, grid=(n_pairs, b, nq), in_specs=in_specs, out_specs=out_spec,
            scratch_shapes=scratch)
        return pl.pallas_call(kern, grid_spec=grid_spec, out_shape=out_shape,
                              compiler_params=params, name="attn_" + variant)(sinks, *args)
    return pl.pallas_call(kern, grid=(n_pairs, b, nq), in_specs=in_specs, out_specs=out_spec,
                          out_shape=out_shape, scratch_shapes=scratch,
                          compiler_params=params, name="attn_" + variant)(*args)


def _lead_rows(window, n_var, v):
    return max(window - TQ - (n_var - 1 - v) * TQ, 0)


def _swa_bias():
    window, left = SWA_WINDOW_KEYS, SWA_LEFT_CHUNKS
    c = jnp.arange(window)[:, None]
    a = jnp.arange(TQ)[None, :]
    dist = a - c + (window - TQ)
    ahead = (a // CHUNK + left) - c // CHUNK
    slopes = jnp.exp2(-8.0 * jnp.arange(1, N_HEADS + 1, dtype=F32) / N_HEADS)
    bias = -slopes[:, None, None] * jnp.abs(dist).astype(F32)[None] * LOG2E
    bias = jnp.where(((ahead >= 0) & (ahead <= left))[None], bias, MASK)
    bias = jnp.concatenate([bias[:4], bias[4:]], axis=2)
    n_var = 2
    return jnp.stack([jnp.where(c[None] >= _lead_rows(window, n_var, v), bias, MASK)
                      for v in range(n_var)], axis=1)


_REL_ROWS = 384


_CK_VARIANTS = 3


def _ck_bias_kernel(tab_ref, out_ref):
    window = CK_WINDOW_KEYS
    span = TQ + window
    tab = tab_ref[0] * LOG2E
    hi = tab.astype(BF16)
    rem = tab - hi.astype(F32)
    mid = rem.astype(BF16)
    lo = (rem - mid.astype(F32)).astype(BF16)
    t = lax.broadcasted_iota(jnp.int32, (_REL_ROWS, span), 0)
    n = lax.broadcasted_iota(jnp.int32, (_REL_ROWS, span), 1)
    delta = jnp.where(n < TQ, n, n - span)
    idx = jnp.clip(delta + (window - TQ), -(CHUNK - 1), REL_MAX) + (CHUNK - 1)
    onehot = jnp.where(t == idx, 1.0, 0.0).astype(BF16)
    base = _dot(hi, onehot) + _dot(mid, onehot) + _dot(lo, onehot)
    c = lax.broadcasted_iota(jnp.int32, (window, TQ), 0)
    a = lax.broadcasted_iota(jnp.int32, (window, TQ), 1)
    valid = ((a >> 6) + CK_LEFT_CHUNKS - (c >> 6)).astype(jnp.uint32) <= CK_LEFT_CHUNKS
    for h in range(2):
        rows = jnp.broadcast_to(base[h:h + 1, :], (window, span))
        tile = jnp.where(valid, pltpu.roll(rows, 0, 1, stride=1, stride_axis=0)[:, :TQ], MASK)
        for v in range(_CK_VARIANTS):
            lead = _lead_rows(window, _CK_VARIANTS, v)
            out_ref[0, v, :, h * TQ:(h + 1) * TQ] = jnp.where(c >= lead, tile, MASK)


def _ck_bias(rel_bias):
    tab = jnp.pad(rel_bias.T, ((0, 0), (0, _REL_ROWS - rel_bias.shape[0])))
    tab = jnp.pad(tab.reshape(N_HEADS // 2, 2, _REL_ROWS), ((0, 0), (0, 6), (0, 0)))
    return pl.pallas_call(
        _ck_bias_kernel,
        grid=(N_HEADS // 2,),
        in_specs=[pl.BlockSpec((1, 8, _REL_ROWS), lambda p: (p, 0, 0))],
        out_specs=pl.BlockSpec((1, _CK_VARIANTS, CK_WINDOW_KEYS, 2 * TQ), lambda p: (p, 0, 0, 0)),
        out_shape=jax.ShapeDtypeStruct((N_HEADS // 2, _CK_VARIANTS, CK_WINDOW_KEYS, 2 * TQ), F32),
        compiler_params=pltpu.CompilerParams(dimension_semantics=("arbitrary",),
                                             vmem_limit_bytes=VMEM_LIMIT),
        name="ck_bias",
    )(tab)


def _out_ffn_kernel(x_ref, oa_ref, ob_ref, woa_ref, wob_ref, g1_ref, b1_ref,
                    wg_ref, wu_ref, wd_ref, g2_ref, b2_ref, y_ref):
    mix = _dot(oa_ref[...], woa_ref[...]) + _dot(ob_ref[...], wob_ref[...])
    x1 = _layer_norm(DEEPNORM_ALPHA * x_ref[...] + mix, g1_ref[...], b1_ref[...])
    x1b = x1.astype(BF16)
    gate = _dot(x1b, wg_ref[...])
    up = _dot(x1b, wu_ref[...])
    hidden = (gate * jax.nn.sigmoid(gate) * up).astype(BF16)
    ffn = _dot(hidden, wd_ref[...])
    y_ref[...] = _layer_norm(DEEPNORM_ALPHA * x1 + ffn, g2_ref[...], b2_ref[...])


def _out_ffn(x2, o_a, o_b, w_out_a, w_out_b, g1, b1, w_gate, w_up, w_down, g2, b2):
    t = x2.shape[0]
    d_ff = w_gate.shape[1]
    row = lambda w: pl.BlockSpec((TM_FFN, w), lambda i: (i, 0))
    vec = _const_spec((1, D_MODEL))
    return pl.pallas_call(
        _out_ffn_kernel,
        grid=(t // TM_FFN,),
        in_specs=[row(D_MODEL), row(512), row(512), _const_spec((512, D_MODEL)),
                  _const_spec((512, D_MODEL)), vec, vec, _const_spec((D_MODEL, d_ff)),
                  _const_spec((D_MODEL, d_ff)), _const_spec((d_ff, D_MODEL)), vec, vec],
        out_specs=row(D_MODEL),
        out_shape=jax.ShapeDtypeStruct((t, D_MODEL), F32),
        compiler_params=pltpu.CompilerParams(dimension_semantics=("arbitrary",),
                                             vmem_limit_bytes=VMEM_LIMIT),
        name="out_ffn",
    )(x2, o_a, o_b, w_out_a.astype(BF16), w_out_b.astype(BF16), g1.reshape(1, -1), b1.reshape(1, -1),
      w_gate.astype(BF16), w_up.astype(BF16), w_down.astype(BF16), g2.reshape(1, -1), b2.reshape(1, -1))


def kernel(x, ab_w_in, ab_q_norm, ab_w_uq, ab_kv_norm, ab_w_ukv, ab_sinks, ab_w_out,
           cd_w_in, cd_b_forget, cd_rel_bias, cd_w_out,
           ln1_g, ln1_b, ffn_w_gate, ffn_w_up, ffn_w_down, ln2_g, ln2_b):
    b, s, d = x.shape
    assert d == D_MODEL and TQ == TK
    assert s % TM == 0 and s % (2 * FULL_TILE) == 0 and s % (TQ * BAND_SUB_BLOCKS) == 0
    t = b * s
    x2 = x.reshape(t, d)

    qa, ka, va, qs, ks, vs = _proj_ab(x2, ab_w_in[0], ab_q_norm[0], ab_w_uq[0], ab_kv_norm[0],
                                      ab_w_ukv[0], s)
    r3 = lambda a: a.reshape(b, s, a.shape[-1])
    o_a = _attention("mla", r3(qa), r3(ka), r3(va), 0, 0, 0)
    o_b = _attention("swa", r3(qs), r3(ks), r3(vs), 0, 0, 0, extra=_swa_bias(),
                     sinks=ab_sinks[0] * LOG2E)
    w_out = ab_w_out[0]
    w_out_b = w_out[512:].reshape(2, 4, HEAD_DIM, D_MODEL).transpose(1, 0, 2, 3).reshape(512, D_MODEL)
    x2 = _out_ffn(x2, o_a.reshape(t, 512), o_b.reshape(t, 512), w_out[:512], w_out_b,
                  ln1_g[0], ln1_b[0], ffn_w_gate[0], ffn_w_up[0], ffn_w_down[0], ln2_g[0], ln2_b[0])

    qkv, cum = _proj_cd(x2, cd_w_in[0], cd_b_forget[0], s)
    qkv = qkv.reshape(b, s, 3072)
    o_c = _attention("fox", qkv, qkv, qkv, 0, 4, 8, extra=cum.reshape(b, s, LANES))
    o_d = _attention("ck", qkv, qkv, qkv, 12, 16, 20, extra=_ck_bias(cd_rel_bias[0]))
    w_out = cd_w_out[0]
    x2 = _out_ffn(x2, o_c.reshape(t, 512), o_d.reshape(t, 512), w_out[:512], w_out[512:],
                  ln1_g[1], ln1_b[1], ffn_w_gate[1], ffn_w_up[1], ffn_w_down[1], ln2_g[1], ln2_b[1])
    return x2.reshape(b, s, d)
```

```python
import functools
import math

import numpy as np
import jax
import jax.numpy as jnp
from jax import lax
from jax.experimental import pallas as pl
from jax.experimental.pallas import tpu as pltpu

D_MODEL = 1024
CHUNK = 64
HEAD_DIM = 64
N_HEADS = 8
LANES = 128
LN_EPS = 1e-5
RMS_EPS = 1e-6
MASK = -1e30
LOG2E = math.log2(math.e)

MLA_Q_RANK = 384
MLA_KV_RANK = 256
MLA_NOPE = 64
MLA_ROPE = 32
ROPE_THETA = 10000.0
SWA_KV_HEADS = 2
SWA_LEFT_CHUNKS = 2
CK_LEFT_CHUNKS = 8
REL_MAX = 256
DEPTH = 2
DEEPNORM_ALPHA = (2 * DEPTH) ** 0.25

TQ = 256
TK = 256
BAND_SUB_BLOCKS = 16
BAND_GROUP = 2
SWA_WINDOW_KEYS = SWA_LEFT_CHUNKS * CHUNK + TQ
CK_WINDOW_KEYS = CK_LEFT_CHUNKS * CHUNK + TQ
FULL_TILE = 512
TM = 1024
TM_FFN = 512
CUMSUM_ROWS = 128
VT_ROWS = LANES + 16
VMEM_LIMIT = 56 * 1024 * 1024

BF16 = jnp.bfloat16
F32 = jnp.float32


def _dot(a, b):
    return jnp.dot(a, b, preferred_element_type=F32)


def _dot_nt(a, b):
    return lax.dot_general(a, b, (((1,), (1,)), ((), ())), preferred_element_type=F32)


def _layer_norm(y, g, b):
    mu = jnp.mean(y, axis=-1, keepdims=True)
    yc = y - mu
    var = jnp.mean(yc * yc, axis=-1, keepdims=True)
    return yc * lax.rsqrt(var + LN_EPS) * g + b


def _rms_norm(c, g):
    return c * lax.rsqrt(jnp.mean(c * c, axis=-1, keepdims=True) + RMS_EPS) * g


def _const_spec(shape):
    zeros = (0,) * len(shape)
    return pl.BlockSpec(shape, lambda *_: zeros, pipeline_mode=pl.Buffered(1))


_AB_CQ, _AB_CKV, _AB_QS, _AB_KS, _AB_VS, _AB_KR, _AB_COLS = 0, 384, 640, 1152, 1280, 1408, 1536


def _rope_mix(t, m1, m2):
    return t * m1 + pltpu.roll(t, 96, 1) * m2


def _proj_ab_kernel(x_ref, win_ref, qn_ref, wuq_ref, kvn_ref, wukv_ref,
                    m1q_ref, m2q_ref, m1k_ref, m2k_ref,
                    qa_ref, ka_ref, va_ref, qs_ref, ks_ref, vs_ref):
    proj = _dot(x_ref[...].astype(BF16), win_ref[...])
    c_q = _rms_norm(proj[:, _AB_CQ:_AB_CKV], qn_ref[...])
    qf = _dot(c_q.astype(BF16), wuq_ref[...])
    c_kv = _rms_norm(proj[:, _AB_CKV:_AB_QS], kvn_ref[...])
    kvf = _dot(c_kv.astype(BF16), wukv_ref[...])
    k_rope = _rope_mix(proj[:, _AB_KR:_AB_COLS], m1k_ref[...], m2k_ref[...])
    m1q, m2q = m1q_ref[...], m2q_ref[...]
    for h in range(N_HEADS):
        sl = slice(h * LANES, (h + 1) * LANES)
        qa_ref[:, sl] = _rope_mix(qf[:, sl], m1q, m2q).astype(BF16)
        ka_ref[:, sl] = (kvf[:, sl] + k_rope).astype(BF16)
    va_ref[...] = kvf[:, N_HEADS * LANES:].astype(BF16)
    qs_ref[...] = (proj[:, _AB_QS:_AB_KS] * (HEAD_DIM ** -0.5 * LOG2E)).astype(BF16)
    ks_ref[...] = proj[:, _AB_KS:_AB_VS].astype(BF16)
    vs_ref[...] = proj[:, _AB_VS:_AB_KR].astype(BF16)


def _proj_ab(x2, w_in, q_norm, w_uq, kv_norm, w_ukv, seq):
    t = x2.shape[0]
    nsb = seq // TM
    c_q, c_kv, k_r, q_s, k_s, v_s = jnp.split(w_in, [384, 640, 672, 1184, 1312], axis=1)
    q_s = q_s.reshape(D_MODEL, 2, 4, HEAD_DIM).transpose(0, 2, 1, 3).reshape(D_MODEL, 512)
    kr_blk = jnp.concatenate([jnp.zeros((D_MODEL, 64), F32), k_r, k_r[:, 16:], k_r[:, :16]], axis=1)
    win_p = jnp.concatenate([c_q, c_kv, q_s, k_s, v_s, kr_blk], axis=1).astype(BF16)
    wq = w_uq.reshape(MLA_Q_RANK, N_HEADS, MLA_NOPE + MLA_ROPE)
    wuq_p = jnp.concatenate([wq, wq[..., 80:96], wq[..., 64:80]], axis=-1)
    wuq_p = wuq_p.reshape(MLA_Q_RANK, N_HEADS * LANES).astype(BF16)
    wkv = w_ukv.reshape(MLA_KV_RANK, N_HEADS, 2 * HEAD_DIM)
    k_pad = jnp.concatenate([wkv[..., :64], jnp.zeros_like(wkv[..., :64])], axis=-1)
    wukv_p = jnp.concatenate([k_pad.reshape(MLA_KV_RANK, N_HEADS * LANES),
                              wkv[..., 64:].reshape(MLA_KV_RANK, 512)], axis=1).astype(BF16)
    inv = ROPE_THETA ** (-jnp.arange(0, MLA_ROPE, 2, dtype=F32) / MLA_ROPE)
    ang = jnp.arange(seq, dtype=F32)[:, None] * inv[None, :]
    cos, sin = jnp.cos(ang), jnp.sin(ang)
    z64, z32, o64 = jnp.zeros((seq, 64), F32), jnp.zeros((seq, 32), F32), jnp.ones((seq, 64), F32)
    m1k = jnp.concatenate([z64, cos, cos, z32], axis=1)
    m2k = jnp.concatenate([z64, -sin, sin, z32], axis=1)
    q_scale = (MLA_NOPE + MLA_ROPE) ** -0.5 * LOG2E
    m1q = jnp.concatenate([o64, cos, cos, z32], axis=1) * q_scale
    m2q = m2k * q_scale

    row = lambda w: pl.BlockSpec((TM, w), lambda i: (i, 0))
    tab = pl.BlockSpec((TM, LANES), lambda i: (i % nsb, 0))
    out_w = (1024, 1024, 512, 512, 128, 128)
    return pl.pallas_call(
        _proj_ab_kernel,
        grid=(t // TM,),
        in_specs=[row(D_MODEL), _const_spec((D_MODEL, _AB_COLS)), _const_spec((1, MLA_Q_RANK)),
                  _const_spec((MLA_Q_RANK, 1024)), _const_spec((1, MLA_KV_RANK)),
                  _const_spec((MLA_KV_RANK, 1536)), tab, tab, tab, tab],
        out_specs=[row(w) for w in out_w],
        out_shape=[jax.ShapeDtypeStruct((t, w), BF16) for w in out_w],
        compiler_params=pltpu.CompilerParams(dimension_semantics=("arbitrary",),
                                             vmem_limit_bytes=VMEM_LIMIT),
        name="proj_ab",
    )(x2, win_p, q_norm.reshape(1, -1), wuq_p, kv_norm.reshape(1, -1), wukv_p, m1q, m2q, m1k, m2k)


def _proj_cd_kernel(x_ref, win_ref, bf_ref, qkv_ref, cum_ref, carry_ref, *, nsb):
    @pl.when(pl.program_id(0) % nsb == 0)
    def _():
        carry_ref[...] = jnp.zeros_like(carry_ref)

    proj = _dot(x_ref[...].astype(BF16), win_ref[...])
    q_scale = HEAD_DIM ** -0.5 * LOG2E
    for blk, scale in enumerate((q_scale, None, None, q_scale, None, None)):
        part = proj[:, blk * 512:(blk + 1) * 512]
        qkv_ref[:, blk * 512:(blk + 1) * 512] = (part if scale is None else part * scale).astype(BF16)
    z = proj[:, 3072:] + bf_ref[...]
    log_f = jnp.minimum(z, 0.0) - jnp.log(1.0 + jnp.exp(-jnp.abs(z)))
    r = lax.broadcasted_iota(jnp.int32, (CUMSUM_ROWS, CUMSUM_ROWS), 0)
    c = lax.broadcasted_iota(jnp.int32, (CUMSUM_ROWS, CUMSUM_ROWS), 1)
    tri = jnp.where(c <= r, 1.0, 0.0).astype(BF16)
    hi = log_f.astype(BF16)
    rem = log_f - hi.astype(F32)
    mid = rem.astype(BF16)
    lo = (rem - mid.astype(F32)).astype(BF16)
    carry = carry_ref[...]
    for blk in range(TM // CUMSUM_ROWS):
        rows = slice(blk * CUMSUM_ROWS, (blk + 1) * CUMSUM_ROWS)
        cum = _dot(tri, hi[rows]) + _dot(tri, mid[rows]) + _dot(tri, lo[rows]) + carry
        cum_ref[rows, :] = cum
        carry = cum[CUMSUM_ROWS - 1:CUMSUM_ROWS, :]
    carry_ref[...] = carry


def _proj_cd(x2, w_in, b_forget, seq):
    t = x2.shape[0]
    q_f, k_f, v_f, f_l, q_c, k_c, v_c = jnp.split(w_in, [512, 1024, 1536, 1544, 2056, 2568], axis=1)
    f_pad = jnp.concatenate([f_l, jnp.zeros((D_MODEL, LANES - N_HEADS), F32)], axis=1)
    win_p = jnp.concatenate([q_f, k_f, v_f, q_c, k_c, v_c, f_pad], axis=1).astype(BF16)
    bf = jnp.concatenate([b_forget, jnp.zeros((LANES - N_HEADS,), F32)]).reshape(1, LANES)
    return pl.pallas_call(
        functools.partial(_proj_cd_kernel, nsb=seq // TM),
        grid=(t // TM,),
        in_specs=[pl.BlockSpec((TM, D_MODEL), lambda i: (i, 0)),
                  _const_spec((D_MODEL, 3072 + LANES)), _const_spec((1, LANES))],
        out_specs=[pl.BlockSpec((TM, 3072), lambda i: (i, 0)),
                   pl.BlockSpec((TM, LANES), lambda i: (i, 0))],
        out_shape=[jax.ShapeDtypeStruct((t, 3072), BF16), jax.ShapeDtypeStruct((t, LANES), F32)],
        scratch_shapes=[pltpu.VMEM((1, LANES), F32)],
        compiler_params=pltpu.CompilerParams(dimension_semantics=("arbitrary",),
                                             vmem_limit_bytes=VMEM_LIMIT),
        name="proj_cd",
    )(x2, win_p, bf)


def _attn_kernel(*refs, variant, window, seq, tq, tk, n_sub):
    if variant == "mla":
        q_ref, k_ref, v_ref, qn_ref, o_ref, vt_ref, m_ref, acc_ref, sa_ref, sb_ref = refs
    elif variant == "fox":
        (q_ref, k_ref, v_ref, qn_ref, ck_ref, o_ref, vt_ref, m_ref, acc_ref, sa_ref, sb_ref,
         ckrep_ref) = refs
    elif variant == "swa":
        sink_ref, q_ref, k_ref, v_ref, bias_ref, o_ref, vt_ref, kpad_ref, sbuf_ref = refs
    else:
        q_ref, k_ref, v_ref, bias_ref, o_ref, vt_ref, kpad_ref, sbuf_ref = refs

    i = pl.program_id(2)
    banded = window is not None
    pad = window - tq if banded else 0

    def setup():
        for c in range(seq // 512):
            sl = slice(pad + c * 512, pad + (c + 1) * 512)
            vt_ref[0:LANES, sl] = v_ref[0, c * 512:(c + 1) * 512, :].astype(F32).T.astype(BF16)
        vt_ref[LANES:VT_ROWS, :] = jnp.ones((VT_ROWS - LANES, pad + seq), BF16)
        if banded:
            vt_ref[0:LANES, 0:pad] = jnp.zeros((LANES, pad), BF16)
            kpad_ref[0:pad, :] = jnp.zeros((pad, LANES), BF16)
            kpad_ref[pad:pad + seq, :] = k_ref[0]
        if variant == "fox":
            src = lax.broadcasted_iota(jnp.int32, (LANES, 2 * LANES), 0)
            dst = lax.broadcasted_iota(jnp.int32, (LANES, 2 * LANES), 1)
            pick = jnp.where(src == 2 * pl.program_id(0) + (dst >> 7), 1.0, 0.0).astype(BF16)
            for c in range(seq // 512):
                rows = slice(c * 512, (c + 1) * 512)
                cum = ck_ref[0, rows, :]
                hi = cum.astype(BF16)
                rem = cum - hi.astype(F32)
                mid = rem.astype(BF16)
                lo = (rem - mid.astype(F32)).astype(BF16)
                ckrep_ref[rows, :] = (_dot(hi, pick) + _dot(mid, pick) + _dot(lo, pick)) * LOG2E

    sub = lax.broadcasted_iota(jnp.int32, (LANES, tq), 0)
    lane = lax.broadcasted_iota(jnp.int32, (tq, LANES), 1)

    def pair_rows(q_blk):
        zero = jnp.zeros_like(q_blk)
        return jnp.concatenate([jnp.where(lane < HEAD_DIM, q_blk, zero),
                                jnp.where(lane >= HEAD_DIM, q_blk, zero)], axis=0)

    def finish(acc, m, rows):
        l = acc[LANES:LANES + 1, :]
        o_t = acc[:LANES, :]
        if variant == "swa":
            pair = pl.program_id(0)
            col = lax.broadcasted_iota(jnp.int32, (1, 2 * tq), 1)
            sink = jnp.where(col < tq, sink_ref[pair], sink_ref[pair + N_HEADS // 2])
            m_fin = jnp.maximum(m, sink)
            scale = jnp.exp2(m - m_fin)
            l = l * scale + jnp.exp2(sink - m_fin)
            o_t = o_t * scale
        o_t = o_t / l
        o_ref[0, rows, :] = jnp.where(sub < HEAD_DIM, o_t[:, :tq], o_t[:, tq:]).T.astype(o_ref.dtype)

    if banded:
        pl.when(i == 0)(setup)
        n_var = bias_ref.shape[1]

        def band_scores(sb, s_ref):
            ib = i * n_sub + sb
            start = pl.multiple_of(ib * tq, tq)
            rows = pl.ds(pl.multiple_of(sb * tq, tq), tq)
            v = jnp.maximum(n_var - 1 - ib, 0)
            s_ref[...] = (_dot_nt(kpad_ref[pl.ds(start, window), :], pair_rows(q_ref[0, rows, :]))
                          + bias_ref[0, v])

        def band_softmax(sb, s_ref):
            start = pl.multiple_of((i * n_sub + sb) * tq, tq)
            s = s_ref[...]
            m = jnp.max(s, axis=0, keepdims=True)
            p = jnp.exp2(s - m).astype(BF16)
            finish(_dot(vt_ref[:, pl.ds(start, window)], p), m,
                   pl.ds(pl.multiple_of(sb * tq, tq), tq))

        group = BAND_GROUP
        n_groups = n_sub // group

        def group_scores(g, parity):
            for r in range(group):
                band_scores(g * group + r, sbuf_ref.at[parity * group + r])

        def group_softmax(g, parity):
            for r in range(group):
                band_softmax(g * group + r, sbuf_ref.at[parity * group + r])

        group_scores(0, 0)

        def two_groups(u, carry):
            g = 2 * u
            group_scores(g + 1, 1)
            group_softmax(g, 0)
            group_scores(g + 2, 0)
            group_softmax(g + 1, 1)
            return carry

        lax.fori_loop(0, n_groups // 2 - 1, two_groups, None)
        group_scores(n_groups - 1, 1)
        group_softmax(n_groups - 2, 0)
        group_softmax(n_groups - 1, 1)
        return

    if variant == "mla":
        query_operands = lambda ref: (ref[0, :, :LANES], ref[0, :, LANES:])
    else:
        query_operands = lambda ref: pair_rows(ref[0])

    m_ref[...] = jnp.full_like(m_ref, MASK)
    acc_ref[...] = jnp.zeros_like(acc_ref)

    first_half = [(h * tq, tk) for h in range(2)]
    second_half = [(h * tq + tk, tk) for h in range(2)]

    q_now = query_operands(q_ref)

    def scores(j, s_ref, slabs=((0, 2 * tq),), queries=q_now):
        start = pl.multiple_of(j * tk, tk)
        k = k_ref[0, pl.ds(start, tk), :]
        for c0, w in slabs:
            cols = slice(c0, c0 + w)
            if variant == "mla":
                for h in range(2):
                    lo, hi = max(c0, h * tq), min(c0 + w, (h + 1) * tq)
                    if lo < hi:
                        q_h = queries[h][lo - h * tq:hi - h * tq, :]
                        s_ref[:, lo:hi] = _dot_nt(k[:, h * LANES:(h + 1) * LANES], q_h)
                continue
            s = _dot_nt(k, queries[cols, :])
            if variant == "fox":
                ck_reps = [ckrep_ref[pl.ds(start, tk), (c // tq) * LANES:(c // tq + 1) * LANES]
                           for c in range(c0, c0 + w, LANES)]
                s = s - jnp.concatenate(ck_reps, axis=1)
            s_ref[:, cols] = s

    def consume(j, s_ref, slabs, masked=()):
        start = pl.multiple_of(j * tk, tk)
        v_t = vt_ref[:, pl.ds(start, tk)]
        for c0, w in slabs:
            cols = slice(c0, c0 + w)
            s = s_ref[:, cols]
            if (c0, w) in masked:
                key = lax.broadcasted_iota(jnp.int32, (tk, w), 0)
                qry = lax.broadcasted_iota(jnp.int32, (tk, w), 1)
                if variant == "mla":
                    keep = key <= (qry | (CHUNK - 1))
                else:
                    keep = key <= qry
                s = jnp.where(keep, s, MASK)
            m_prev = m_ref[:, cols]
            m_new = jnp.maximum(m_prev, jnp.max(s, axis=0, keepdims=True))
            alpha = jnp.exp2(m_prev - m_new)
            p = jnp.exp2(s - m_new).astype(BF16)
            acc_ref[:, cols] = alpha * acc_ref[:, cols] + _dot(v_t, p)
            m_ref[:, cols] = m_new

    everything = [(0, 2 * tq)]

    @pl.when(i == 0)
    def _():
        setup()
        scores(0, sa_ref)

    def two_blocks(u, carry):
        t = 2 * u
        scores(t + 1, sb_ref)
        consume(t, sa_ref, everything)
        scores(t + 2, sa_ref)
        consume(t + 1, sb_ref, everything)
        return carry

    lax.fori_loop(0, i, two_blocks, None)

    def diagonal_blocks(prefetch_next):
        q_next = query_operands(qn_ref) if prefetch_next else None
        scores(2 * i + 1, sb_ref, second_half)
        consume(2 * i, sa_ref, first_half, masked=first_half)
        if prefetch_next:
            scores(0, sa_ref, first_half, queries=q_next)
        consume(2 * i, sa_ref, second_half)
        if prefetch_next:
            scores(0, sa_ref, second_half, queries=q_next)
        consume(2 * i + 1, sb_ref, second_half, masked=second_half)

    last = pl.num_programs(2) - 1
    pl.when(i < last)(lambda: diagonal_blocks(True))
    pl.when(i == last)(lambda: diagonal_blocks(False))
    finish(acc_ref[...], m_ref[...], slice(0, tq))


def _attention(variant, q, k, v, q_col, k_col, v_col, *, extra=None, sinks=None):
    b, s, _ = q.shape
    n_pairs = N_HEADS // 2
    if variant in ("mla", "fox"):
        tq, tk, n_sub = 2 * FULL_TILE, FULL_TILE, 1
    else:
        tq, tk, n_sub = TQ, TK, BAND_SUB_BLOCKS
    nq = s // (tq * n_sub)
    window = {"mla": None, "fox": None, "swa": SWA_WINDOW_KEYS, "ck": CK_WINDOW_KEYS}[variant]
    qw = 2 * LANES if variant == "mla" else LANES
    shared_kv = variant == "swa"

    def im(f):
        if sinks is None:
            return f
        return lambda p, bi, i, sink_ref: f(p, bi, i)

    in_specs = [
        pl.BlockSpec((1, tq * n_sub, qw), im(lambda p, bi, i: (bi, i, q_col + p))),
        pl.BlockSpec((1, s, qw), im(lambda p, bi, i: (bi, 0, k_col + (0 if shared_kv else p)))),
        pl.BlockSpec((1, s, LANES), im(lambda p, bi, i: (bi, 0, v_col + (0 if shared_kv else p)))),
    ]
    args = [q, k, v]
    if window is None:
        in_specs.append(pl.BlockSpec((1, tq, qw), lambda p, bi, i: (bi, jnp.minimum(i + 1, nq - 1),
                                                                     q_col + p)))
        args.append(q)
        scratch = [pltpu.VMEM((VT_ROWS, s), BF16), pltpu.VMEM((1, 2 * tq), F32),
                   pltpu.VMEM((VT_ROWS, 2 * tq), F32),
                   pltpu.VMEM((tk, 2 * tq), F32), pltpu.VMEM((tk, 2 * tq), F32)]
    else:
        pad = window - tq
        scratch = [pltpu.VMEM((VT_ROWS, pad + s), BF16), pltpu.VMEM((pad + s, LANES), BF16),
                   pltpu.VMEM((2 * BAND_GROUP, window, 2 * tq), F32)]
        in_specs.append(pl.BlockSpec((1,) + extra.shape[1:], im(lambda p, bi, i: (p, 0, 0, 0))))
        args.append(extra)
    if variant == "fox":
        in_specs.append(pl.BlockSpec((1, s, LANES), lambda p, bi, i: (bi, 0, 0)))
        args.append(extra)
        scratch.append(pltpu.VMEM((s, 2 * LANES), F32))
    out_spec = pl.BlockSpec((1, tq * n_sub, LANES), im(lambda p, bi, i: (bi, i, p)))
    kern = functools.partial(_attn_kernel, variant=variant, window=window, seq=s, tq=tq, tk=tk,
                             n_sub=n_sub)
    params = pltpu.CompilerParams(dimension_semantics=("arbitrary",) * 3, vmem_limit_bytes=VMEM_LIMIT)
    out_shape = jax.ShapeDtypeStruct((b, s, n_pairs * LANES), BF16)
    if sinks is not None:
        grid_spec = pltpu.PrefetchScalarGridSpec(
            num_scalar_prefetch=1, grid=(n_pairs, b, nq), in_specs=in_specs, out_specs=out_spec,
            scratch_shapes=scratch)
        return pl.pallas_call(kern, grid_spec=grid_spec, out_shape=out_shape,
                              compiler_params=params, name="attn_" + variant)(sinks, *args)
    return pl.pallas_call(kern, grid=(n_pairs, b, nq), in_specs=in_specs, out_specs=out_spec,
                          out_shape=out_shape, scratch_shapes=scratch,
                          compiler_params=params, name="attn_" + variant)(*args)


def _lead_rows(window, n_var, v):
    return max(window - TQ - (n_var - 1 - v) * TQ, 0)


def _swa_bias():
    window, left = SWA_WINDOW_KEYS, SWA_LEFT_CHUNKS
    c = jnp.arange(window)[:, None]
    a = jnp.arange(TQ)[None, :]
    dist = a - c + (window - TQ)
    ahead = (a // CHUNK + left) - c // CHUNK
    slopes = jnp.exp2(-8.0 * jnp.arange(1, N_HEADS + 1, dtype=F32) / N_HEADS)
    bias = -slopes[:, None, None] * jnp.abs(dist).astype(F32)[None] * LOG2E
    bias = jnp.where(((ahead >= 0) & (ahead <= left))[None], bias, MASK)
    bias = jnp.concatenate([bias[:4], bias[4:]], axis=2)
    n_var = 2
    return jnp.stack([jnp.where(c[None] >= _lead_rows(window, n_var, v), bias, MASK)
                      for v in range(n_var)], axis=1)


_REL_ROWS = 384


_CK_VARIANTS = 3


def _ck_bias_kernel(tab_ref, out_ref):
    window = CK_WINDOW_KEYS
    span = TQ + window
    tab = tab_ref[0] * LOG2E
    hi = tab.astype(BF16)
    rem = tab - hi.astype(F32)
    mid = rem.astype(BF16)
    lo = (rem - mid.astype(F32)).astype(BF16)
    t = lax.broadcasted_iota(jnp.int32, (_REL_ROWS, span), 0)
    n = lax.broadcasted_iota(jnp.int32, (_REL_ROWS, span), 1)
    delta = jnp.where(n < TQ, n, n - span)
    idx = jnp.clip(delta + (window - TQ), -(CHUNK - 1), REL_MAX) + (CHUNK - 1)
    onehot = jnp.where(t == idx, 1.0, 0.0).astype(BF16)
    base = _dot(hi, onehot) + _dot(mid, onehot) + _dot(lo, onehot)
    c = lax.broadcasted_iota(jnp.int32, (window, TQ), 0)
    a = lax.broadcasted_iota(jnp.int32, (window, TQ), 1)
    valid = ((a >> 6) + CK_LEFT_CHUNKS - (c >> 6)).astype(jnp.uint32) <= CK_LEFT_CHUNKS
    for h in range(2):
        rows = jnp.broadcast_to(base[h:h + 1, :], (window, span))
        tile = jnp.where(valid, pltpu.roll(rows, 0, 1, stride=1, stride_axis=0)[:, :TQ], MASK)
        for v in range(_CK_VARIANTS):
            lead = _lead_rows(window, _CK_VARIANTS, v)
            out_ref[0, v, :, h * TQ:(h + 1) * TQ] = jnp.where(c >= lead, tile, MASK)


def _ck_bias(rel_bias):
    tab = jnp.pad(rel_bias.T, ((0, 0), (0, _REL_ROWS - rel_bias.shape[0])))
    tab = jnp.pad(tab.reshape(N_HEADS // 2, 2, _REL_ROWS), ((0, 0), (0, 6), (0, 0)))
    return pl.pallas_call(
        _ck_bias_kernel,
        grid=(N_HEADS // 2,),
        in_specs=[pl.BlockSpec((1, 8, _REL_ROWS), lambda p: (p, 0, 0))],
        out_specs=pl.BlockSpec((1, _CK_VARIANTS, CK_WINDOW_KEYS, 2 * TQ), lambda p: (p, 0, 0, 0)),
        out_shape=jax.ShapeDtypeStruct((N_HEADS // 2, _CK_VARIANTS, CK_WINDOW_KEYS, 2 * TQ), F32),
        compiler_params=pltpu.CompilerParams(dimension_semantics=("arbitrary",),
                                             vmem_limit_bytes=VMEM_LIMIT),
        name="ck_bias",
    )(tab)


def _out_ffn_kernel(x_ref, oa_ref, ob_ref, woa_ref, wob_ref, g1_ref, b1_ref,
                    wg_ref, wu_ref, wd_ref, g2_ref, b2_ref, y_ref):
    mix = _dot(oa_ref[...], woa_ref[...]) + _dot(ob_ref[...], wob_ref[...])
    x1 = _layer_norm(DEEPNORM_ALPHA * x_ref[...] + mix, g1_ref[...], b1_ref[...])
    x1b = x1.astype(BF16)
    gate = _dot(x1b, wg_ref[...])
    up = _dot(x1b, wu_ref[...])
    hidden = (gate * jax.nn.sigmoid(gate) * up).astype(BF16)
    ffn = _dot(hidden, wd_ref[...])
    y_ref[...] = _layer_norm(DEEPNORM_ALPHA * x1 + ffn, g2_ref[...], b2_ref[...])


def _out_ffn(x2, o_a, o_b, w_out_a, w_out_b, g1, b1, w_gate, w_up, w_down, g2, b2):
    t = x2.shape[0]
    d_ff = w_gate.shape[1]
    row = lambda w: pl.BlockSpec((TM_FFN, w), lambda i: (i, 0))
    vec = _const_spec((1, D_MODEL))
    return pl.pallas_call(
        _out_ffn_kernel,
        grid=(t // TM_FFN,),
        in_specs=[row(D_MODEL), row(512), row(512), _const_spec((512, D_MODEL)),
                  _const_spec((512, D_MODEL)), vec, vec, _const_spec((D_MODEL, d_ff)),
                  _const_spec((D_MODEL, d_ff)), _const_spec((d_ff, D_MODEL)), vec, vec],
        out_specs=row(D_MODEL),
        out_shape=jax.ShapeDtypeStruct((t, D_MODEL), F32),
        compiler_params=pltpu.CompilerParams(dimension_semantics=("arbitrary",),
                                             vmem_limit_bytes=VMEM_LIMIT),
        name="out_ffn",
    )(x2, o_a, o_b, w_out_a.astype(BF16), w_out_b.astype(BF16), g1.reshape(1, -1), b1.reshape(1, -1),
      w_gate.astype(BF16), w_up.astype(BF16), w_down.astype(BF16), g2.reshape(1, -1), b2.reshape(1, -1))


def kernel(x, ab_w_in, ab_q_norm, ab_w_uq, ab_kv_norm, ab_w_ukv, ab_sinks, ab_w_out,
           cd_w_in, cd_b_forget, cd_rel_bias, cd_w_out,
           ln1_g, ln1_b, ffn_w_gate, ffn_w_up, ffn_w_down, ln2_g, ln2_b):
    b, s, d = x.shape
    assert d == D_MODEL and TQ == TK
    assert s % TM == 0 and s % (2 * FULL_TILE) == 0 and s % (TQ * BAND_SUB_BLOCKS) == 0
    t = b * s
    x2 = x.reshape(t, d)

    qa, ka, va, qs, ks, vs = _proj_ab(x2, ab_w_in[0], ab_q_norm[0], ab_w_uq[0], ab_kv_norm[0],
                                      ab_w_ukv[0], s)
    r3 = lambda a: a.reshape(b, s, a.shape[-1])
    o_a = _attention("mla", r3(qa), r3(ka), r3(va), 0, 0, 0)
    o_b = _attention("swa", r3(qs), r3(ks), r3(vs), 0, 0, 0, extra=_swa_bias(),
                     sinks=ab_sinks[0] * LOG2E)
    w_out = ab_w_out[0]
    w_out_b = w_out[512:].reshape(2, 4, HEAD_DIM, D_MODEL).transpose(1, 0, 2, 3).reshape(512, D_MODEL)
    x2 = _out_ffn(x2, o_a.reshape(t, 512), o_b.reshape(t, 512), w_out[:512], w_out_b,
                  ln1_g[0], ln1_b[0], ffn_w_gate[0], ffn_w_up[0], ffn_w_down[0], ln2_g[0], ln2_b[0])

    qkv, cum = _proj_cd(x2, cd_w_in[0], cd_b_forget[0], s)
    qkv = qkv.reshape(b, s, 3072)
    o_c = _attention("fox", qkv, qkv, qkv, 0, 4, 8, extra=cum.reshape(b, s, LANES))
    o_d = _attention("ck", qkv, qkv, qkv, 12, 16, 20, extra=_ck_bias(cd_rel_bias[0]))
    w_out = cd_w_out[0]
    x2 = _out_ffn(x2, o_c.reshape(t, 512), o_d.reshape(t, 512), w_out[:512], w_out[512:],
                  ln1_g[1], ln1_b[1], ffn_w_gate[1], ffn_w_up[1], ffn_w_down[1], ln2_g[1], ln2_b[1])
    return x2.reshape(b, s, d)
```

```python
import functools
import math

import numpy as np
import jax
import jax.numpy as jnp
from jax import lax
from jax.experimental import pallas as pl
from jax.experimental.pallas import tpu as pltpu

D_MODEL = 1024
CHUNK = 64
HEAD_DIM = 64
N_HEADS = 8
LANES = 128
LN_EPS = 1e-5
RMS_EPS = 1e-6
MASK = -1e30
LOG2E = math.log2(math.e)

MLA_Q_RANK = 384
MLA_KV_RANK = 256
MLA_NOPE = 64
MLA_ROPE = 32
ROPE_THETA = 10000.0
SWA_KV_HEADS = 2
SWA_LEFT_CHUNKS = 2
CK_LEFT_CHUNKS = 8
REL_MAX = 256
DEPTH = 2
DEEPNORM_ALPHA = (2 * DEPTH) ** 0.25

TQ = 256
TK = 256
BAND_SUB_BLOCKS = 16
BAND_GROUP = 2
SWA_WINDOW_KEYS = SWA_LEFT_CHUNKS * CHUNK + TQ
CK_WINDOW_KEYS = CK_LEFT_CHUNKS * CHUNK + TQ
FULL_TILE = 512
TM = 1024
TM_FFN = 512
CUMSUM_ROWS = 128
VT_ROWS = LANES + 16
VMEM_LIMIT = 56 * 1024 * 1024

BF16 = jnp.bfloat16
F32 = jnp.float32


def _dot(a, b):
    return jnp.dot(a, b, preferred_element_type=F32)


def _dot_nt(a, b):
    return lax.dot_general(a, b, (((1,), (1,)), ((), ())), preferred_element_type=F32)


def _layer_norm(y, g, b):
    mu = jnp.mean(y, axis=-1, keepdims=True)
    yc = y - mu
    var = jnp.mean(yc * yc, axis=-1, keepdims=True)
    return yc * lax.rsqrt(var + LN_EPS) * g + b


def _rms_norm(c, g):
    return c * lax.rsqrt(jnp.mean(c * c, axis=-1, keepdims=True) + RMS_EPS) * g


def _block_major_spec(n_blocks, steps_per_seq):
    return pl.BlockSpec((1, n_blocks, TM, LANES),
                        lambda i: (i // steps_per_seq, 0, i % steps_per_seq, 0))


def _const_spec(shape):
    zeros = (0,) * len(shape)
    return pl.BlockSpec(shape, lambda *_: zeros, pipeline_mode=pl.Buffered(1))


_AB_CQ, _AB_CKV, _AB_QS, _AB_KS, _AB_VS, _AB_KR, _AB_COLS = 0, 384, 640, 1152, 1280, 1408, 1536


def _rope_mix(t, m1, m2):
    return t * m1 + pltpu.roll(t, 96, 1) * m2


def _proj_ab_kernel(x_ref, win_ref, qn_ref, wuq_ref, kvn_ref, wukv_ref,
                    m1q_ref, m2q_ref, m1k_ref, m2k_ref,
                    qa_ref, ka_ref, va_ref, qs_ref, ks_ref, vs_ref):
    proj = _dot(x_ref[...].astype(BF16), win_ref[...])
    c_q = _rms_norm(proj[:, _AB_CQ:_AB_CKV], qn_ref[...])
    qf = _dot(c_q.astype(BF16), wuq_ref[...])
    c_kv = _rms_norm(proj[:, _AB_CKV:_AB_QS], kvn_ref[...])
    kvf = _dot(c_kv.astype(BF16), wukv_ref[...])
    k_rope = _rope_mix(proj[:, _AB_KR:_AB_COLS], m1k_ref[...], m2k_ref[...])
    m1q, m2q = m1q_ref[...], m2q_ref[...]
    for h in range(N_HEADS):
        sl = slice(h * LANES, (h + 1) * LANES)
        qa_ref[0, h] = _rope_mix(qf[:, sl], m1q, m2q).astype(BF16)
        ka_ref[0, h] = (kvf[:, sl] + k_rope).astype(BF16)
    for j in range(N_HEADS // 2):
        sl = slice(j * LANES, (j + 1) * LANES)
        va_ref[0, j] = kvf[:, N_HEADS * LANES:][:, sl].astype(BF16)
        qs_ref[0, j] = (proj[:, _AB_QS:_AB_KS][:, sl] * (HEAD_DIM ** -0.5 * LOG2E)).astype(BF16)
    ks_ref[0, 0] = proj[:, _AB_KS:_AB_VS].astype(BF16)
    vs_ref[0, 0] = proj[:, _AB_VS:_AB_KR].astype(BF16)


def _proj_ab(x2, w_in, q_norm, w_uq, kv_norm, w_ukv, seq):
    t = x2.shape[0]
    nsb = seq // TM
    c_q, c_kv, k_r, q_s, k_s, v_s = jnp.split(w_in, [384, 640, 672, 1184, 1312], axis=1)
    q_s = q_s.reshape(D_MODEL, 2, 4, HEAD_DIM).transpose(0, 2, 1, 3).reshape(D_MODEL, 512)
    kr_blk = jnp.concatenate([jnp.zeros((D_MODEL, 64), F32), k_r, k_r[:, 16:], k_r[:, :16]], axis=1)
    win_p = jnp.concatenate([c_q, c_kv, q_s, k_s, v_s, kr_blk], axis=1).astype(BF16)
    wq = w_uq.reshape(MLA_Q_RANK, N_HEADS, MLA_NOPE + MLA_ROPE)
    wuq_p = jnp.concatenate([wq, wq[..., 80:96], wq[..., 64:80]], axis=-1)
    wuq_p = wuq_p.reshape(MLA_Q_RANK, N_HEADS * LANES).astype(BF16)
    wkv = w_ukv.reshape(MLA_KV_RANK, N_HEADS, 2 * HEAD_DIM)
    k_pad = jnp.concatenate([wkv[..., :64], jnp.zeros_like(wkv[..., :64])], axis=-1)
    wukv_p = jnp.concatenate([k_pad.reshape(MLA_KV_RANK, N_HEADS * LANES),
                              wkv[..., 64:].reshape(MLA_KV_RANK, 512)], axis=1).astype(BF16)
    inv = ROPE_THETA ** (-jnp.arange(0, MLA_ROPE, 2, dtype=F32) / MLA_ROPE)
    ang = jnp.arange(seq, dtype=F32)[:, None] * inv[None, :]
    cos, sin = jnp.cos(ang), jnp.sin(ang)
    z64, z32, o64 = jnp.zeros((seq, 64), F32), jnp.zeros((seq, 32), F32), jnp.ones((seq, 64), F32)
    m1k = jnp.concatenate([z64, cos, cos, z32], axis=1)
    m2k = jnp.concatenate([z64, -sin, sin, z32], axis=1)
    q_scale = (MLA_NOPE + MLA_ROPE) ** -0.5 * LOG2E
    m1q = jnp.concatenate([o64, cos, cos, z32], axis=1) * q_scale
    m2q = m2k * q_scale

    row = lambda w: pl.BlockSpec((TM, w), lambda i: (i, 0))
    tab = pl.BlockSpec((TM, LANES), lambda i: (i % nsb, 0))
    out_blocks = (8, 8, 4, 4, 1, 1)
    return pl.pallas_call(
        _proj_ab_kernel,
        grid=(t // TM,),
        in_specs=[row(D_MODEL), _const_spec((D_MODEL, _AB_COLS)), _const_spec((1, MLA_Q_RANK)),
                  _const_spec((MLA_Q_RANK, 1024)), _const_spec((1, MLA_KV_RANK)),
                  _const_spec((MLA_KV_RANK, 1536)), tab, tab, tab, tab],
        out_specs=[_block_major_spec(n, nsb) for n in out_blocks],
        out_shape=[jax.ShapeDtypeStruct((t // seq, n, seq, LANES), BF16) for n in out_blocks],
        compiler_params=pltpu.CompilerParams(dimension_semantics=("arbitrary",),
                                             vmem_limit_bytes=VMEM_LIMIT),
        name="proj_ab",
    )(x2, win_p, q_norm.reshape(1, -1), wuq_p, kv_norm.reshape(1, -1), wukv_p, m1q, m2q, m1k, m2k)


def _proj_cd_kernel(x_ref, win_ref, bf_ref, qkv_ref, cum_ref, carry_ref, *, nsb):
    @pl.when(pl.program_id(0) % nsb == 0)
    def _():
        carry_ref[...] = jnp.zeros_like(carry_ref)

    proj = _dot(x_ref[...].astype(BF16), win_ref[...])
    q_scale = HEAD_DIM ** -0.5 * LOG2E
    for blk in range(3072 // LANES):
        part = proj[:, blk * LANES:(blk + 1) * LANES]
        is_query = (blk // 4) % 3 == 0
        qkv_ref[0, blk] = (part * q_scale if is_query else part).astype(BF16)
    z = proj[:, 3072:] + bf_ref[...]
    log_f = jnp.minimum(z, 0.0) - jnp.log(1.0 + jnp.exp(-jnp.abs(z)))
    r = lax.broadcasted_iota(jnp.int32, (CUMSUM_ROWS, CUMSUM_ROWS), 0)
    c = lax.broadcasted_iota(jnp.int32, (CUMSUM_ROWS, CUMSUM_ROWS), 1)
    tri = jnp.where(c <= r, 1.0, 0.0).astype(BF16)
    hi = log_f.astype(BF16)
    rem = log_f - hi.astype(F32)
    mid = rem.astype(BF16)
    lo = (rem - mid.astype(F32)).astype(BF16)
    carry = carry_ref[...]
    for blk in range(TM // CUMSUM_ROWS):
        rows = slice(blk * CUMSUM_ROWS, (blk + 1) * CUMSUM_ROWS)
        cum = _dot(tri, hi[rows]) + _dot(tri, mid[rows]) + _dot(tri, lo[rows]) + carry
        cum_ref[rows, :] = cum
        carry = cum[CUMSUM_ROWS - 1:CUMSUM_ROWS, :]
    carry_ref[...] = carry


def _proj_cd(x2, w_in, b_forget, seq):
    t = x2.shape[0]
    q_f, k_f, v_f, f_l, q_c, k_c, v_c = jnp.split(w_in, [512, 1024, 1536, 1544, 2056, 2568], axis=1)
    f_pad = jnp.concatenate([f_l, jnp.zeros((D_MODEL, LANES - N_HEADS), F32)], axis=1)
    win_p = jnp.concatenate([q_f, k_f, v_f, q_c, k_c, v_c, f_pad], axis=1).astype(BF16)
    bf = jnp.concatenate([b_forget, jnp.zeros((LANES - N_HEADS,), F32)]).reshape(1, LANES)
    return pl.pallas_call(
        functools.partial(_proj_cd_kernel, nsb=seq // TM),
        grid=(t // TM,),
        in_specs=[pl.BlockSpec((TM, D_MODEL), lambda i: (i, 0)),
                  _const_spec((D_MODEL, 3072 + LANES)), _const_spec((1, LANES))],
        out_specs=[_block_major_spec(3072 // LANES, seq // TM),
                   pl.BlockSpec((TM, LANES), lambda i: (i, 0))],
        out_shape=[jax.ShapeDtypeStruct((t // seq, 3072 // LANES, seq, LANES), BF16),
                   jax.ShapeDtypeStruct((t, LANES), F32)],
        scratch_shapes=[pltpu.VMEM((1, LANES), F32)],
        compiler_params=pltpu.CompilerParams(dimension_semantics=("arbitrary",),
                                             vmem_limit_bytes=VMEM_LIMIT),
        name="proj_cd",
    )(x2, win_p, bf)


def _attn_kernel(*refs, variant, window, seq, tq, tk, n_sub):
    if variant == "mla":
        q_ref, k_ref, v_ref, qn_ref, o_ref, vt_ref, m_ref, acc_ref, sa_ref, sb_ref = refs
    elif variant == "fox":
        (q_ref, k_ref, v_ref, qn_ref, ck_ref, o_ref, vt_ref, m_ref, acc_ref, sa_ref, sb_ref,
         ckrep_ref) = refs
    elif variant == "swa":
        sink_ref, q_ref, k_ref, v_ref, bias_ref, o_ref, vt_ref, kpad_ref, sbuf_ref = refs
    else:
        q_ref, k_ref, v_ref, bias_ref, o_ref, vt_ref, kpad_ref, sbuf_ref = refs

    i = pl.program_id(2)
    banded = window is not None
    pad = window - tq if banded else 0

    def setup():
        for c in range(seq // 512):
            sl = slice(pad + c * 512, pad + (c + 1) * 512)
            vt_ref[0:LANES, sl] = v_ref[0, 0, c * 512:(c + 1) * 512, :].astype(F32).T.astype(BF16)
        vt_ref[LANES:VT_ROWS, :] = jnp.ones((VT_ROWS - LANES, pad + seq), BF16)
        if banded:
            vt_ref[0:LANES, 0:pad] = jnp.zeros((LANES, pad), BF16)
            kpad_ref[0:pad, :] = jnp.zeros((pad, LANES), BF16)
            kpad_ref[pad:pad + seq, :] = k_ref[0, 0]
        if variant == "fox":
            src = lax.broadcasted_iota(jnp.int32, (LANES, 2 * LANES), 0)
            dst = lax.broadcasted_iota(jnp.int32, (LANES, 2 * LANES), 1)
            pick = jnp.where(src == 2 * pl.program_id(0) + (dst >> 7), 1.0, 0.0).astype(BF16)
            for c in range(seq // 512):
                rows = slice(c * 512, (c + 1) * 512)
                cum = ck_ref[0, rows, :]
                hi = cum.astype(BF16)
                rem = cum - hi.astype(F32)
                mid = rem.astype(BF16)
                lo = (rem - mid.astype(F32)).astype(BF16)
                ckrep_ref[rows, :] = (_dot(hi, pick) + _dot(mid, pick) + _dot(lo, pick)) * LOG2E

    sub = lax.broadcasted_iota(jnp.int32, (LANES, tq), 0)
    lane = lax.broadcasted_iota(jnp.int32, (tq, LANES), 1)

    def pair_rows(q_blk):
        zero = jnp.zeros_like(q_blk)
        return jnp.concatenate([jnp.where(lane < HEAD_DIM, q_blk, zero),
                                jnp.where(lane >= HEAD_DIM, q_blk, zero)], axis=0)

    def finish(acc, m, rows):
        l = acc[LANES:LANES + 1, :]
        o_t = acc[:LANES, :]
        if variant == "swa":
            pair = pl.program_id(0)
            col = lax.broadcasted_iota(jnp.int32, (1, 2 * tq), 1)
            sink = jnp.where(col < tq, sink_ref[pair], sink_ref[pair + N_HEADS // 2])
            m_fin = jnp.maximum(m, sink)
            scale = jnp.exp2(m - m_fin)
            l = l * scale + jnp.exp2(sink - m_fin)
            o_t = o_t * scale
        o_t = o_t / l
        o_ref[0, 0, rows, :] = jnp.where(sub < HEAD_DIM, o_t[:, :tq], o_t[:, tq:]).T.astype(o_ref.dtype)

    if banded:
        pl.when(i == 0)(setup)
        n_var = bias_ref.shape[1]

        def band_scores(sb, s_ref):
            ib = i * n_sub + sb
            start = pl.multiple_of(ib * tq, tq)
            rows = pl.ds(pl.multiple_of(sb * tq, tq), tq)
            v = jnp.maximum(n_var - 1 - ib, 0)
            s_ref[...] = (_dot_nt(kpad_ref[pl.ds(start, window), :], pair_rows(q_ref[0, 0, rows, :]))
                          + bias_ref[0, v])

        def band_softmax(sb, s_ref):
            start = pl.multiple_of((i * n_sub + sb) * tq, tq)
            s = s_ref[...]
            m = jnp.max(s, axis=0, keepdims=True)
            p = jnp.exp2(s - m).astype(BF16)
            finish(_dot(vt_ref[:, pl.ds(start, window)], p), m,
                   pl.ds(pl.multiple_of(sb * tq, tq), tq))

        group = BAND_GROUP
        n_groups = n_sub // group

        def group_scores(g, parity):
            for r in range(group):
                band_scores(g * group + r, sbuf_ref.at[parity * group + r])

        def group_softmax(g, parity):
            for r in range(group):
                band_softmax(g * group + r, sbuf_ref.at[parity * group + r])

        group_scores(0, 0)

        def two_groups(u, carry):
            g = 2 * u
            group_scores(g + 1, 1)
            group_softmax(g, 0)
            group_scores(g + 2, 0)
            group_softmax(g + 1, 1)
            return carry

        lax.fori_loop(0, n_groups // 2 - 1, two_groups, None)
        group_scores(n_groups - 1, 1)
        group_softmax(n_groups - 2, 0)
        group_softmax(n_groups - 1, 1)
        return

    if variant == "mla":
        query_operands = lambda ref: (ref[0, 0], ref[0, 1])
    else:
        query_operands = lambda ref: pair_rows(ref[0, 0])

    m_ref[...] = jnp.full_like(m_ref, MASK)
    acc_ref[...] = jnp.zeros_like(acc_ref)

    first_half = [(h * tq, tk) for h in range(2)]
    second_half = [(h * tq + tk, tk) for h in range(2)]

    q_now = query_operands(q_ref)

    def scores(j, s_ref, slabs=((0, 2 * tq),), queries=q_now):
        start = pl.multiple_of(j * tk, tk)
        k_blocks = [k_ref[0, h, pl.ds(start, tk), :] for h in range(k_ref.shape[1])]
        for c0, w in slabs:
            cols = slice(c0, c0 + w)
            if variant == "mla":
                for h in range(2):
                    lo, hi = max(c0, h * tq), min(c0 + w, (h + 1) * tq)
                    if lo < hi:
                        q_h = queries[h][lo - h * tq:hi - h * tq, :]
                        s_ref[:, lo:hi] = _dot_nt(k_blocks[h], q_h)
                continue
            s = _dot_nt(k_blocks[0], queries[cols, :])
            if variant == "fox":
                ck_reps = [ckrep_ref[pl.ds(start, tk), (c // tq) * LANES:(c // tq + 1) * LANES]
                           for c in range(c0, c0 + w, LANES)]
                s = s - jnp.concatenate(ck_reps, axis=1)
            s_ref[:, cols] = s

    def consume(j, s_ref, slabs, masked=()):
        start = pl.multiple_of(j * tk, tk)
        v_t = vt_ref[:, pl.ds(start, tk)]
        for c0, w in slabs:
            cols = slice(c0, c0 + w)
            s = s_ref[:, cols]
            if (c0, w) in masked:
                key = lax.broadcasted_iota(jnp.int32, (tk, w), 0)
                qry = lax.broadcasted_iota(jnp.int32, (tk, w), 1)
                if variant == "mla":
                    keep = key <= (qry | (CHUNK - 1))
                else:
                    keep = key <= qry
                s = jnp.where(keep, s, MASK)
            m_prev = m_ref[:, cols]
            m_new = jnp.maximum(m_prev, jnp.max(s, axis=0, keepdims=True))
            alpha = jnp.exp2(m_prev - m_new)
            p = jnp.exp2(s - m_new).astype(BF16)
            acc_ref[:, cols] = alpha * acc_ref[:, cols] + _dot(v_t, p)
            m_ref[:, cols] = m_new

    everything = [(0, 2 * tq)]

    @pl.when(i == 0)
    def _():
        setup()
        scores(0, sa_ref)

    def two_blocks(u, carry):
        t = 2 * u
        scores(t + 1, sb_ref)
        consume(t, sa_ref, everything)
        scores(t + 2, sa_ref)
        consume(t + 1, sb_ref, everything)
        return carry

    lax.fori_loop(0, i, two_blocks, None)

    def diagonal_blocks(prefetch_next):
        q_next = query_operands(qn_ref) if prefetch_next else None
        scores(2 * i + 1, sb_ref, second_half)
        consume(2 * i, sa_ref, first_half, masked=first_half)
        if prefetch_next:
            scores(0, sa_ref, first_half, queries=q_next)
        consume(2 * i, sa_ref, second_half)
        if prefetch_next:
            scores(0, sa_ref, second_half, queries=q_next)
        consume(2 * i + 1, sb_ref, second_half, masked=second_half)

    last = pl.num_programs(2) - 1
    pl.when(i < last)(lambda: diagonal_blocks(True))
    pl.when(i == last)(lambda: diagonal_blocks(False))
    finish(acc_ref[...], m_ref[...], slice(0, tq))


def _attention(variant, q, k, v, q_col, k_col, v_col, *, extra=None, sinks=None):
    b, _, s, _ = q.shape
    n_pairs = N_HEADS // 2
    if variant in ("mla", "fox"):
        tq, tk, n_sub = 2 * FULL_TILE, FULL_TILE, 1
    else:
        tq, tk, n_sub = TQ, TK, BAND_SUB_BLOCKS
    nq = s // (tq * n_sub)
    window = {"mla": None, "fox": None, "swa": SWA_WINDOW_KEYS, "ck": CK_WINDOW_KEYS}[variant]
    qb = 2 if variant == "mla" else 1
    shared_kv = variant == "swa"

    def im(f):
        if sinks is None:
            return f
        return lambda p, bi, i, sink_ref: f(p, bi, i)

    in_specs = [
        pl.BlockSpec((1, qb, tq * n_sub, LANES), im(lambda p, bi, i: (bi, q_col + p, i, 0))),
        pl.BlockSpec((1, qb, s, LANES), im(lambda p, bi, i: (bi, k_col + (0 if shared_kv else p), 0, 0))),
        pl.BlockSpec((1, 1, s, LANES), im(lambda p, bi, i: (bi, v_col + (0 if shared_kv else p), 0, 0))),
    ]
    args = [q, k, v]
    if window is None:
        in_specs.append(pl.BlockSpec((1, qb, tq, LANES),
                                     lambda p, bi, i: (bi, q_col + p, jnp.minimum(i + 1, nq - 1), 0)))
        args.append(q)
        scratch = [pltpu.VMEM((VT_ROWS, s), BF16), pltpu.VMEM((1, 2 * tq), F32),
                   pltpu.VMEM((VT_ROWS, 2 * tq), F32),
                   pltpu.VMEM((tk, 2 * tq), F32), pltpu.VMEM((tk, 2 * tq), F32)]
    else:
        pad = window - tq
        scratch = [pltpu.VMEM((VT_ROWS, pad + s), BF16), pltpu.VMEM((pad + s, LANES), BF16),
                   pltpu.VMEM((2 * BAND_GROUP, window, 2 * tq), F32)]
        in_specs.append(pl.BlockSpec((1,) + extra.shape[1:], im(lambda p, bi, i: (p, 0, 0, 0))))
        args.append(extra)
    if variant == "fox":
        in_specs.append(pl.BlockSpec((1, s, LANES), lambda p, bi, i: (bi, 0, 0)))
        args.append(extra)
        scratch.append(pltpu.VMEM((s, 2 * LANES), F32))
    out_spec = pl.BlockSpec((1, 1, tq * n_sub, LANES), im(lambda p, bi, i: (bi, p, i, 0)))
    kern = functools.partial(_attn_kernel, variant=variant, window=window, seq=s, tq=tq, tk=tk,
                             n_sub=n_sub)
    params = pltpu.CompilerParams(dimension_semantics=("arbitrary",) * 3, vmem_limit_bytes=VMEM_LIMIT)
    out_shape = jax.ShapeDtypeStruct((b, n_pairs, s, LANES), BF16)
    if sinks is not None:
        grid_spec = pltpu.PrefetchScalarGridSpec(
            num_scalar_prefetch=1, grid=(n_pairs, b, nq), in_specs=in_specs, out_specs=out_spec,
            scratch_shapes=scratch)
        return pl.pallas_call(kern, grid_spec=grid_spec, out_shape=out_shape,
                              compiler_params=params, name="attn_" + variant)(sinks, *args)
    return pl.pallas_call(kern, grid=(n_pairs, b, nq), in_specs=in_specs, out_specs=out_spec,
                          out_shape=out_shape, scratch_shapes=scratch,
                          compiler_params=params, name="attn_" + variant)(*args)


def _lead_rows(window, n_var, v):
    return max(window - TQ - (n_var - 1 - v) * TQ, 0)


def _swa_bias():
    window, left = SWA_WINDOW_KEYS, SWA_LEFT_CHUNKS
    c = jnp.arange(window)[:, None]
    a = jnp.arange(TQ)[None, :]
    dist = a - c + (window - TQ)
    ahead = (a // CHUNK + left) - c // CHUNK
    slopes = jnp.exp2(-8.0 * jnp.arange(1, N_HEADS + 1, dtype=F32) / N_HEADS)
    bias = -slopes[:, None, None] * jnp.abs(dist).astype(F32)[None] * LOG2E
    bias = jnp.where(((ahead >= 0) & (ahead <= left))[None], bias, MASK)
    bias = jnp.concatenate([bias[:4], bias[4:]], axis=2)
    n_var = 2
    return jnp.stack([jnp.where(c[None] >= _lead_rows(window, n_var, v), bias, MASK)
                      for v in range(n_var)], axis=1)


_REL_ROWS = 384


_CK_VARIANTS = 3


def _ck_bias_kernel(tab_ref, out_ref):
    window = CK_WINDOW_KEYS
    span = TQ + window
    tab = tab_ref[0] * LOG2E
    hi = tab.astype(BF16)
    rem = tab - hi.astype(F32)
    mid = rem.astype(BF16)
    lo = (rem - mid.astype(F32)).astype(BF16)
    t = lax.broadcasted_iota(jnp.int32, (_REL_ROWS, span), 0)
    n = lax.broadcasted_iota(jnp.int32, (_REL_ROWS, span), 1)
    delta = jnp.where(n < TQ, n, n - span)
    idx = jnp.clip(delta + (window - TQ), -(CHUNK - 1), REL_MAX) + (CHUNK - 1)
    onehot = jnp.where(t == idx, 1.0, 0.0).astype(BF16)
    base = _dot(hi, onehot) + _dot(mid, onehot) + _dot(lo, onehot)
    c = lax.broadcasted_iota(jnp.int32, (window, TQ), 0)
    a = lax.broadcasted_iota(jnp.int32, (window, TQ), 1)
    valid = ((a >> 6) + CK_LEFT_CHUNKS - (c >> 6)).astype(jnp.uint32) <= CK_LEFT_CHUNKS
    for h in range(2):
        rows = jnp.broadcast_to(base[h:h + 1, :], (window, span))
        tile = jnp.where(valid, pltpu.roll(rows, 0, 1, stride=1, stride_axis=0)[:, :TQ], MASK)
        for v in range(_CK_VARIANTS):
            lead = _lead_rows(window, _CK_VARIANTS, v)
            out_ref[0, v, :, h * TQ:(h + 1) * TQ] = jnp.where(c >= lead, tile, MASK)


def _ck_bias(rel_bias):
    tab = jnp.pad(rel_bias.T, ((0, 0), (0, _REL_ROWS - rel_bias.shape[0])))
    tab = jnp.pad(tab.reshape(N_HEADS // 2, 2, _REL_ROWS), ((0, 0), (0, 6), (0, 0)))
    return pl.pallas_call(
        _ck_bias_kernel,
        grid=(N_HEADS // 2,),
        in_specs=[pl.BlockSpec((1, 8, _REL_ROWS), lambda p: (p, 0, 0))],
        out_specs=pl.BlockSpec((1, _CK_VARIANTS, CK_WINDOW_KEYS, 2 * TQ), lambda p: (p, 0, 0, 0)),
        out_shape=jax.ShapeDtypeStruct((N_HEADS // 2, _CK_VARIANTS, CK_WINDOW_KEYS, 2 * TQ), F32),
        compiler_params=pltpu.CompilerParams(dimension_semantics=("arbitrary",),
                                             vmem_limit_bytes=VMEM_LIMIT),
        name="ck_bias",
    )(tab)


def _out_ffn_kernel(x_ref, oa_ref, ob_ref, woa_ref, wob_ref, g1_ref, b1_ref,
                    wg_ref, wu_ref, wd_ref, g2_ref, b2_ref, y_ref):
    o_a = jnp.concatenate([oa_ref[0, j] for j in range(N_HEADS // 2)], axis=1)
    o_b = jnp.concatenate([ob_ref[0, j] for j in range(N_HEADS // 2)], axis=1)
    mix = _dot(o_a, woa_ref[...]) + _dot(o_b, wob_ref[...])
    x1 = _layer_norm(DEEPNORM_ALPHA * x_ref[...] + mix, g1_ref[...], b1_ref[...])
    x1b = x1.astype(BF16)
    gate = _dot(x1b, wg_ref[...])
    up = _dot(x1b, wu_ref[...])
    hidden = (gate * jax.nn.sigmoid(gate) * up).astype(BF16)
    ffn = _dot(hidden, wd_ref[...])
    y_ref[...] = _layer_norm(DEEPNORM_ALPHA * x1 + ffn, g2_ref[...], b2_ref[...])


def _out_ffn(x2, o_a, o_b, w_out_a, w_out_b, g1, b1, w_gate, w_up, w_down, g2, b2):
    t = x2.shape[0]
    d_ff = w_gate.shape[1]
    row = lambda w: pl.BlockSpec((TM_FFN, w), lambda i: (i, 0))
    steps_per_seq = o_a.shape[2] // TM_FFN
    mixer = pl.BlockSpec((1, N_HEADS // 2, TM_FFN, LANES),
                         lambda i: (i // steps_per_seq, 0, i % steps_per_seq, 0))
    vec = _const_spec((1, D_MODEL))
    return pl.pallas_call(
        _out_ffn_kernel,
        grid=(t // TM_FFN,),
        in_specs=[row(D_MODEL), mixer, mixer, _const_spec((512, D_MODEL)),
                  _const_spec((512, D_MODEL)), vec, vec, _const_spec((D_MODEL, d_ff)),
                  _const_spec((D_MODEL, d_ff)), _const_spec((d_ff, D_MODEL)), vec, vec],
        out_specs=row(D_MODEL),
        out_shape=jax.ShapeDtypeStruct((t, D_MODEL), F32),
        compiler_params=pltpu.CompilerParams(dimension_semantics=("arbitrary",),
                                             vmem_limit_bytes=VMEM_LIMIT),
        name="out_ffn",
    )(x2, o_a, o_b, w_out_a.astype(BF16), w_out_b.astype(BF16), g1.reshape(1, -1), b1.reshape(1, -1),
      w_gate.astype(BF16), w_up.astype(BF16), w_down.astype(BF16), g2.reshape(1, -1), b2.reshape(1, -1))


def kernel(x, ab_w_in, ab_q_norm, ab_w_uq, ab_kv_norm, ab_w_ukv, ab_sinks, ab_w_out,
           cd_w_in, cd_b_forget, cd_rel_bias, cd_w_out,
           ln1_g, ln1_b, ffn_w_gate, ffn_w_up, ffn_w_down, ln2_g, ln2_b):
    b, s, d = x.shape
    assert d == D_MODEL and TQ == TK
    assert s % TM == 0 and s % (2 * FULL_TILE) == 0 and s % (TQ * BAND_SUB_BLOCKS) == 0
    t = b * s
    x2 = x.reshape(t, d)

    qa, ka, va, qs, ks, vs = _proj_ab(x2, ab_w_in[0], ab_q_norm[0], ab_w_uq[0], ab_kv_norm[0],
                                      ab_w_ukv[0], s)
    o_a = _attention("mla", qa, ka, va, 0, 0, 0)
    o_b = _attention("swa", qs, ks, vs, 0, 0, 0, extra=_swa_bias(), sinks=ab_sinks[0] * LOG2E)
    w_out = ab_w_out[0]
    w_out_b = w_out[512:].reshape(2, 4, HEAD_DIM, D_MODEL).transpose(1, 0, 2, 3).reshape(512, D_MODEL)
    x2 = _out_ffn(x2, o_a, o_b, w_out[:512], w_out_b,
                  ln1_g[0], ln1_b[0], ffn_w_gate[0], ffn_w_up[0], ffn_w_down[0], ln2_g[0], ln2_b[0])

    qkv, cum = _proj_cd(x2, cd_w_in[0], cd_b_forget[0], s)
    o_c = _attention("fox", qkv, qkv, qkv, 0, 4, 8, extra=cum.reshape(b, s, LANES))
    o_d = _attention("ck", qkv, qkv, qkv, 12, 16, 20, extra=_ck_bias(cd_rel_bias[0]))
    w_out = cd_w_out[0]
    x2 = _out_ffn(x2, o_c, o_d, w_out[:512], w_out[512:],
                  ln1_g[1], ln1_b[1], ffn_w_gate[1], ffn_w_up[1], ffn_w_down[1], ln2_g[1], ln2_b[1])
    return x2.reshape(b, s, d)
```

```python
import functools
import math

import numpy as np
import jax
import jax.numpy as jnp
from jax import lax
from jax.experimental import pallas as pl
from jax.experimental.pallas import tpu as pltpu

D_MODEL = 1024
CHUNK = 64
HEAD_DIM = 64
N_HEADS = 8
LANES = 128
LN_EPS = 1e-5
RMS_EPS = 1e-6
MASK = -1e30
LOG2E = math.log2(math.e)

MLA_Q_RANK = 384
MLA_KV_RANK = 256
MLA_NOPE = 64
MLA_ROPE = 32
ROPE_THETA = 10000.0
SWA_KV_HEADS = 2
SWA_LEFT_CHUNKS = 2
CK_LEFT_CHUNKS = 8
REL_MAX = 256
DEPTH = 2
DEEPNORM_ALPHA = (2 * DEPTH) ** 0.25

TQ = 256
TK = 256
BAND_SUB_BLOCKS = 16
BAND_GROUP = 2
SWA_WINDOW_KEYS = SWA_LEFT_CHUNKS * CHUNK + TQ
CK_WINDOW_KEYS = CK_LEFT_CHUNKS * CHUNK + TQ
FULL_TILE = 512
TM = 1024
TM_FFN = 512
CUMSUM_ROWS = 128
VT_ROWS = LANES + 16
VMEM_LIMIT = 56 * 1024 * 1024

BF16 = jnp.bfloat16
F32 = jnp.float32


def _dot(a, b):
    return jnp.dot(a, b, preferred_element_type=F32)


def _dot_nt(a, b):
    return lax.dot_general(a, b, (((1,), (1,)), ((), ())), preferred_element_type=F32)


def _layer_norm(y, g, b):
    mu = jnp.mean(y, axis=-1, keepdims=True)
    yc = y - mu
    var = jnp.mean(yc * yc, axis=-1, keepdims=True)
    return yc * lax.rsqrt(var + LN_EPS) * g + b


def _rms_norm(c, g):
    return c * lax.rsqrt(jnp.mean(c * c, axis=-1, keepdims=True) + RMS_EPS) * g


def _block_major_spec(n_blocks, steps_per_seq):
    return pl.BlockSpec((1, n_blocks, TM, LANES),
                        lambda i: (i // steps_per_seq, 0, i % steps_per_seq, 0))


def _const_spec(shape):
    zeros = (0,) * len(shape)
    return pl.BlockSpec(shape, lambda *_: zeros, pipeline_mode=pl.Buffered(1))


_AB_CQ, _AB_CKV, _AB_QS, _AB_KS, _AB_VS, _AB_KR, _AB_COLS = 0, 384, 640, 1152, 1280, 1408, 1536


def _rope_mix(t, m1, m2):
    return t * m1 + pltpu.roll(t, 96, 1) * m2


def _proj_ab_kernel(x_ref, win_ref, qn_ref, wuq_ref, kvn_ref, wukv_ref,
                    m1q_ref, m2q_ref, m1k_ref, m2k_ref,
                    qa_ref, ka_ref, va_ref, qs_ref, ks_ref, vs_ref):
    proj = _dot(x_ref[...].astype(BF16), win_ref[...])
    c_q = _rms_norm(proj[:, _AB_CQ:_AB_CKV], qn_ref[...])
    qf = _dot(c_q.astype(BF16), wuq_ref[...])
    c_kv = _rms_norm(proj[:, _AB_CKV:_AB_QS], kvn_ref[...])
    kvf = _dot(c_kv.astype(BF16), wukv_ref[...])
    k_rope = _rope_mix(proj[:, _AB_KR:_AB_COLS], m1k_ref[...], m2k_ref[...])
    m1q, m2q = m1q_ref[...], m2q_ref[...]
    for h in range(N_HEADS):
        sl = slice(h * LANES, (h + 1) * LANES)
        qa_ref[0, h] = _rope_mix(qf[:, sl], m1q, m2q).astype(BF16)
        ka_ref[0, h] = (kvf[:, sl] + k_rope).astype(BF16)
    for j in range(N_HEADS // 2):
        sl = slice(j * LANES, (j + 1) * LANES)
        va_ref[0, j] = kvf[:, N_HEADS * LANES:][:, sl].astype(BF16)
        qs_ref[0, j] = (proj[:, _AB_QS:_AB_KS][:, sl] * (HEAD_DIM ** -0.5 * LOG2E)).astype(BF16)
    ks_ref[0, 0] = proj[:, _AB_KS:_AB_VS].astype(BF16)
    vs_ref[0, 0] = proj[:, _AB_VS:_AB_KR].astype(BF16)


def _proj_ab(x2, w_in, q_norm, w_uq, kv_norm, w_ukv, seq):
    t = x2.shape[0]
    nsb = seq // TM
    c_q, c_kv, k_r, q_s, k_s, v_s = jnp.split(w_in, [384, 640, 672, 1184, 1312], axis=1)
    q_s = q_s.reshape(D_MODEL, 2, 4, HEAD_DIM).transpose(0, 2, 1, 3).reshape(D_MODEL, 512)
    kr_blk = jnp.concatenate([jnp.zeros((D_MODEL, 64), F32), k_r, k_r[:, 16:], k_r[:, :16]], axis=1)
    win_p = jnp.concatenate([c_q, c_kv, q_s, k_s, v_s, kr_blk], axis=1).astype(BF16)
    wq = w_uq.reshape(MLA_Q_RANK, N_HEADS, MLA_NOPE + MLA_ROPE)
    wuq_p = jnp.concatenate([wq, wq[..., 80:96], wq[..., 64:80]], axis=-1)
    wuq_p = wuq_p.reshape(MLA_Q_RANK, N_HEADS * LANES).astype(BF16)
    wkv = w_ukv.reshape(MLA_KV_RANK, N_HEADS, 2 * HEAD_DIM)
    k_pad = jnp.concatenate([wkv[..., :64], jnp.zeros_like(wkv[..., :64])], axis=-1)
    wukv_p = jnp.concatenate([k_pad.reshape(MLA_KV_RANK, N_HEADS * LANES),
                              wkv[..., 64:].reshape(MLA_KV_RANK, 512)], axis=1).astype(BF16)
    inv = ROPE_THETA ** (-jnp.arange(0, MLA_ROPE, 2, dtype=F32) / MLA_ROPE)
    ang = jnp.arange(seq, dtype=F32)[:, None] * inv[None, :]
    cos, sin = jnp.cos(ang), jnp.sin(ang)
    z64, z32, o64 = jnp.zeros((seq, 64), F32), jnp.zeros((seq, 32), F32), jnp.ones((seq, 64), F32)
    m1k = jnp.concatenate([z64, cos, cos, z32], axis=1)
    m2k = jnp.concatenate([z64, -sin, sin, z32], axis=1)
    q_scale = (MLA_NOPE + MLA_ROPE) ** -0.5 * LOG2E
    m1q = jnp.concatenate([o64, cos, cos, z32], axis=1) * q_scale
    m2q = m2k * q_scale

    row = lambda w: pl.BlockSpec((TM, w), lambda i: (i, 0))
    tab = pl.BlockSpec((TM, LANES), lambda i: (i % nsb, 0))
    out_blocks = (8, 8, 4, 4, 1, 1)
    return pl.pallas_call(
        _proj_ab_kernel,
        grid=(t // TM,),
        in_specs=[row(D_MODEL), _const_spec((D_MODEL, _AB_COLS)), _const_spec((1, MLA_Q_RANK)),
                  _const_spec((MLA_Q_RANK, 1024)), _const_spec((1, MLA_KV_RANK)),
                  _const_spec((MLA_KV_RANK, 1536)), tab, tab, tab, tab],
        out_specs=[_block_major_spec(n, nsb) for n in out_blocks],
        out_shape=[jax.ShapeDtypeStruct((t // seq, n, seq, LANES), BF16) for n in out_blocks],
        compiler_params=pltpu.CompilerParams(dimension_semantics=("arbitrary",),
                                             vmem_limit_bytes=VMEM_LIMIT),
        name="proj_ab",
    )(x2, win_p, q_norm.reshape(1, -1), wuq_p, kv_norm.reshape(1, -1), wukv_p, m1q, m2q, m1k, m2k)


def _proj_cd_kernel(x_ref, win_ref, bf_ref, qkv_ref, cum_ref, carry_ref, *, nsb):
    @pl.when(pl.program_id(0) % nsb == 0)
    def _():
        carry_ref[...] = jnp.zeros_like(carry_ref)

    proj = _dot(x_ref[...].astype(BF16), win_ref[...])
    q_scale = HEAD_DIM ** -0.5 * LOG2E
    for blk in range(3072 // LANES):
        part = proj[:, blk * LANES:(blk + 1) * LANES]
        is_query = (blk // 4) % 3 == 0
        qkv_ref[0, blk] = (part * q_scale if is_query else part).astype(BF16)
    z = proj[:, 3072:] + bf_ref[...]
    log_f = jnp.minimum(z, 0.0) - jnp.log(1.0 + jnp.exp(-jnp.abs(z)))
    r = lax.broadcasted_iota(jnp.int32, (CUMSUM_ROWS, CUMSUM_ROWS), 0)
    c = lax.broadcasted_iota(jnp.int32, (CUMSUM_ROWS, CUMSUM_ROWS), 1)
    tri = jnp.where(c <= r, 1.0, 0.0).astype(BF16)
    hi = log_f.astype(BF16)
    rem = log_f - hi.astype(F32)
    mid = rem.astype(BF16)
    lo = (rem - mid.astype(F32)).astype(BF16)
    carry = carry_ref[...]
    for blk in range(TM // CUMSUM_ROWS):
        rows = slice(blk * CUMSUM_ROWS, (blk + 1) * CUMSUM_ROWS)
        cum = _dot(tri, hi[rows]) + _dot(tri, mid[rows]) + _dot(tri, lo[rows]) + carry
        cum_ref[rows, :] = cum
        carry = cum[CUMSUM_ROWS - 1:CUMSUM_ROWS, :]
    carry_ref[...] = carry


def _proj_cd(x2, w_in, b_forget, seq):
    t = x2.shape[0]
    q_f, k_f, v_f, f_l, q_c, k_c, v_c = jnp.split(w_in, [512, 1024, 1536, 1544, 2056, 2568], axis=1)
    f_pad = jnp.concatenate([f_l, jnp.zeros((D_MODEL, LANES - N_HEADS), F32)], axis=1)
    win_p = jnp.concatenate([q_f, k_f, v_f, q_c, k_c, v_c, f_pad], axis=1).astype(BF16)
    bf = jnp.concatenate([b_forget, jnp.zeros((LANES - N_HEADS,), F32)]).reshape(1, LANES)
    return pl.pallas_call(
        functools.partial(_proj_cd_kernel, nsb=seq // TM),
        grid=(t // TM,),
        in_specs=[pl.BlockSpec((TM, D_MODEL), lambda i: (i, 0)),
                  _const_spec((D_MODEL, 3072 + LANES)), _const_spec((1, LANES))],
        out_specs=[_block_major_spec(3072 // LANES, seq // TM),
                   pl.BlockSpec((TM, LANES), lambda i: (i, 0))],
        out_shape=[jax.ShapeDtypeStruct((t // seq, 3072 // LANES, seq, LANES), BF16),
                   jax.ShapeDtypeStruct((t, LANES), F32)],
        scratch_shapes=[pltpu.VMEM((1, LANES), F32)],
        compiler_params=pltpu.CompilerParams(dimension_semantics=("arbitrary",),
                                             vmem_limit_bytes=VMEM_LIMIT),
        name="proj_cd",
    )(x2, win_p, bf)


def _attn_kernel(*refs, variant, window, seq, tq, tk, n_sub):
    if variant == "mla":
        q_ref, k_ref, v_ref, qn_ref, o_ref, vt_ref, m_ref, acc_ref, sa_ref, sb_ref = refs
    elif variant == "fox":
        (q_ref, k_ref, v_ref, qn_ref, ck_ref, o_ref, vt_ref, m_ref, acc_ref, sa_ref, sb_ref,
         ckrep_ref) = refs
    elif variant == "swa":
        sink_ref, q_ref, k_ref, v_ref, bias_ref, o_ref, vt_ref, kpad_ref, sbuf_ref = refs
    else:
        q_ref, k_ref, v_ref, bias_ref, o_ref, vt_ref, kpad_ref, sbuf_ref = refs

    i = pl.program_id(2)
    banded = window is not None
    pad = window - tq if banded else 0

    def setup():
        for c in range(seq // 512):
            sl = slice(pad + c * 512, pad + (c + 1) * 512)
            vt_ref[0:LANES, sl] = v_ref[0, 0, c * 512:(c + 1) * 512, :].astype(F32).T.astype(BF16)
        vt_ref[LANES:VT_ROWS, :] = jnp.ones((VT_ROWS - LANES, pad + seq), BF16)
        if banded:
            vt_ref[0:LANES, 0:pad] = jnp.zeros((LANES, pad), BF16)
            kpad_ref[0:pad, :] = jnp.zeros((pad, LANES), BF16)
            kpad_ref[pad:pad + seq, :] = k_ref[0, 0]
        if variant == "fox":
            src = lax.broadcasted_iota(jnp.int32, (LANES, 2 * LANES), 0)
            dst = lax.broadcasted_iota(jnp.int32, (LANES, 2 * LANES), 1)
            pick = jnp.where(src == 2 * pl.program_id(0) + (dst >> 7), 1.0, 0.0).astype(BF16)
            pick3 = jnp.concatenate([pick, pick, pick], axis=0)
            for c in range(seq // 512):
                rows = slice(c * 512, (c + 1) * 512)
                cum = ck_ref[0, rows, :]
                hi = cum.astype(BF16)
                rem = cum - hi.astype(F32)
                mid = rem.astype(BF16)
                lo = (rem - mid.astype(F32)).astype(BF16)
                ckrep_ref[rows, :] = _dot(jnp.concatenate([hi, mid, lo], axis=1), pick3) * LOG2E

    sub = lax.broadcasted_iota(jnp.int32, (LANES, tq), 0)
    lane = lax.broadcasted_iota(jnp.int32, (tq, LANES), 1)

    def pair_rows(q_blk):
        zero = jnp.zeros_like(q_blk)
        return jnp.concatenate([jnp.where(lane < HEAD_DIM, q_blk, zero),
                                jnp.where(lane >= HEAD_DIM, q_blk, zero)], axis=0)

    def finish(acc, m, rows):
        l = acc[LANES:LANES + 1, :]
        o_t = acc[:LANES, :]
        if variant == "swa":
            pair = pl.program_id(1)
            col = lax.broadcasted_iota(jnp.int32, (1, 2 * tq), 1)
            sink = jnp.where(col < tq, sink_ref[pair], sink_ref[pair + N_HEADS // 2])
            m_fin = jnp.maximum(m, sink)
            scale = jnp.exp2(m - m_fin)
            l = l * scale + jnp.exp2(sink - m_fin)
            o_t = o_t * scale
        o_t = o_t / l
        o_ref[0, 0, rows, :] = jnp.where(sub < HEAD_DIM, o_t[:, :tq], o_t[:, tq:]).T.astype(o_ref.dtype)

    if banded:
        pl.when(i == 0)(setup)
        n_var = bias_ref.shape[1]

        def band_scores(sb, s_ref):
            ib = i * n_sub + sb
            start = pl.multiple_of(ib * tq, tq)
            rows = pl.ds(pl.multiple_of(sb * tq, tq), tq)
            v = jnp.maximum(n_var - 1 - ib, 0)
            s_ref[...] = (_dot_nt(kpad_ref[pl.ds(start, window), :], pair_rows(q_ref[0, 0, rows, :]))
                          + bias_ref[pl.program_id(1) if variant == "swa" else 0, v])

        def band_softmax(sb, s_ref):
            start = pl.multiple_of((i * n_sub + sb) * tq, tq)
            s = s_ref[...]
            m = jnp.max(s, axis=0, keepdims=True)
            p = jnp.exp2(s - m).astype(BF16)
            finish(_dot(vt_ref[:, pl.ds(start, window)], p), m,
                   pl.ds(pl.multiple_of(sb * tq, tq), tq))

        group = BAND_GROUP
        n_groups = n_sub // group

        def group_scores(g, parity):
            for r in range(group):
                band_scores(g * group + r, sbuf_ref.at[parity * group + r])

        def group_softmax(g, parity):
            for r in range(group):
                band_softmax(g * group + r, sbuf_ref.at[parity * group + r])

        group_scores(0, 0)

        def two_groups(u, carry):
            g = 2 * u
            group_scores(g + 1, 1)
            group_softmax(g, 0)
            group_scores(g + 2, 0)
            group_softmax(g + 1, 1)
            return carry

        lax.fori_loop(0, n_groups // 2 - 1, two_groups, None)
        group_scores(n_groups - 1, 1)
        group_softmax(n_groups - 2, 0)
        group_softmax(n_groups - 1, 1)
        return

    if variant == "mla":
        query_operands = lambda ref: (ref[0, 0], ref[0, 1])
    else:
        query_operands = lambda ref: pair_rows(ref[0, 0])

    m_ref[...] = jnp.full_like(m_ref, MASK)
    acc_ref[...] = jnp.zeros_like(acc_ref)

    first_half = [(h * tq, tk) for h in range(2)]
    second_half = [(h * tq + tk, tk) for h in range(2)]

    q_now = query_operands(q_ref)

    def scores(j, s_ref, slabs=((0, 2 * tq),), queries=q_now):
        start = pl.multiple_of(j * tk, tk)
        k_blocks = [k_ref[0, h, pl.ds(start, tk), :] for h in range(k_ref.shape[1])]
        for c0, w in slabs:
            cols = slice(c0, c0 + w)
            if variant == "mla":
                for h in range(2):
                    lo, hi = max(c0, h * tq), min(c0 + w, (h + 1) * tq)
                    if lo < hi:
                        q_h = queries[h][lo - h * tq:hi - h * tq, :]
                        s_ref[:, lo:hi] = _dot_nt(k_blocks[h], q_h)
                continue
            s = _dot_nt(k_blocks[0], queries[cols, :])
            if variant == "fox":
                ck_reps = [ckrep_ref[pl.ds(start, tk), (c // tq) * LANES:(c // tq + 1) * LANES]
                           for c in range(c0, c0 + w, LANES)]
                s = s - jnp.concatenate(ck_reps, axis=1)
            s_ref[:, cols] = s

    def consume(j, s_ref, slabs, masked=()):
        start = pl.multiple_of(j * tk, tk)
        v_t = vt_ref[:, pl.ds(start, tk)]
        for c0, w in slabs:
            cols = slice(c0, c0 + w)
            s = s_ref[:, cols]
            if (c0, w) in masked:
                key = lax.broadcasted_iota(jnp.int32, (tk, w), 0)
                qry = lax.broadcasted_iota(jnp.int32, (tk, w), 1)
                if variant == "mla":
                    keep = key <= (qry | (CHUNK - 1))
                else:
                    keep = key <= qry
                s = jnp.where(keep, s, MASK)
            m_prev = m_ref[:, cols]
            m_new = jnp.maximum(m_prev, jnp.max(s, axis=0, keepdims=True))
            alpha = jnp.exp2(m_prev - m_new)
            p = jnp.exp2(s - m_new).astype(BF16)
            acc_ref[:, cols] = alpha * acc_ref[:, cols] + _dot(v_t, p)
            m_ref[:, cols] = m_new

    everything = [(0, 2 * tq)]

    @pl.when(i == 0)
    def _():
        setup()
        scores(0, sa_ref)

    def two_blocks(u, carry):
        t = 2 * u
        scores(t + 1, sb_ref)
        consume(t, sa_ref, everything)
        scores(t + 2, sa_ref)
        consume(t + 1, sb_ref, everything)
        return carry

    lax.fori_loop(0, i, two_blocks, None)

    def diagonal_blocks(prefetch_next):
        q_next = query_operands(qn_ref) if prefetch_next else None
        scores(2 * i + 1, sb_ref, second_half)
        consume(2 * i, sa_ref, first_half, masked=first_half)
        if prefetch_next:
            scores(0, sa_ref, first_half, queries=q_next)
        consume(2 * i, sa_ref, second_half)
        if prefetch_next:
            scores(0, sa_ref, second_half, queries=q_next)
        consume(2 * i + 1, sb_ref, second_half, masked=second_half)

    last = pl.num_programs(2) - 1
    pl.when(i < last)(lambda: diagonal_blocks(True))
    pl.when(i == last)(lambda: diagonal_blocks(False))
    finish(acc_ref[...], m_ref[...], slice(0, tq))


def _attention(variant, q, k, v, q_col, k_col, v_col, *, extra=None, sinks=None):
    b, _, s, _ = q.shape
    n_pairs = N_HEADS // 2
    if variant in ("mla", "fox"):
        tq, tk, n_sub = 2 * FULL_TILE, FULL_TILE, 1
    else:
        tq, tk, n_sub = TQ, TK, BAND_SUB_BLOCKS
    nq = s // (tq * n_sub)
    window = {"mla": None, "fox": None, "swa": SWA_WINDOW_KEYS, "ck": CK_WINDOW_KEYS}[variant]
    qb = 2 if variant == "mla" else 1
    shared_kv = variant == "swa"

    def im(f):
        if sinks is None:
            return f
        return lambda bi, p, i, sink_ref: f(p, bi, i)

    in_specs = [
        pl.BlockSpec((1, qb, tq * n_sub, LANES), im(lambda p, bi, i: (bi, q_col + p, i, 0))),
        pl.BlockSpec((1, qb, s, LANES), im(lambda p, bi, i: (bi, k_col + (0 if shared_kv else p), 0, 0))),
        pl.BlockSpec((1, 1, s, LANES), im(lambda p, bi, i: (bi, v_col + (0 if shared_kv else p), 0, 0))),
    ]
    args = [q, k, v]
    if window is None:
        in_specs.append(pl.BlockSpec((1, qb, tq, LANES),
                                     lambda p, bi, i: (bi, q_col + p, jnp.minimum(i + 1, nq - 1), 0)))
        args.append(q)
        scratch = [pltpu.VMEM((VT_ROWS, s), BF16), pltpu.VMEM((1, 2 * tq), F32),
                   pltpu.VMEM((VT_ROWS, 2 * tq), F32),
                   pltpu.VMEM((tk, 2 * tq), F32), pltpu.VMEM((tk, 2 * tq), F32)]
    else:
        pad = window - tq
        scratch = [pltpu.VMEM((VT_ROWS, pad + s), BF16), pltpu.VMEM((pad + s, LANES), BF16),
                   pltpu.VMEM((2 * BAND_GROUP, window, 2 * tq), F32)]
        if variant == "swa":
            in_specs.append(pl.BlockSpec(extra.shape, lambda bi, p, i, sink_ref: (0, 0, 0, 0),
                                         pipeline_mode=pl.Buffered(1)))
        else:
            in_specs.append(pl.BlockSpec((1,) + extra.shape[1:], lambda p, bi, i: (p, 0, 0, 0)))
        args.append(extra)
    if variant == "fox":
        in_specs.append(pl.BlockSpec((1, s, LANES), lambda p, bi, i: (bi, 0, 0)))
        args.append(extra)
        scratch.append(pltpu.VMEM((s, 2 * LANES), F32))
    out_spec = pl.BlockSpec((1, 1, tq * n_sub, LANES), im(lambda p, bi, i: (bi, p, i, 0)))
    kern = functools.partial(_attn_kernel, variant=variant, window=window, seq=s, tq=tq, tk=tk,
                             n_sub=n_sub)
    params = pltpu.CompilerParams(dimension_semantics=("arbitrary",) * 3, vmem_limit_bytes=VMEM_LIMIT)
    out_shape = jax.ShapeDtypeStruct((b, n_pairs, s, LANES), BF16)
    if sinks is not None:
        grid_spec = pltpu.PrefetchScalarGridSpec(
            num_scalar_prefetch=1, grid=(b, n_pairs, nq), in_specs=in_specs, out_specs=out_spec,
            scratch_shapes=scratch)
        return pl.pallas_call(kern, grid_spec=grid_spec, out_shape=out_shape,
                              compiler_params=params, name="attn_" + variant)(sinks, *args)
    return pl.pallas_call(kern, grid=(n_pairs, b, nq), in_specs=in_specs, out_specs=out_spec,
                          out_shape=out_shape, scratch_shapes=scratch,
                          compiler_params=params, name="attn_" + variant)(*args)


def _lead_rows(window, n_var, v):
    return max(window - TQ - (n_var - 1 - v) * TQ, 0)


def _swa_bias():
    window, left = SWA_WINDOW_KEYS, SWA_LEFT_CHUNKS
    c = jnp.arange(window)[:, None]
    a = jnp.arange(TQ)[None, :]
    dist = a - c + (window - TQ)
    ahead = (a // CHUNK + left) - c // CHUNK
    slopes = jnp.exp2(-8.0 * jnp.arange(1, N_HEADS + 1, dtype=F32) / N_HEADS)
    bias = -slopes[:, None, None] * jnp.abs(dist).astype(F32)[None] * LOG2E
    bias = jnp.where(((ahead >= 0) & (ahead <= left))[None], bias, MASK)
    bias = jnp.concatenate([bias[:4], bias[4:]], axis=2)
    n_var = 2
    return jnp.stack([jnp.where(c[None] >= _lead_rows(window, n_var, v), bias, MASK)
                      for v in range(n_var)], axis=1)


_REL_ROWS = 384


_CK_VARIANTS = 3


def _ck_bias_kernel(tab_ref, out_ref):
    window = CK_WINDOW_KEYS
    span = TQ + window
    tab = tab_ref[0] * LOG2E
    hi = tab.astype(BF16)
    rem = tab - hi.astype(F32)
    mid = rem.astype(BF16)
    lo = (rem - mid.astype(F32)).astype(BF16)
    t = lax.broadcasted_iota(jnp.int32, (_REL_ROWS, span), 0)
    n = lax.broadcasted_iota(jnp.int32, (_REL_ROWS, span), 1)
    delta = jnp.where(n < TQ, n, n - span)
    idx = jnp.clip(delta + (window - TQ), -(CHUNK - 1), REL_MAX) + (CHUNK - 1)
    onehot = jnp.where(t == idx, 1.0, 0.0).astype(BF16)
    base = _dot(hi, onehot) + _dot(mid, onehot) + _dot(lo, onehot)
    c = lax.broadcasted_iota(jnp.int32, (window, TQ), 0)
    a = lax.broadcasted_iota(jnp.int32, (window, TQ), 1)
    valid = ((a >> 6) + CK_LEFT_CHUNKS - (c >> 6)).astype(jnp.uint32) <= CK_LEFT_CHUNKS
    for h in range(2):
        rows = jnp.broadcast_to(base[h:h + 1, :], (window, span))
        tile = jnp.where(valid, pltpu.roll(rows, 0, 1, stride=1, stride_axis=0)[:, :TQ], MASK)
        for v in range(_CK_VARIANTS):
            lead = _lead_rows(window, _CK_VARIANTS, v)
            out_ref[0, v, :, h * TQ:(h + 1) * TQ] = jnp.where(c >= lead, tile, MASK)


def _ck_bias(rel_bias):
    tab = jnp.pad(rel_bias.T, ((0, 0), (0, _REL_ROWS - rel_bias.shape[0])))
    tab = jnp.pad(tab.reshape(N_HEADS // 2, 2, _REL_ROWS), ((0, 0), (0, 6), (0, 0)))
    return pl.pallas_call(
        _ck_bias_kernel,
        grid=(N_HEADS // 2,),
        in_specs=[pl.BlockSpec((1, 8, _REL_ROWS), lambda p: (p, 0, 0))],
        out_specs=pl.BlockSpec((1, _CK_VARIANTS, CK_WINDOW_KEYS, 2 * TQ), lambda p: (p, 0, 0, 0)),
        out_shape=jax.ShapeDtypeStruct((N_HEADS // 2, _CK_VARIANTS, CK_WINDOW_KEYS, 2 * TQ), F32),
        compiler_params=pltpu.CompilerParams(dimension_semantics=("arbitrary",),
                                             vmem_limit_bytes=VMEM_LIMIT),
        name="ck_bias",
    )(tab)


def _out_ffn_kernel(x_ref, oa_ref, ob_ref, woa_ref, wob_ref, g1_ref, b1_ref,
                    wg_ref, wu_ref, wd_ref, g2_ref, b2_ref, y_ref):
    o_a = jnp.concatenate([oa_ref[0, j] for j in range(N_HEADS // 2)], axis=1)
    o_b = jnp.concatenate([ob_ref[0, j] for j in range(N_HEADS // 2)], axis=1)
    mix = _dot(o_a, woa_ref[...]) + _dot(o_b, wob_ref[...])
    x1 = _layer_norm(DEEPNORM_ALPHA * x_ref[...] + mix, g1_ref[...], b1_ref[...])
    x1b = x1.astype(BF16)
    gate = _dot(x1b, wg_ref[...])
    up = _dot(x1b, wu_ref[...])
    hidden = (gate * jax.nn.sigmoid(gate) * up).astype(BF16)
    ffn = _dot(hidden, wd_ref[...])
    y_ref[...] = _layer_norm(DEEPNORM_ALPHA * x1 + ffn, g2_ref[...], b2_ref[...])


def _out_ffn(x2, o_a, o_b, w_out_a, w_out_b, g1, b1, w_gate, w_up, w_down, g2, b2):
    t = x2.shape[0]
    d_ff = w_gate.shape[1]
    row = lambda w: pl.BlockSpec((TM_FFN, w), lambda i: (i, 0))
    steps_per_seq = o_a.shape[2] // TM_FFN
    mixer = pl.BlockSpec((1, N_HEADS // 2, TM_FFN, LANES),
                         lambda i: (i // steps_per_seq, 0, i % steps_per_seq, 0))
    vec = _const_spec((1, D_MODEL))
    return pl.pallas_call(
        _out_ffn_kernel,
        grid=(t // TM_FFN,),
        in_specs=[row(D_MODEL), mixer, mixer, _const_spec((512, D_MODEL)),
                  _const_spec((512, D_MODEL)), vec, vec, _const_spec((D_MODEL, d_ff)),
                  _const_spec((D_MODEL, d_ff)), _const_spec((d_ff, D_MODEL)), vec, vec],
        out_specs=row(D_MODEL),
        out_shape=jax.ShapeDtypeStruct((t, D_MODEL), F32),
        compiler_params=pltpu.CompilerParams(dimension_semantics=("arbitrary",),
                                             vmem_limit_bytes=VMEM_LIMIT),
        name="out_ffn",
    )(x2, o_a, o_b, w_out_a.astype(BF16), w_out_b.astype(BF16), g1.reshape(1, -1), b1.reshape(1, -1),
      w_gate.astype(BF16), w_up.astype(BF16), w_down.astype(BF16), g2.reshape(1, -1), b2.reshape(1, -1))


def kernel(x, ab_w_in, ab_q_norm, ab_w_uq, ab_kv_norm, ab_w_ukv, ab_sinks, ab_w_out,
           cd_w_in, cd_b_forget, cd_rel_bias, cd_w_out,
           ln1_g, ln1_b, ffn_w_gate, ffn_w_up, ffn_w_down, ln2_g, ln2_b):
    b, s, d = x.shape
    assert d == D_MODEL and TQ == TK
    assert s % TM == 0 and s % (2 * FULL_TILE) == 0 and s % (TQ * BAND_SUB_BLOCKS) == 0
    t = b * s
    x2 = x.reshape(t, d)

    qa, ka, va, qs, ks, vs = _proj_ab(x2, ab_w_in[0], ab_q_norm[0], ab_w_uq[0], ab_kv_norm[0],
                                      ab_w_ukv[0], s)
    o_a = _attention("mla", qa, ka, va, 0, 0, 0)
    o_b = _attention("swa", qs, ks, vs, 0, 0, 0, extra=_swa_bias(), sinks=ab_sinks[0] * LOG2E)
    w_out = ab_w_out[0]
    w_out_b = w_out[512:].reshape(2, 4, HEAD_DIM, D_MODEL).transpose(1, 0, 2, 3).reshape(512, D_MODEL)
    x2 = _out_ffn(x2, o_a, o_b, w_out[:512], w_out_b,
                  ln1_g[0], ln1_b[0], ffn_w_gate[0], ffn_w_up[0], ffn_w_down[0], ln2_g[0], ln2_b[0])

    qkv, cum = _proj_cd(x2, cd_w_in[0], cd_b_forget[0], s)
    o_c = _attention("fox", qkv, qkv, qkv, 0, 4, 8, extra=cum.reshape(b, s, LANES))
    o_d = _attention("ck", qkv, qkv, qkv, 12, 16, 20, extra=_ck_bias(cd_rel_bias[0]))
    w_out = cd_w_out[0]
    x2 = _out_ffn(x2, o_c, o_d, w_out[:512], w_out[512:],
                  ln1_g[1], ln1_b[1], ffn_w_gate[1], ffn_w_up[1], ffn_w_down[1], ln2_g[1], ln2_b[1])
    return x2.reshape(b, s, d)
```

```python
import functools
import math

import numpy as np
import jax
import jax.numpy as jnp
from jax import lax
from jax.experimental import pallas as pl
from jax.experimental.pallas import tpu as pltpu

D_MODEL = 1024
CHUNK = 64
HEAD_DIM = 64
N_HEADS = 8
LANES = 128
LN_EPS = 1e-5
RMS_EPS = 1e-6
MASK = -1e30
LOG2E = math.log2(math.e)

MLA_Q_RANK = 384
MLA_KV_RANK = 256
MLA_NOPE = 64
MLA_ROPE = 32
ROPE_THETA = 10000.0
SWA_KV_HEADS = 2
SWA_LEFT_CHUNKS = 2
CK_LEFT_CHUNKS = 8
REL_MAX = 256
DEPTH = 2
DEEPNORM_ALPHA = (2 * DEPTH) ** 0.25

TQ = 256
TK = 256
BAND_SUB_BLOCKS = 16
BAND_GROUP = 2
SWA_WINDOW_KEYS = SWA_LEFT_CHUNKS * CHUNK + TQ
CK_WINDOW_KEYS = CK_LEFT_CHUNKS * CHUNK + TQ
FULL_TILE = 512
TM = 1024
TM_FFN = 512
CUMSUM_ROWS = 128
VT_ROWS = LANES + 16
VMEM_LIMIT = 56 * 1024 * 1024

BF16 = jnp.bfloat16
F32 = jnp.float32


def _dot(a, b):
    return jnp.dot(a, b, preferred_element_type=F32)


def _dot_nt(a, b):
    return lax.dot_general(a, b, (((1,), (1,)), ((), ())), preferred_element_type=F32)


def _layer_norm(y, g, b):
    mu = jnp.mean(y, axis=-1, keepdims=True)
    yc = y - mu
    var = jnp.mean(yc * yc, axis=-1, keepdims=True)
    return yc * lax.rsqrt(var + LN_EPS) * g + b


def _rms_norm(c, g):
    return c * lax.rsqrt(jnp.mean(c * c, axis=-1, keepdims=True) + RMS_EPS) * g


def _block_major_spec(n_blocks, steps_per_seq):
    return pl.BlockSpec((1, n_blocks, TM, LANES),
                        lambda i: (i // steps_per_seq, 0, i % steps_per_seq, 0))


def _const_spec(shape):
    zeros = (0,) * len(shape)
    return pl.BlockSpec(shape, lambda *_: zeros, pipeline_mode=pl.Buffered(1))


_AB_CQ, _AB_CKV, _AB_QS, _AB_KS, _AB_VS, _AB_KR, _AB_COLS = 0, 384, 640, 1152, 1280, 1408, 1536


def _rope_mix(t, m1, m2):
    return t * m1 + pltpu.roll(t, 96, 1) * m2


def _proj_ab_kernel(x_ref, win_ref, qn_ref, wuq_ref, kvn_ref, wukv_ref,
                    m1q_ref, m2q_ref, m1k_ref, m2k_ref,
                    qa_ref, ka_ref, va_ref, qs_ref, ks_ref, vs_ref):
    proj = _dot(x_ref[...].astype(BF16), win_ref[...])
    c_q = _rms_norm(proj[:, _AB_CQ:_AB_CKV], qn_ref[...])
    qf = _dot(c_q.astype(BF16), wuq_ref[...])
    c_kv = _rms_norm(proj[:, _AB_CKV:_AB_QS], kvn_ref[...])
    kvf = _dot(c_kv.astype(BF16), wukv_ref[...])
    k_rope = _rope_mix(proj[:, _AB_KR:_AB_COLS], m1k_ref[...], m2k_ref[...])
    m1q, m2q = m1q_ref[...], m2q_ref[...]
    for h in range(N_HEADS):
        sl = slice(h * LANES, (h + 1) * LANES)
        qa_ref[0, h] = _rope_mix(qf[:, sl], m1q, m2q).astype(BF16)
        ka_ref[0, h] = (kvf[:, sl] + k_rope).astype(BF16)
    for j in range(N_HEADS // 2):
        sl = slice(j * LANES, (j + 1) * LANES)
        va_ref[0, j] = kvf[:, N_HEADS * LANES:][:, sl].astype(BF16)
        qs_ref[0, j] = (proj[:, _AB_QS:_AB_KS][:, sl] * (HEAD_DIM ** -0.5 * LOG2E)).astype(BF16)
    ks_ref[0, 0] = proj[:, _AB_KS:_AB_VS].astype(BF16)
    vs_ref[0, 0] = proj[:, _AB_VS:_AB_KR].astype(BF16)


def _proj_ab(x2, w_in, q_norm, w_uq, kv_norm, w_ukv, seq):
    t = x2.shape[0]
    nsb = seq // TM
    c_q, c_kv, k_r, q_s, k_s, v_s = jnp.split(w_in, [384, 640, 672, 1184, 1312], axis=1)
    q_s = q_s.reshape(D_MODEL, 2, 4, HEAD_DIM).transpose(0, 2, 1, 3).reshape(D_MODEL, 512)
    kr_blk = jnp.concatenate([jnp.zeros((D_MODEL, 64), F32), k_r, k_r[:, 16:], k_r[:, :16]], axis=1)
    win_p = jnp.concatenate([c_q, c_kv, q_s, k_s, v_s, kr_blk], axis=1).astype(BF16)
    wq = w_uq.reshape(MLA_Q_RANK, N_HEADS, MLA_NOPE + MLA_ROPE)
    wuq_p = jnp.concatenate([wq, wq[..., 80:96], wq[..., 64:80]], axis=-1)
    wuq_p = wuq_p.reshape(MLA_Q_RANK, N_HEADS * LANES).astype(BF16)
    wkv = w_ukv.reshape(MLA_KV_RANK, N_HEADS, 2 * HEAD_DIM)
    k_pad = jnp.concatenate([wkv[..., :64], jnp.zeros_like(wkv[..., :64])], axis=-1)
    wukv_p = jnp.concatenate([k_pad.reshape(MLA_KV_RANK, N_HEADS * LANES),
                              wkv[..., 64:].reshape(MLA_KV_RANK, 512)], axis=1).astype(BF16)
    inv = ROPE_THETA ** (-jnp.arange(0, MLA_ROPE, 2, dtype=F32) / MLA_ROPE)
    ang = jnp.arange(seq, dtype=F32)[:, None] * inv[None, :]
    cos, sin = jnp.cos(ang), jnp.sin(ang)
    z64, z32, o64 = jnp.zeros((seq, 64), F32), jnp.zeros((seq, 32), F32), jnp.ones((seq, 64), F32)
    m1k = jnp.concatenate([z64, cos, cos, z32], axis=1)
    m2k = jnp.concatenate([z64, -sin, sin, z32], axis=1)
    q_scale = (MLA_NOPE + MLA_ROPE) ** -0.5 * LOG2E
    m1q = jnp.concatenate([o64, cos, cos, z32], axis=1) * q_scale
    m2q = m2k * q_scale

    row = lambda w: pl.BlockSpec((TM, w), lambda i: (i, 0))
    tab = pl.BlockSpec((TM, LANES), lambda i: (i % nsb, 0))
    out_blocks = (8, 8, 4, 4, 1, 1)
    return pl.pallas_call(
        _proj_ab_kernel,
        grid=(t // TM,),
        in_specs=[row(D_MODEL), _const_spec((D_MODEL, _AB_COLS)), _const_spec((1, MLA_Q_RANK)),
                  _const_spec((MLA_Q_RANK, 1024)), _const_spec((1, MLA_KV_RANK)),
                  _const_spec((MLA_KV_RANK, 1536)), tab, tab, tab, tab],
        out_specs=[_block_major_spec(n, nsb) for n in out_blocks],
        out_shape=[jax.ShapeDtypeStruct((t // seq, n, seq, LANES), BF16) for n in out_blocks],
        compiler_params=pltpu.CompilerParams(dimension_semantics=("arbitrary",),
                                             vmem_limit_bytes=VMEM_LIMIT),
        name="proj_ab",
    )(x2, win_p, q_norm.reshape(1, -1), wuq_p, kv_norm.reshape(1, -1), wukv_p, m1q, m2q, m1k, m2k)


def _proj_cd_kernel(x_ref, win_ref, bf_ref, qkv_ref, cum_ref, carry_ref, *, nsb):
    @pl.when(pl.program_id(0) % nsb == 0)
    def _():
        carry_ref[...] = jnp.zeros_like(carry_ref)

    proj = _dot(x_ref[...].astype(BF16), win_ref[...])
    q_scale = HEAD_DIM ** -0.5 * LOG2E
    for blk in range(3072 // LANES):
        part = proj[:, blk * LANES:(blk + 1) * LANES]
        is_query = (blk // 4) % 3 == 0
        qkv_ref[0, blk] = (part * q_scale if is_query else part).astype(BF16)
    z = proj[:, 3072:] + bf_ref[...]
    log_f = jnp.minimum(z, 0.0) - jnp.log(1.0 + jnp.exp(-jnp.abs(z)))
    r = lax.broadcasted_iota(jnp.int32, (CUMSUM_ROWS, CUMSUM_ROWS), 0)
    c = lax.broadcasted_iota(jnp.int32, (CUMSUM_ROWS, CUMSUM_ROWS), 1)
    tri = jnp.where(c <= r, 1.0, 0.0).astype(BF16)
    hi = log_f.astype(BF16)
    rem = log_f - hi.astype(F32)
    mid = rem.astype(BF16)
    lo = (rem - mid.astype(F32)).astype(BF16)
    carry = carry_ref[...]
    for blk in range(TM // CUMSUM_ROWS):
        rows = slice(blk * CUMSUM_ROWS, (blk + 1) * CUMSUM_ROWS)
        cum = _dot(tri, hi[rows]) + _dot(tri, mid[rows]) + _dot(tri, lo[rows]) + carry
        cum_ref[rows, :] = cum
        carry = cum[CUMSUM_ROWS - 1:CUMSUM_ROWS, :]
    carry_ref[...] = carry


def _proj_cd(x2, w_in, b_forget, seq):
    t = x2.shape[0]
    q_f, k_f, v_f, f_l, q_c, k_c, v_c = jnp.split(w_in, [512, 1024, 1536, 1544, 2056, 2568], axis=1)
    f_pad = jnp.concatenate([f_l, jnp.zeros((D_MODEL, LANES - N_HEADS), F32)], axis=1)
    win_p = jnp.concatenate([q_f, k_f, v_f, q_c, k_c, v_c, f_pad], axis=1).astype(BF16)
    bf = jnp.concatenate([b_forget, jnp.zeros((LANES - N_HEADS,), F32)]).reshape(1, LANES)
    return pl.pallas_call(
        functools.partial(_proj_cd_kernel, nsb=seq // TM),
        grid=(t // TM,),
        in_specs=[pl.BlockSpec((TM, D_MODEL), lambda i: (i, 0)),
                  _const_spec((D_MODEL, 3072 + LANES)), _const_spec((1, LANES))],
        out_specs=[_block_major_spec(3072 // LANES, seq // TM),
                   pl.BlockSpec((TM, LANES), lambda i: (i, 0))],
        out_shape=[jax.ShapeDtypeStruct((t // seq, 3072 // LANES, seq, LANES), BF16),
                   jax.ShapeDtypeStruct((t, LANES), F32)],
        scratch_shapes=[pltpu.VMEM((1, LANES), F32)],
        compiler_params=pltpu.CompilerParams(dimension_semantics=("arbitrary",),
                                             vmem_limit_bytes=VMEM_LIMIT),
        name="proj_cd",
    )(x2, win_p, bf)


def _attn_kernel(*refs, variant, window, seq, tq, tk, n_sub):
    if variant == "mla":
        q_ref, k_ref, v_ref, qn_ref, o_ref, vt_ref, m_ref, acc_ref, sa_ref, sb_ref = refs
    elif variant == "fox":
        (q_ref, k_ref, v_ref, qn_ref, ck_ref, o_ref, vt_ref, m_ref, acc_ref, sa_ref, sb_ref,
         ckrep_ref) = refs
    elif variant == "swa":
        sink_ref, q_ref, k_ref, v_ref, bias_ref, o_ref, vt_ref, kpad_ref, sbuf_ref = refs
    else:
        q_ref, k_ref, v_ref, bias_ref, o_ref, vt_ref, kpad_ref, sbuf_ref = refs

    i = pl.program_id(2)
    banded = window is not None
    pad = window - tq if banded else 0

    def setup():
        for c in range(seq // 512):
            sl = slice(pad + c * 512, pad + (c + 1) * 512)
            vt_ref[0:LANES, sl] = v_ref[0, 0, c * 512:(c + 1) * 512, :].astype(F32).T.astype(BF16)
        vt_ref[LANES:VT_ROWS, :] = jnp.ones((VT_ROWS - LANES, pad + seq), BF16)
        if banded:
            vt_ref[0:LANES, 0:pad] = jnp.zeros((LANES, pad), BF16)
            kpad_ref[0:pad, :] = jnp.zeros((pad, LANES), BF16)
            kpad_ref[pad:pad + seq, :] = k_ref[0, 0]
        if variant == "fox":
            src = lax.broadcasted_iota(jnp.int32, (LANES, 2 * LANES), 0)
            dst = lax.broadcasted_iota(jnp.int32, (LANES, 2 * LANES), 1)
            pick = jnp.where(src == 2 * pl.program_id(0) + (dst >> 7), 1.0, 0.0).astype(BF16)
            pick3 = jnp.concatenate([pick, pick, pick], axis=0)
            for c in range(seq // 512):
                rows = slice(c * 512, (c + 1) * 512)
                cum = ck_ref[0, rows, :]
                hi = cum.astype(BF16)
                rem = cum - hi.astype(F32)
                mid = rem.astype(BF16)
                lo = (rem - mid.astype(F32)).astype(BF16)
                ckrep_ref[rows, :] = _dot(jnp.concatenate([hi, mid, lo], axis=1), pick3) * LOG2E

    sub = lax.broadcasted_iota(jnp.int32, (LANES, tq), 0)
    lane = lax.broadcasted_iota(jnp.int32, (tq, LANES), 1)

    def pair_rows(q_blk):
        zero = jnp.zeros_like(q_blk)
        return jnp.concatenate([jnp.where(lane < HEAD_DIM, q_blk, zero),
                                jnp.where(lane >= HEAD_DIM, q_blk, zero)], axis=0)

    def finish(acc, m, rows):
        l = acc[LANES:LANES + 1, :]
        o_t = acc[:LANES, :]
        if variant == "swa":
            pair = pl.program_id(0)
            col = lax.broadcasted_iota(jnp.int32, (1, 2 * tq), 1)
            sink = jnp.where(col < tq, sink_ref[pair], sink_ref[pair + N_HEADS // 2])
            m_fin = jnp.maximum(m, sink)
            scale = jnp.exp2(m - m_fin)
            l = l * scale + jnp.exp2(sink - m_fin)
            o_t = o_t * scale
        o_t = o_t / l
        o_ref[0, rows, :] = jnp.where(sub < HEAD_DIM, o_t[:, :tq], o_t[:, tq:]).T.astype(o_ref.dtype)

    if banded:
        pl.when(i == 0)(setup)
        n_var = bias_ref.shape[1]

        def band_scores(sb, s_ref):
            ib = i * n_sub + sb
            start = pl.multiple_of(ib * tq, tq)
            rows = pl.ds(pl.multiple_of(sb * tq, tq), tq)
            v = jnp.maximum(n_var - 1 - ib, 0)
            s_ref[...] = (_dot_nt(kpad_ref[pl.ds(start, window), :], pair_rows(q_ref[0, 0, rows, :]))
                          + bias_ref[0, v])

        def band_softmax(sb, s_ref):
            start = pl.multiple_of((i * n_sub + sb) * tq, tq)
            s = s_ref[...]
            m = jnp.max(s, axis=0, keepdims=True)
            p = jnp.exp2(s - m).astype(BF16)
            finish(_dot(vt_ref[:, pl.ds(start, window)], p), m,
                   pl.ds(pl.multiple_of(sb * tq, tq), tq))

        group = BAND_GROUP
        n_groups = n_sub // group

        def group_scores(g, parity):
            for r in range(group):
                band_scores(g * group + r, sbuf_ref.at[parity * group + r])

        def group_softmax(g, parity):
            for r in range(group):
                band_softmax(g * group + r, sbuf_ref.at[parity * group + r])

        group_scores(0, 0)

        def two_groups(u, carry):
            g = 2 * u
            group_scores(g + 1, 1)
            group_softmax(g, 0)
            group_scores(g + 2, 0)
            group_softmax(g + 1, 1)
            return carry

        lax.fori_loop(0, n_groups // 2 - 1, two_groups, None)
        group_scores(n_groups - 1, 1)
        group_softmax(n_groups - 2, 0)
        group_softmax(n_groups - 1, 1)
        return

    if variant == "mla":
        query_operands = lambda ref: (ref[0, 0], ref[0, 1])
    else:
        query_operands = lambda ref: pair_rows(ref[0, 0])

    m_ref[...] = jnp.full_like(m_ref, MASK)
    acc_ref[...] = jnp.zeros_like(acc_ref)

    first_half = [(h * tq, tk) for h in range(2)]
    second_half = [(h * tq + tk, tk) for h in range(2)]

    q_now = query_operands(q_ref)

    def scores(j, s_ref, slabs=((0, 2 * tq),), queries=q_now):
        start = pl.multiple_of(j * tk, tk)
        k_blocks = [k_ref[0, h, pl.ds(start, tk), :] for h in range(k_ref.shape[1])]
        for c0, w in slabs:
            cols = slice(c0, c0 + w)
            if variant == "mla":
                for h in range(2):
                    lo, hi = max(c0, h * tq), min(c0 + w, (h + 1) * tq)
                    if lo < hi:
                        q_h = queries[h][lo - h * tq:hi - h * tq, :]
                        s_ref[:, lo:hi] = _dot_nt(k_blocks[h], q_h)
                continue
            s = _dot_nt(k_blocks[0], queries[cols, :])
            if variant == "fox":
                ck_reps = [ckrep_ref[pl.ds(start, tk), (c // tq) * LANES:(c // tq + 1) * LANES]
                           for c in range(c0, c0 + w, LANES)]
                s = s - jnp.concatenate(ck_reps, axis=1)
            s_ref[:, cols] = s

    def consume(j, s_ref, slabs, masked=()):
        start = pl.multiple_of(j * tk, tk)
        v_t = vt_ref[:, pl.ds(start, tk)]
        for c0, w in slabs:
            cols = slice(c0, c0 + w)
            s = s_ref[:, cols]
            if (c0, w) in masked:
                key = lax.broadcasted_iota(jnp.int32, (tk, w), 0)
                qry = lax.broadcasted_iota(jnp.int32, (tk, w), 1)
                if variant == "mla":
                    keep = key <= (qry | (CHUNK - 1))
                else:
                    keep = key <= qry
                s = jnp.where(keep, s, MASK)
            m_prev = m_ref[:, cols]
            m_new = jnp.maximum(m_prev, jnp.max(s, axis=0, keepdims=True))
            alpha = jnp.exp2(m_prev - m_new)
            p = jnp.exp2(s - m_new).astype(BF16)
            acc_ref[:, cols] = alpha * acc_ref[:, cols] + _dot(v_t, p)
            m_ref[:, cols] = m_new

    everything = [(0, 2 * tq)]

    @pl.when(i == 0)
    def _():
        setup()
        scores(0, sa_ref)

    def two_blocks(u, carry):
        t = 2 * u
        scores(t + 1, sb_ref)
        consume(t, sa_ref, everything)
        scores(t + 2, sa_ref)
        consume(t + 1, sb_ref, everything)
        return carry

    lax.fori_loop(0, i, two_blocks, None)

    def diagonal_blocks(prefetch_next):
        q_next = query_operands(qn_ref) if prefetch_next else None
        scores(2 * i + 1, sb_ref, second_half)
        consume(2 * i, sa_ref, first_half, masked=first_half)
        if prefetch_next:
            scores(0, sa_ref, first_half, queries=q_next)
        consume(2 * i, sa_ref, second_half)
        if prefetch_next:
            scores(0, sa_ref, second_half, queries=q_next)
        consume(2 * i + 1, sb_ref, second_half, masked=second_half)

    last = pl.num_programs(2) - 1
    pl.when(i < last)(lambda: diagonal_blocks(True))
    pl.when(i == last)(lambda: diagonal_blocks(False))
    finish(acc_ref[...], m_ref[...], slice(0, tq))


def _attention(variant, q, k, v, q_col, k_col, v_col, *, extra=None, sinks=None):
    b, _, s, _ = q.shape
    n_pairs = N_HEADS // 2
    if variant in ("mla", "fox"):
        tq, tk, n_sub = 2 * FULL_TILE, FULL_TILE, 1
    else:
        tq, tk, n_sub = TQ, TK, BAND_SUB_BLOCKS
    nq = s // (tq * n_sub)
    window = {"mla": None, "fox": None, "swa": SWA_WINDOW_KEYS, "ck": CK_WINDOW_KEYS}[variant]
    qb = 2 if variant == "mla" else 1
    shared_kv = variant == "swa"

    def im(f):
        if sinks is None:
            return f
        return lambda p, bi, i, sink_ref: f(p, bi, i)

    in_specs = [
        pl.BlockSpec((1, qb, tq * n_sub, LANES), im(lambda p, bi, i: (bi, q_col + p, i, 0))),
        pl.BlockSpec((1, qb, s, LANES), im(lambda p, bi, i: (bi, k_col + (0 if shared_kv else p), 0, 0))),
        pl.BlockSpec((1, 1, s, LANES), im(lambda p, bi, i: (bi, v_col + (0 if shared_kv else p), 0, 0))),
    ]
    args = [q, k, v]
    if window is None:
        in_specs.append(pl.BlockSpec((1, qb, tq, LANES),
                                     lambda p, bi, i: (bi, q_col + p, jnp.minimum(i + 1, nq - 1), 0)))
        args.append(q)
        scratch = [pltpu.VMEM((VT_ROWS, s), BF16), pltpu.VMEM((1, 2 * tq), F32),
                   pltpu.VMEM((VT_ROWS, 2 * tq), F32),
                   pltpu.VMEM((tk, 2 * tq), F32), pltpu.VMEM((tk, 2 * tq), F32)]
    else:
        pad = window - tq
        scratch = [pltpu.VMEM((VT_ROWS, pad + s), BF16), pltpu.VMEM((pad + s, LANES), BF16),
                   pltpu.VMEM((2 * BAND_GROUP, window, 2 * tq), F32)]
        in_specs.append(pl.BlockSpec((1,) + extra.shape[1:], im(lambda p, bi, i: (p, 0, 0, 0))))
        args.append(extra)
    if variant == "fox":
        in_specs.append(pl.BlockSpec((1, s, LANES), lambda p, bi, i: (bi, 0, 0)))
        args.append(extra)
        scratch.append(pltpu.VMEM((s, 2 * LANES), F32))
    out_spec = pl.BlockSpec((1, tq * n_sub, LANES), im(lambda p, bi, i: (bi, i, p)))
    kern = functools.partial(_attn_kernel, variant=variant, window=window, seq=s, tq=tq, tk=tk,
                             n_sub=n_sub)
    params = pltpu.CompilerParams(dimension_semantics=("arbitrary",) * 3, vmem_limit_bytes=VMEM_LIMIT)
    out_shape = jax.ShapeDtypeStruct((b, s, n_pairs * LANES), BF16)
    if sinks is not None:
        grid_spec = pltpu.PrefetchScalarGridSpec(
            num_scalar_prefetch=1, grid=(n_pairs, b, nq), in_specs=in_specs, out_specs=out_spec,
            scratch_shapes=scratch)
        return pl.pallas_call(kern, grid_spec=grid_spec, out_shape=out_shape,
                              compiler_params=params, name="attn_" + variant)(sinks, *args)
    return pl.pallas_call(kern, grid=(n_pairs, b, nq), in_specs=in_specs, out_specs=out_spec,
                          out_shape=out_shape, scratch_shapes=scratch,
                          compiler_params=params, name="attn_" + variant)(*args)


def _lead_rows(window, n_var, v):
    return max(window - TQ - (n_var - 1 - v) * TQ, 0)


def _swa_bias():
    window, left = SWA_WINDOW_KEYS, SWA_LEFT_CHUNKS
    c = jnp.arange(window)[:, None]
    a = jnp.arange(TQ)[None, :]
    dist = a - c + (window - TQ)
    ahead = (a // CHUNK + left) - c // CHUNK
    slopes = jnp.exp2(-8.0 * jnp.arange(1, N_HEADS + 1, dtype=F32) / N_HEADS)
    bias = -slopes[:, None, None] * jnp.abs(dist).astype(F32)[None] * LOG2E
    bias = jnp.where(((ahead >= 0) & (ahead <= left))[None], bias, MASK)
    bias = jnp.concatenate([bias[:4], bias[4:]], axis=2)
    n_var = 2
    return jnp.stack([jnp.where(c[None] >= _lead_rows(window, n_var, v), bias, MASK)
                      for v in range(n_var)], axis=1)


_REL_ROWS = 384


_CK_VARIANTS = 3


def _ck_bias_kernel(tab_ref, out_ref):
    window = CK_WINDOW_KEYS
    span = TQ + window
    tab = tab_ref[0] * LOG2E
    hi = tab.astype(BF16)
    rem = tab - hi.astype(F32)
    mid = rem.astype(BF16)
    lo = (rem - mid.astype(F32)).astype(BF16)
    t = lax.broadcasted_iota(jnp.int32, (_REL_ROWS, span), 0)
    n = lax.broadcasted_iota(jnp.int32, (_REL_ROWS, span), 1)
    delta = jnp.where(n < TQ, n, n - span)
    idx = jnp.clip(delta + (window - TQ), -(CHUNK - 1), REL_MAX) + (CHUNK - 1)
    onehot = jnp.where(t == idx, 1.0, 0.0).astype(BF16)
    base = _dot(hi, onehot) + _dot(mid, onehot) + _dot(lo, onehot)
    c = lax.broadcasted_iota(jnp.int32, (window, TQ), 0)
    a = lax.broadcasted_iota(jnp.int32, (window, TQ), 1)
    valid = ((a >> 6) + CK_LEFT_CHUNKS - (c >> 6)).astype(jnp.uint32) <= CK_LEFT_CHUNKS
    for h in range(2):
        rows = jnp.broadcast_to(base[h:h + 1, :], (window, span))
        tile = jnp.where(valid, pltpu.roll(rows, 0, 1, stride=1, stride_axis=0)[:, :TQ], MASK)
        for v in range(_CK_VARIANTS):
            lead = _lead_rows(window, _CK_VARIANTS, v)
            out_ref[0, v, :, h * TQ:(h + 1) * TQ] = jnp.where(c >= lead, tile, MASK)


def _ck_bias(rel_bias):
    tab = jnp.pad(rel_bias.T, ((0, 0), (0, _REL_ROWS - rel_bias.shape[0])))
    tab = jnp.pad(tab.reshape(N_HEADS // 2, 2, _REL_ROWS), ((0, 0), (0, 6), (0, 0)))
    return pl.pallas_call(
        _ck_bias_kernel,
        grid=(N_HEADS // 2,),
        in_specs=[pl.BlockSpec((1, 8, _REL_ROWS), lambda p: (p, 0, 0))],
        out_specs=pl.BlockSpec((1, _CK_VARIANTS, CK_WINDOW_KEYS, 2 * TQ), lambda p: (p, 0, 0, 0)),
        out_shape=jax.ShapeDtypeStruct((N_HEADS // 2, _CK_VARIANTS, CK_WINDOW_KEYS, 2 * TQ), F32),
        compiler_params=pltpu.CompilerParams(dimension_semantics=("arbitrary",),
                                             vmem_limit_bytes=VMEM_LIMIT),
        name="ck_bias",
    )(tab)


def _out_ffn_kernel(x_ref, oa_ref, ob_ref, woa_ref, wob_ref, g1_ref, b1_ref,
                    wg_ref, wu_ref, wd_ref, g2_ref, b2_ref, y_ref):
    mix = _dot(oa_ref[...], woa_ref[...]) + _dot(ob_ref[...], wob_ref[...])
    x1 = _layer_norm(DEEPNORM_ALPHA * x_ref[...] + mix, g1_ref[...], b1_ref[...])
    x1b = x1.astype(BF16)
    gate = _dot(x1b, wg_ref[...])
    up = _dot(x1b, wu_ref[...])
    hidden = (gate * jax.nn.sigmoid(gate) * up).astype(BF16)
    ffn = _dot(hidden, wd_ref[...])
    y_ref[...] = _layer_norm(DEEPNORM_ALPHA * x1 + ffn, g2_ref[...], b2_ref[...])


def _out_ffn(x2, o_a, o_b, w_out_a, w_out_b, g1, b1, w_gate, w_up, w_down, g2, b2):
    t = x2.shape[0]
    d_ff = w_gate.shape[1]
    row = lambda w: pl.BlockSpec((TM_FFN, w), lambda i: (i, 0))
    vec = _const_spec((1, D_MODEL))
    return pl.pallas_call(
        _out_ffn_kernel,
        grid=(t // TM_FFN,),
        in_specs=[row(D_MODEL), row(512), row(512), _const_spec((512, D_MODEL)),
                  _const_spec((512, D_MODEL)), vec, vec, _const_spec((D_MODEL, d_ff)),
                  _const_spec((D_MODEL, d_ff)), _const_spec((d_ff, D_MODEL)), vec, vec],
        out_specs=row(D_MODEL),
        out_shape=jax.ShapeDtypeStruct((t, D_MODEL), F32),
        compiler_params=pltpu.CompilerParams(dimension_semantics=("arbitrary",),
                                             vmem_limit_bytes=VMEM_LIMIT),
        name="out_ffn",
    )(x2, o_a, o_b, w_out_a.astype(BF16), w_out_b.astype(BF16), g1.reshape(1, -1), b1.reshape(1, -1),
      w_gate.astype(BF16), w_up.astype(BF16), w_down.astype(BF16), g2.reshape(1, -1), b2.reshape(1, -1))


def kernel(x, ab_w_in, ab_q_norm, ab_w_uq, ab_kv_norm, ab_w_ukv, ab_sinks, ab_w_out,
           cd_w_in, cd_b_forget, cd_rel_bias, cd_w_out,
           ln1_g, ln1_b, ffn_w_gate, ffn_w_up, ffn_w_down, ln2_g, ln2_b):
    b, s, d = x.shape
    assert d == D_MODEL and TQ == TK
    assert s % TM == 0 and s % (2 * FULL_TILE) == 0 and s % (TQ * BAND_SUB_BLOCKS) == 0
    t = b * s
    x2 = x.reshape(t, d)

    qa, ka, va, qs, ks, vs = _proj_ab(x2, ab_w_in[0], ab_q_norm[0], ab_w_uq[0], ab_kv_norm[0],
                                      ab_w_ukv[0], s)
    o_a = _attention("mla", qa, ka, va, 0, 0, 0)
    o_b = _attention("swa", qs, ks, vs, 0, 0, 0, extra=_swa_bias(), sinks=ab_sinks[0] * LOG2E)
    w_out = ab_w_out[0]
    w_out_b = w_out[512:].reshape(2, 4, HEAD_DIM, D_MODEL).transpose(1, 0, 2, 3).reshape(512, D_MODEL)
    x2 = _out_ffn(x2, o_a.reshape(t, 512), o_b.reshape(t, 512), w_out[:512], w_out_b,
                  ln1_g[0], ln1_b[0], ffn_w_gate[0], ffn_w_up[0], ffn_w_down[0], ln2_g[0], ln2_b[0])

    qkv, cum = _proj_cd(x2, cd_w_in[0], cd_b_forget[0], s)
    o_c = _attention("fox", qkv, qkv, qkv, 0, 4, 8, extra=cum.reshape(b, s, LANES))
    o_d = _attention("ck", qkv, qkv, qkv, 12, 16, 20, extra=_ck_bias(cd_rel_bias[0]))
    w_out = cd_w_out[0]
    x2 = _out_ffn(x2, o_c.reshape(t, 512), o_d.reshape(t, 512), w_out[:512], w_out[512:],
                  ln1_g[1], ln1_b[1], ffn_w_gate[1], ffn_w_up[1], ffn_w_down[1], ln2_g[1], ln2_b[1])
    return x2.reshape(b, s, d)
```

```python
import functools
import math

import numpy as np
import jax
import jax.numpy as jnp
from jax import lax
from jax.experimental import pallas as pl
from jax.experimental.pallas import tpu as pltpu

D_MODEL = 1024
CHUNK = 64
HEAD_DIM = 64
N_HEADS = 8
LANES = 128
LN_EPS = 1e-5
RMS_EPS = 1e-6
MASK = -1e30
LOG2E = math.log2(math.e)

MLA_Q_RANK = 384
MLA_KV_RANK = 256
MLA_NOPE = 64
MLA_ROPE = 32
ROPE_THETA = 10000.0
SWA_KV_HEADS = 2
SWA_LEFT_CHUNKS = 2
CK_LEFT_CHUNKS = 8
REL_MAX = 256
DEPTH = 2
DEEPNORM_ALPHA = (2 * DEPTH) ** 0.25

TQ = 256
TK = 256
BAND_SUB_BLOCKS = 16
BAND_GROUP = 2
SWA_WINDOW_KEYS = SWA_LEFT_CHUNKS * CHUNK + TQ
CK_WINDOW_KEYS = CK_LEFT_CHUNKS * CHUNK + TQ
FULL_TILE = 512
TM = 1024
TM_FFN = 512
CUMSUM_ROWS = 128
VT_ROWS = LANES + 16
VMEM_LIMIT = 56 * 1024 * 1024
ATTN_VMEM_LIMIT = 40 * 1024 * 1024

BF16 = jnp.bfloat16
F32 = jnp.float32


def _dot(a, b):
    return jnp.dot(a, b, preferred_element_type=F32)


def _dot_nt(a, b):
    return lax.dot_general(a, b, (((1,), (1,)), ((), ())), preferred_element_type=F32)


def _layer_norm(y, g, b):
    mu = jnp.mean(y, axis=-1, keepdims=True)
    yc = y - mu
    var = jnp.mean(yc * yc, axis=-1, keepdims=True)
    return yc * lax.rsqrt(var + LN_EPS) * g + b


def _rms_norm(c, g):
    return c * lax.rsqrt(jnp.mean(c * c, axis=-1, keepdims=True) + RMS_EPS) * g


def _block_major_spec(n_blocks, steps_per_seq):
    return pl.BlockSpec((1, n_blocks, TM, LANES),
                        lambda i: (i // steps_per_seq, 0, i % steps_per_seq, 0))


def _const_spec(shape):
    zeros = (0,) * len(shape)
    return pl.BlockSpec(shape, lambda *_: zeros, pipeline_mode=pl.Buffered(1))


_AB_CQ, _AB_CKV, _AB_QS, _AB_KS, _AB_VS, _AB_KR, _AB_COLS = 0, 384, 640, 1152, 1280, 1408, 1536


def _rope_mix(t, m1, m2):
    return t * m1 + pltpu.roll(t, 96, 1) * m2


def _proj_ab_kernel(x_ref, win_ref, qn_ref, wuq_ref, kvn_ref, wukv_ref,
                    m1q_ref, m2q_ref, m1k_ref, m2k_ref,
                    qa_ref, ka_ref, va_ref, qs_ref, ks_ref, vs_ref):
    proj = _dot(x_ref[...].astype(BF16), win_ref[...])
    c_q = _rms_norm(proj[:, _AB_CQ:_AB_CKV], qn_ref[...])
    qf = _dot(c_q.astype(BF16), wuq_ref[...])
    c_kv = _rms_norm(proj[:, _AB_CKV:_AB_QS], kvn_ref[...])
    kvf = _dot(c_kv.astype(BF16), wukv_ref[...])
    k_rope = _rope_mix(proj[:, _AB_KR:_AB_COLS], m1k_ref[...], m2k_ref[...])
    m1q, m2q = m1q_ref[...], m2q_ref[...]
    for h in range(N_HEADS):
        sl = slice(h * LANES, (h + 1) * LANES)
        qa_ref[0, h] = _rope_mix(qf[:, sl], m1q, m2q).astype(BF16)
        ka_ref[0, h] = (kvf[:, sl] + k_rope).astype(BF16)
    for j in range(N_HEADS // 2):
        sl = slice(j * LANES, (j + 1) * LANES)
        va_ref[0, j] = kvf[:, N_HEADS * LANES:][:, sl].astype(BF16)
        qs_ref[0, j] = (proj[:, _AB_QS:_AB_KS][:, sl] * (HEAD_DIM ** -0.5 * LOG2E)).astype(BF16)
    ks_ref[0, 0] = proj[:, _AB_KS:_AB_VS].astype(BF16)
    vs_ref[0, 0] = proj[:, _AB_VS:_AB_KR].astype(BF16)


def _proj_ab(x2, w_in, q_norm, w_uq, kv_norm, w_ukv, seq):
    t = x2.shape[0]
    nsb = seq // TM
    c_q, c_kv, k_r, q_s, k_s, v_s = jnp.split(w_in, [384, 640, 672, 1184, 1312], axis=1)
    q_s = q_s.reshape(D_MODEL, 2, 4, HEAD_DIM).transpose(0, 2, 1, 3).reshape(D_MODEL, 512)
    kr_blk = jnp.concatenate([jnp.zeros((D_MODEL, 64), F32), k_r, k_r[:, 16:], k_r[:, :16]], axis=1)
    win_p = jnp.concatenate([c_q, c_kv, q_s, k_s, v_s, kr_blk], axis=1).astype(BF16)
    wq = w_uq.reshape(MLA_Q_RANK, N_HEADS, MLA_NOPE + MLA_ROPE)
    wuq_p = jnp.concatenate([wq, wq[..., 80:96], wq[..., 64:80]], axis=-1)
    wuq_p = wuq_p.reshape(MLA_Q_RANK, N_HEADS * LANES).astype(BF16)
    wkv = w_ukv.reshape(MLA_KV_RANK, N_HEADS, 2 * HEAD_DIM)
    k_pad = jnp.concatenate([wkv[..., :64], jnp.zeros_like(wkv[..., :64])], axis=-1)
    wukv_p = jnp.concatenate([k_pad.reshape(MLA_KV_RANK, N_HEADS * LANES),
                              wkv[..., 64:].reshape(MLA_KV_RANK, 512)], axis=1).astype(BF16)
    inv = ROPE_THETA ** (-jnp.arange(0, MLA_ROPE, 2, dtype=F32) / MLA_ROPE)
    ang = jnp.arange(seq, dtype=F32)[:, None] * inv[None, :]
    cos, sin = jnp.cos(ang), jnp.sin(ang)
    z64, z32, o64 = jnp.zeros((seq, 64), F32), jnp.zeros((seq, 32), F32), jnp.ones((seq, 64), F32)
    m1k = jnp.concatenate([z64, cos, cos, z32], axis=1)
    m2k = jnp.concatenate([z64, -sin, sin, z32], axis=1)
    q_scale = (MLA_NOPE + MLA_ROPE) ** -0.5 * LOG2E
    m1q = jnp.concatenate([o64, cos, cos, z32], axis=1) * q_scale
    m2q = m2k * q_scale

    row = lambda w: pl.BlockSpec((TM, w), lambda i: (i, 0))
    tab = pl.BlockSpec((TM, LANES), lambda i: (i % nsb, 0))
    out_blocks = (8, 8, 4, 4, 1, 1)
    return pl.pallas_call(
        _proj_ab_kernel,
        grid=(t // TM,),
        in_specs=[row(D_MODEL), _const_spec((D_MODEL, _AB_COLS)), _const_spec((1, MLA_Q_RANK)),
                  _const_spec((MLA_Q_RANK, 1024)), _const_spec((1, MLA_KV_RANK)),
                  _const_spec((MLA_KV_RANK, 1536)), tab, tab, tab, tab],
        out_specs=[_block_major_spec(n, nsb) for n in out_blocks],
        out_shape=[jax.ShapeDtypeStruct((t // seq, n, seq, LANES), BF16) for n in out_blocks],
        compiler_params=pltpu.CompilerParams(dimension_semantics=("arbitrary",),
                                             vmem_limit_bytes=VMEM_LIMIT),
        name="proj_ab",
    )(x2, win_p, q_norm.reshape(1, -1), wuq_p, kv_norm.reshape(1, -1), wukv_p, m1q, m2q, m1k, m2k)


def _proj_cd_kernel(x_ref, win_ref, bf_ref, qkv_ref, cum_ref, carry_ref, *, nsb):
    @pl.when(pl.program_id(0) % nsb == 0)
    def _():
        carry_ref[...] = jnp.zeros_like(carry_ref)

    proj = _dot(x_ref[...].astype(BF16), win_ref[...])
    q_scale = HEAD_DIM ** -0.5 * LOG2E
    for blk in range(3072 // LANES):
        part = proj[:, blk * LANES:(blk + 1) * LANES]
        is_query = (blk // 4) % 3 == 0
        qkv_ref[0, blk] = (part * q_scale if is_query else part).astype(BF16)
    z = proj[:, 3072:] + bf_ref[...]
    log_f = jnp.minimum(z, 0.0) - jnp.log(1.0 + jnp.exp(-jnp.abs(z)))
    r = lax.broadcasted_iota(jnp.int32, (CUMSUM_ROWS, CUMSUM_ROWS), 0)
    c = lax.broadcasted_iota(jnp.int32, (CUMSUM_ROWS, CUMSUM_ROWS), 1)
    tri = jnp.where(c <= r, 1.0, 0.0).astype(BF16)
    hi = log_f.astype(BF16)
    rem = log_f - hi.astype(F32)
    mid = rem.astype(BF16)
    lo = (rem - mid.astype(F32)).astype(BF16)
    carry = carry_ref[...]
    for blk in range(TM // CUMSUM_ROWS):
        rows = slice(blk * CUMSUM_ROWS, (blk + 1) * CUMSUM_ROWS)
        cum = _dot(tri, hi[rows]) + _dot(tri, mid[rows]) + _dot(tri, lo[rows]) + carry
        cum_ref[rows, :] = cum
        carry = cum[CUMSUM_ROWS - 1:CUMSUM_ROWS, :]
    carry_ref[...] = carry


def _proj_cd(x2, w_in, b_forget, seq):
    t = x2.shape[0]
    q_f, k_f, v_f, f_l, q_c, k_c, v_c = jnp.split(w_in, [512, 1024, 1536, 1544, 2056, 2568], axis=1)
    f_pad = jnp.concatenate([f_l, jnp.zeros((D_MODEL, LANES - N_HEADS), F32)], axis=1)
    win_p = jnp.concatenate([q_f, k_f, v_f, q_c, k_c, v_c, f_pad], axis=1).astype(BF16)
    bf = jnp.concatenate([b_forget, jnp.zeros((LANES - N_HEADS,), F32)]).reshape(1, LANES)
    return pl.pallas_call(
        functools.partial(_proj_cd_kernel, nsb=seq // TM),
        grid=(t // TM,),
        in_specs=[pl.BlockSpec((TM, D_MODEL), lambda i: (i, 0)),
                  _const_spec((D_MODEL, 3072 + LANES)), _const_spec((1, LANES))],
        out_specs=[_block_major_spec(3072 // LANES, seq // TM),
                   pl.BlockSpec((TM, LANES), lambda i: (i, 0))],
        out_shape=[jax.ShapeDtypeStruct((t // seq, 3072 // LANES, seq, LANES), BF16),
                   jax.ShapeDtypeStruct((t, LANES), F32)],
        scratch_shapes=[pltpu.VMEM((1, LANES), F32)],
        compiler_params=pltpu.CompilerParams(dimension_semantics=("arbitrary",),
                                             vmem_limit_bytes=VMEM_LIMIT),
        name="proj_cd",
    )(x2, win_p, bf)


def _attn_kernel(*refs, variant, window, seq, tq, tk, n_sub):
    if variant == "mla":
        q_ref, k_ref, v_ref, qn_ref, o_ref, vt_ref, m_ref, acc_ref, sa_ref, sb_ref = refs
    elif variant == "fox":
        (q_ref, k_ref, v_ref, qn_ref, ck_ref, o_ref, vt_ref, m_ref, acc_ref, sa_ref, sb_ref,
         ckrep_ref) = refs
    elif variant == "swa":
        sink_ref, q_ref, k_ref, v_ref, bias_ref, o_ref, vt_ref, kpad_ref, sbuf_ref = refs
    else:
        q_ref, k_ref, v_ref, bias_ref, o_ref, vt_ref, kpad_ref, sbuf_ref = refs

    i = pl.program_id(2)
    banded = window is not None
    pad = window - tq if banded else 0

    def setup():
        for c in range(seq // 512):
            sl = slice(pad + c * 512, pad + (c + 1) * 512)
            vt_ref[0:LANES, sl] = v_ref[0, 0, c * 512:(c + 1) * 512, :].astype(F32).T.astype(BF16)
        vt_ref[LANES:VT_ROWS, :] = jnp.ones((VT_ROWS - LANES, pad + seq), BF16)
        if banded:
            vt_ref[0:LANES, 0:pad] = jnp.zeros((LANES, pad), BF16)
            kpad_ref[0:pad, :] = jnp.zeros((pad, LANES), BF16)
            kpad_ref[pad:pad + seq, :] = k_ref[0, 0]
        if variant == "fox":
            src = lax.broadcasted_iota(jnp.int32, (LANES, 2 * LANES), 0)
            dst = lax.broadcasted_iota(jnp.int32, (LANES, 2 * LANES), 1)
            pick = jnp.where(src == 2 * pl.program_id(0) + (dst >> 7), 1.0, 0.0).astype(BF16)
            pick3 = jnp.concatenate([pick, pick, pick], axis=0)
            for c in range(seq // 512):
                rows = slice(c * 512, (c + 1) * 512)
                cum = ck_ref[0, rows, :]
                hi = cum.astype(BF16)
                rem = cum - hi.astype(F32)
                mid = rem.astype(BF16)
                lo = (rem - mid.astype(F32)).astype(BF16)
                ckrep_ref[rows, :] = _dot(jnp.concatenate([hi, mid, lo], axis=1), pick3) * LOG2E

    sub = lax.broadcasted_iota(jnp.int32, (LANES, tq), 0)
    lane = lax.broadcasted_iota(jnp.int32, (tq, LANES), 1)

    def pair_rows(q_blk):
        zero = jnp.zeros_like(q_blk)
        return jnp.concatenate([jnp.where(lane < HEAD_DIM, q_blk, zero),
                                jnp.where(lane >= HEAD_DIM, q_blk, zero)], axis=0)

    def finish(acc, m, rows):
        l = acc[LANES:LANES + 1, :]
        o_t = acc[:LANES, :]
        if variant == "swa":
            pair = pl.program_id(0)
            col = lax.broadcasted_iota(jnp.int32, (1, 2 * tq), 1)
            sink = jnp.where(col < tq, sink_ref[pair], sink_ref[pair + N_HEADS // 2])
            m_fin = jnp.maximum(m, sink)
            scale = jnp.exp2(m - m_fin)
            l = l * scale + jnp.exp2(sink - m_fin)
            o_t = o_t * scale
        o_t = o_t / l
        o_ref[0, rows, :] = jnp.where(sub < HEAD_DIM, o_t[:, :tq], o_t[:, tq:]).T.astype(o_ref.dtype)

    if banded:
        pl.when(i == 0)(setup)
        n_var = bias_ref.shape[1]

        def band_scores(sb, s_ref):
            ib = i * n_sub + sb
            start = pl.multiple_of(ib * tq, tq)
            rows = pl.ds(pl.multiple_of(sb * tq, tq), tq)
            v = jnp.maximum(n_var - 1 - ib, 0)
            s_ref[...] = (_dot_nt(kpad_ref[pl.ds(start, window), :], pair_rows(q_ref[0, 0, rows, :]))
                          + bias_ref[0, v])

        def band_softmax(sb, s_ref):
            start = pl.multiple_of((i * n_sub + sb) * tq, tq)
            s = s_ref[...]
            m = jnp.max(s, axis=0, keepdims=True)
            p = jnp.exp2(s - m).astype(BF16)
            finish(_dot(vt_ref[:, pl.ds(start, window)], p), m,
                   pl.ds(pl.multiple_of(sb * tq, tq), tq))

        group = BAND_GROUP
        n_groups = n_sub // group

        def group_scores(g, parity):
            for r in range(group):
                band_scores(g * group + r, sbuf_ref.at[parity * group + r])

        def group_softmax(g, parity):
            for r in range(group):
                band_softmax(g * group + r, sbuf_ref.at[parity * group + r])

        group_scores(0, 0)

        def two_groups(u, carry):
            g = 2 * u
            group_scores(g + 1, 1)
            group_softmax(g, 0)
            group_scores(g + 2, 0)
            group_softmax(g + 1, 1)
            return carry

        lax.fori_loop(0, n_groups // 2 - 1, two_groups, None)
        group_scores(n_groups - 1, 1)
        group_softmax(n_groups - 2, 0)
        group_softmax(n_groups - 1, 1)
        return

    if variant == "mla":
        query_operands = lambda ref: (ref[0, 0], ref[0, 1])
    else:
        query_operands = lambda ref: pair_rows(ref[0, 0])

    m_ref[...] = jnp.full_like(m_ref, MASK)
    acc_ref[...] = jnp.zeros_like(acc_ref)

    first_half = [(h * tq, tk) for h in range(2)]
    second_half = [(h * tq + tk, tk) for h in range(2)]

    q_now = query_operands(q_ref)

    def scores(j, s_ref, slabs=((0, 2 * tq),), queries=q_now):
        start = pl.multiple_of(j * tk, tk)
        k_blocks = [k_ref[0, h, pl.ds(start, tk), :] for h in range(k_ref.shape[1])]
        for c0, w in slabs:
            cols = slice(c0, c0 + w)
            if variant == "mla":
                for h in range(2):
                    lo, hi = max(c0, h * tq), min(c0 + w, (h + 1) * tq)
                    if lo < hi:
                        q_h = queries[h][lo - h * tq:hi - h * tq, :]
                        s_ref[:, lo:hi] = _dot_nt(k_blocks[h], q_h)
                continue
            s = _dot_nt(k_blocks[0], queries[cols, :])
            if variant == "fox":
                ck_reps = [ckrep_ref[pl.ds(start, tk), (c // tq) * LANES:(c // tq + 1) * LANES]
                           for c in range(c0, c0 + w, LANES)]
                s = s - jnp.concatenate(ck_reps, axis=1)
            s_ref[:, cols] = s

    def consume(j, s_ref, slabs, masked=()):
        start = pl.multiple_of(j * tk, tk)
        v_t = vt_ref[:, pl.ds(start, tk)]
        for c0, w in slabs:
            cols = slice(c0, c0 + w)
            s = s_ref[:, cols]
            if (c0, w) in masked:
                key = lax.broadcasted_iota(jnp.int32, (tk, w), 0)
                qry = lax.broadcasted_iota(jnp.int32, (tk, w), 1)
                if variant == "mla":
                    keep = key <= (qry | (CHUNK - 1))
                else:
                    keep = key <= qry
                s = jnp.where(keep, s, MASK)
            m_prev = m_ref[:, cols]
            m_new = jnp.maximum(m_prev, jnp.max(s, axis=0, keepdims=True))
            alpha = jnp.exp2(m_prev - m_new)
            p = jnp.exp2(s - m_new).astype(BF16)
            acc_ref[:, cols] = alpha * acc_ref[:, cols] + _dot(v_t, p)
            m_ref[:, cols] = m_new

    everything = [(0, 2 * tq)]

    @pl.when(i == 0)
    def _():
        setup()
        scores(0, sa_ref)

    def two_blocks(u, carry):
        t = 2 * u
        scores(t + 1, sb_ref)
        consume(t, sa_ref, everything)
        scores(t + 2, sa_ref)
        consume(t + 1, sb_ref, everything)
        return carry

    lax.fori_loop(0, i, two_blocks, None)

    def diagonal_blocks(prefetch_next):
        q_next = query_operands(qn_ref) if prefetch_next else None
        scores(2 * i + 1, sb_ref, second_half)
        consume(2 * i, sa_ref, first_half, masked=first_half)
        if prefetch_next:
            scores(0, sa_ref, first_half, queries=q_next)
        consume(2 * i, sa_ref, second_half)
        if prefetch_next:
            scores(0, sa_ref, second_half, queries=q_next)
        consume(2 * i + 1, sb_ref, second_half, masked=second_half)

    last = pl.num_programs(2) - 1
    pl.when(i < last)(lambda: diagonal_blocks(True))
    pl.when(i == last)(lambda: diagonal_blocks(False))
    finish(acc_ref[...], m_ref[...], slice(0, tq))


def _attention(variant, q, k, v, q_col, k_col, v_col, *, extra=None, sinks=None):
    b, _, s, _ = q.shape
    n_pairs = N_HEADS // 2
    if variant in ("mla", "fox"):
        tq, tk, n_sub = 2 * FULL_TILE, FULL_TILE, 1
    else:
        tq, tk, n_sub = TQ, TK, BAND_SUB_BLOCKS
    nq = s // (tq * n_sub)
    window = {"mla": None, "fox": None, "swa": SWA_WINDOW_KEYS, "ck": CK_WINDOW_KEYS}[variant]
    qb = 2 if variant == "mla" else 1
    shared_kv = variant == "swa"

    def im(f):
        if sinks is None:
            return f
        return lambda p, bi, i, sink_ref: f(p, bi, i)

    in_specs = [
        pl.BlockSpec((1, qb, tq * n_sub, LANES), im(lambda p, bi, i: (bi, q_col + p, i, 0))),
        pl.BlockSpec((1, qb, s, LANES), im(lambda p, bi, i: (bi, k_col + (0 if shared_kv else p), 0, 0))),
        pl.BlockSpec((1, 1, s, LANES), im(lambda p, bi, i: (bi, v_col + (0 if shared_kv else p), 0, 0))),
    ]
    args = [q, k, v]
    if window is None:
        in_specs.append(pl.BlockSpec((1, qb, tq, LANES),
                                     lambda p, bi, i: (bi, q_col + p, jnp.minimum(i + 1, nq - 1), 0)))
        args.append(q)
        scratch = [pltpu.VMEM((VT_ROWS, s), BF16), pltpu.VMEM((1, 2 * tq), F32),
                   pltpu.VMEM((VT_ROWS, 2 * tq), F32),
                   pltpu.VMEM((tk, 2 * tq), F32), pltpu.VMEM((tk, 2 * tq), F32)]
    else:
        pad = window - tq
        scratch = [pltpu.VMEM((VT_ROWS, pad + s), BF16), pltpu.VMEM((pad + s, LANES), BF16),
                   pltpu.VMEM((2 * BAND_GROUP, window, 2 * tq), F32)]
        in_specs.append(pl.BlockSpec((1,) + extra.shape[1:], im(lambda p, bi, i: (p, 0, 0, 0))))
        args.append(extra)
    if variant == "fox":
        in_specs.append(pl.BlockSpec((1, s, LANES), lambda p, bi, i: (bi, 0, 0)))
        args.append(extra)
        scratch.append(pltpu.VMEM((s, 2 * LANES), F32))
    out_spec = pl.BlockSpec((1, tq * n_sub, LANES), im(lambda p, bi, i: (bi, i, p)))
    kern = functools.partial(_attn_kernel, variant=variant, window=window, seq=s, tq=tq, tk=tk,
                             n_sub=n_sub)
    params = pltpu.CompilerParams(dimension_semantics=("arbitrary",) * 3,
                                  vmem_limit_bytes=ATTN_VMEM_LIMIT)
    out_shape = jax.ShapeDtypeStruct((b, s, n_pairs * LANES), BF16)
    if sinks is not None:
        grid_spec = pltpu.PrefetchScalarGridSpec(
            num_scalar_prefetch=1, grid=(n_pairs, b, nq), in_specs=in_specs, out_specs=out_spec,
            scratch_shapes=scratch)
        return pl.pallas_call(kern, grid_spec=grid_spec, out_shape=out_shape,
                              compiler_params=params, name="attn_" + variant)(sinks, *args)
    return pl.pallas_call(kern, grid=(n_pairs, b, nq), in_specs=in_specs, out_specs=out_spec,
                          out_shape=out_shape, scratch_shapes=scratch,
                          compiler_params=params, name="attn_" + variant)(*args)


def _lead_rows(window, n_var, v):
    return max(window - TQ - (n_var - 1 - v) * TQ, 0)


def _swa_bias():
    window, left = SWA_WINDOW_KEYS, SWA_LEFT_CHUNKS
    c = jnp.arange(window)[:, None]
    a = jnp.arange(TQ)[None, :]
    dist = a - c + (window - TQ)
    ahead = (a // CHUNK + left) - c // CHUNK
    slopes = jnp.exp2(-8.0 * jnp.arange(1, N_HEADS + 1, dtype=F32) / N_HEADS)
    bias = -slopes[:, None, None] * jnp.abs(dist).astype(F32)[None] * LOG2E
    bias = jnp.where(((ahead >= 0) & (ahead <= left))[None], bias, MASK)
    bias = jnp.concatenate([bias[:4], bias[4:]], axis=2)
    n_var = 2
    return jnp.stack([jnp.where(c[None] >= _lead_rows(window, n_var, v), bias, MASK)
                      for v in range(n_var)], axis=1)


_REL_ROWS = 384


_CK_VARIANTS = 3


def _ck_bias_kernel(tab_ref, out_ref):
    window = CK_WINDOW_KEYS
    span = TQ + window
    tab = tab_ref[0] * LOG2E
    hi = tab.astype(BF16)
    rem = tab - hi.astype(F32)
    mid = rem.astype(BF16)
    lo = (rem - mid.astype(F32)).astype(BF16)
    t = lax.broadcasted_iota(jnp.int32, (_REL_ROWS, span), 0)
    n = lax.broadcasted_iota(jnp.int32, (_REL_ROWS, span), 1)
    delta = jnp.where(n < TQ, n, n - span)
    idx = jnp.clip(delta + (window - TQ), -(CHUNK - 1), REL_MAX) + (CHUNK - 1)
    onehot = jnp.where(t == idx, 1.0, 0.0).astype(BF16)
    base = _dot(hi, onehot) + _dot(mid, onehot) + _dot(lo, onehot)
    c = lax.broadcasted_iota(jnp.int32, (window, TQ), 0)
    a = lax.broadcasted_iota(jnp.int32, (window, TQ), 1)
    valid = ((a >> 6) + CK_LEFT_CHUNKS - (c >> 6)).astype(jnp.uint32) <= CK_LEFT_CHUNKS
    for h in range(2):
        rows = jnp.broadcast_to(base[h:h + 1, :], (window, span))
        tile = jnp.where(valid, pltpu.roll(rows, 0, 1, stride=1, stride_axis=0)[:, :TQ], MASK)
        for v in range(_CK_VARIANTS):
            lead = _lead_rows(window, _CK_VARIANTS, v)
            out_ref[0, v, :, h * TQ:(h + 1) * TQ] = jnp.where(c >= lead, tile, MASK)


def _ck_bias(rel_bias):
    tab = jnp.pad(rel_bias.T, ((0, 0), (0, _REL_ROWS - rel_bias.shape[0])))
    tab = jnp.pad(tab.reshape(N_HEADS // 2, 2, _REL_ROWS), ((0, 0), (0, 6), (0, 0)))
    return pl.pallas_call(
        _ck_bias_kernel,
        grid=(N_HEADS // 2,),
        in_specs=[pl.BlockSpec((1, 8, _REL_ROWS), lambda p: (p, 0, 0))],
        out_specs=pl.BlockSpec((1, _CK_VARIANTS, CK_WINDOW_KEYS, 2 * TQ), lambda p: (p, 0, 0, 0)),
        out_shape=jax.ShapeDtypeStruct((N_HEADS // 2, _CK_VARIANTS, CK_WINDOW_KEYS, 2 * TQ), F32),
        compiler_params=pltpu.CompilerParams(dimension_semantics=("arbitrary",),
                                             vmem_limit_bytes=VMEM_LIMIT),
        name="ck_bias",
    )(tab)


def _out_ffn_kernel(x_ref, oa_ref, ob_ref, woa_ref, wob_ref, g1_ref, b1_ref,
                    wg_ref, wu_ref, wd_ref, g2_ref, b2_ref, y_ref):
    mix = _dot(oa_ref[...], woa_ref[...]) + _dot(ob_ref[...], wob_ref[...])
    x1 = _layer_norm(DEEPNORM_ALPHA * x_ref[...] + mix, g1_ref[...], b1_ref[...])
    x1b = x1.astype(BF16)
    gate = _dot(x1b, wg_ref[...])
    up = _dot(x1b, wu_ref[...])
    hidden = (gate * jax.nn.sigmoid(gate) * up).astype(BF16)
    ffn = _dot(hidden, wd_ref[...])
    y_ref[...] = _layer_norm(DEEPNORM_ALPHA * x1 + ffn, g2_ref[...], b2_ref[...])


def _out_ffn(x2, o_a, o_b, w_out_a, w_out_b, g1, b1, w_gate, w_up, w_down, g2, b2):
    t = x2.shape[0]
    d_ff = w_gate.shape[1]
    row = lambda w: pl.BlockSpec((TM_FFN, w), lambda i: (i, 0))
    vec = _const_spec((1, D_MODEL))
    return pl.pallas_call(
        _out_ffn_kernel,
        grid=(t // TM_FFN,),
        in_specs=[row(D_MODEL), row(512), row(512), _const_spec((512, D_MODEL)),
                  _const_spec((512, D_MODEL)), vec, vec, _const_spec((D_MODEL, d_ff)),
                  _const_spec((D_MODEL, d_ff)), _const_spec((d_ff, D_MODEL)), vec, vec],
        out_specs=row(D_MODEL),
        out_shape=jax.ShapeDtypeStruct((t, D_MODEL), F32),
        compiler_params=pltpu.CompilerParams(dimension_semantics=("arbitrary",),
                                             vmem_limit_bytes=VMEM_LIMIT),
        name="out_ffn",
    )(x2, o_a, o_b, w_out_a.astype(BF16), w_out_b.astype(BF16), g1.reshape(1, -1), b1.reshape(1, -1),
      w_gate.astype(BF16), w_up.astype(BF16), w_down.astype(BF16), g2.reshape(1, -1), b2.reshape(1, -1))


def kernel(x, ab_w_in, ab_q_norm, ab_w_uq, ab_kv_norm, ab_w_ukv, ab_sinks, ab_w_out,
           cd_w_in, cd_b_forget, cd_rel_bias, cd_w_out,
           ln1_g, ln1_b, ffn_w_gate, ffn_w_up, ffn_w_down, ln2_g, ln2_b):
    b, s, d = x.shape
    assert d == D_MODEL and TQ == TK
    assert s % TM == 0 and s % (2 * FULL_TILE) == 0 and s % (TQ * BAND_SUB_BLOCKS) == 0
    t = b * s
    x2 = x.reshape(t, d)

    qa, ka, va, qs, ks, vs = _proj_ab(x2, ab_w_in[0], ab_q_norm[0], ab_w_uq[0], ab_kv_norm[0],
                                      ab_w_ukv[0], s)
    o_a = _attention("mla", qa, ka, va, 0, 0, 0)
    o_b = _attention("swa", qs, ks, vs, 0, 0, 0, extra=_swa_bias(), sinks=ab_sinks[0] * LOG2E)
    w_out = ab_w_out[0]
    w_out_b = w_out[512:].reshape(2, 4, HEAD_DIM, D_MODEL).transpose(1, 0, 2, 3).reshape(512, D_MODEL)
    x2 = _out_ffn(x2, o_a.reshape(t, 512), o_b.reshape(t, 512), w_out[:512], w_out_b,
                  ln1_g[0], ln1_b[0], ffn_w_gate[0], ffn_w_up[0], ffn_w_down[0], ln2_g[0], ln2_b[0])

    qkv, cum = _proj_cd(x2, cd_w_in[0], cd_b_forget[0], s)
    o_c = _attention("fox", qkv, qkv, qkv, 0, 4, 8, extra=cum.reshape(b, s, LANES))
    o_d = _attention("ck", qkv, qkv, qkv, 12, 16, 20, extra=_ck_bias(cd_rel_bias[0]))
    w_out = cd_w_out[0]
    x2 = _out_ffn(x2, o_c.reshape(t, 512), o_d.reshape(t, 512), w_out[:512], w_out[512:],
                  ln1_g[1], ln1_b[1], ffn_w_gate[1], ffn_w_up[1], ffn_w_down[1], ln2_g[1], ln2_b[1])
    return x2.reshape(b, s, d)
```
